```python
import math
import jax, jax.numpy as jnp
from jax import lax
import numpy as np

D_MODEL = 1024
BATCH = 16
SEQ = 256
DEPTH = 2
DEC_BATCH = 8
DEC_SEQ = 2048
PAST_LEN = 256

GRID_W = 64
EPS = 1e-6
ROPE_BASE = 10000.0
ATTN_BLOCK = 128
RET_HEADS = 4
RET_DK = 64
RET_DV = 64
RET_CHUNK = 128
FNET_GROUPS = 4
FNET_GC = 64
MLA_HEADS = 4
MLA_Q_LORA = 256
MLA_KV_LORA = 128
MLA_NOPE = 64
MLA_ROPE = 32
MLA_V = 64
HY_CH = 256
HY_ORDER = 2
HY_BANDS = 16
HY_EMB = 1 + 2 * HY_BANDS
HY_FFN = 64
HY_FAST_DECAY = 0.3
HY_SLOW_DECAY = 1.5
HY_TARGET = 1e-2
D_FF = ((8 * D_MODEL + 3 * 256 - 1) // (3 * 256)) * 256

RET_W = RET_HEADS * RET_DV
FNET_W = FNET_GROUPS * FNET_GC
MLA_W = MLA_HEADS * MLA_V
MIX_W = RET_W + FNET_W + MLA_W + HY_CH
IN_SIZES = (RET_HEADS * RET_DK, RET_HEADS * RET_DK, RET_W, RET_W, FNET_W,
            MLA_Q_LORA, MLA_KV_LORA, MLA_ROPE, 3 * HY_CH)
IN_W = sum(IN_SIZES)
F32 = jnp.float32

kernel_name = 'hybrid_retention_fnet_mla_hyena_prefix_step'


def rmsnorm(x, g):
    xf = x.astype(F32)
    y = xf * lax.rsqrt(jnp.mean(xf * xf, axis=-1, keepdims=True) + EPS)
    return (y * g.astype(F32)).astype(x.dtype)


def grid_positions(L):
    rows = L // GRID_W
    row = jnp.repeat(jnp.arange(rows, dtype=jnp.int32), GRID_W)
    col = jnp.tile(jnp.arange(GRID_W, dtype=jnp.int32), rows)
    return row, col


def axial_rope(x):
    L, R = x.shape[1], x.shape[-1]
    half = R // 4
    row, col = grid_positions(L)
    inv = ROPE_BASE ** (-jnp.arange(half, dtype=F32) / half)

    def rot(xa, pos):
        ang = pos.astype(F32)[:, None] * inv[None, :]
        cos = jnp.cos(ang)[None, :, None, :]
        sin = jnp.sin(ang)[None, :, None, :]
        x1, x2 = jnp.split(xa.astype(F32), 2, axis=-1)
        return jnp.concatenate([x1 * cos - x2 * sin, x2 * cos + x1 * sin], axis=-1)

    xr, xc = jnp.split(x, 2, axis=-1)
    return jnp.concatenate([rot(xr, row), rot(xc, col)], axis=-1).astype(x.dtype)


def retention_scan(q, k, v, log_gamma, s0):
    B, L, H, dk = q.shape
    dv = v.shape[-1]
    C = RET_CHUNK
    n = L // C
    qc = q.reshape(B, n, C, H, dk)
    kc = k.reshape(B, n, C, H, dk)
    vc = v.reshape(B, n, C, H, dv)
    i = jnp.arange(C, dtype=F32)
    diff = i[:, None] - i[None, :]
    dmask = jnp.where(diff[None] >= 0,
                      jnp.exp(jnp.maximum(diff, 0.0)[None] * log_gamma[:, None, None]), 0.0)
    scores = jnp.einsum('bnchd,bnmhd->bnhcm', qc, kc) * dmask[None, None]
    inner = jnp.einsum('bnhcm,bnmhe->bnche', scores, vc)
    xi = jnp.exp((i[:, None] + 1.0) * log_gamma[None, :])
    zeta = jnp.exp((C - 1.0 - i)[:, None] * log_gamma[None, :])
    g_chunk = jnp.exp(C * log_gamma)
    kv = jnp.einsum('bnchd,bnche->nbhde', kc * zeta[None, None, :, :, None], vc)

    def step(S, kv_j):
        return g_chunk[None, :, None, None] * S + kv_j, S

    s_final, s_prev = lax.scan(step, s0, kv)
    cross = jnp.einsum('bnchd,nbhde->bnche', qc * xi[None, None, :, :, None], s_prev)
    return (inner + cross).reshape(B, L, H, dv), s_final


def retention(rq, rk, rv, rg, decay, s0, latent):
    B, L, _ = rq.shape
    q = rq.reshape(B, L, RET_HEADS, RET_DK)
    k = rk.reshape(B, L, RET_HEADS, RET_DK) * (RET_DK ** -0.5)
    v = rv.reshape(B, L, RET_HEADS, RET_DV)
    if latent:
        q = axial_rope(q)
        k = axial_rope(k)
    q, k, v = q.astype(F32), k.astype(F32), v.astype(F32)
    log_g = jax.nn.log_sigmoid(decay.astype(F32))
    s0 = s0.astype(F32)
    o_f, s_f = retention_scan(q, k, v, log_g[0], s0[:, 0])
    o_b, s_b = retention_scan(q[:, ::-1], k[:, ::-1], v[:, ::-1], log_g[1], s0[:, 1])
    o = o_f + o_b[:, ::-1]
    mu = jnp.mean(o, axis=-1, keepdims=True)
    var = jnp.mean(jnp.square(o - mu), axis=-1, keepdims=True)
    o = ((o - mu) * lax.rsqrt(var + EPS)).reshape(B, L, RET_W)
    out = jax.nn.silu(rg.astype(F32)) * o
    return out.astype(rq.dtype), jnp.stack([s_f, s_b], axis=1)


def fourier_mix(f):
    B, L, _ = f.shape
    ff = f.astype(F32).reshape(B, L, FNET_GROUPS, FNET_GC)
    y = jnp.fft.fftn(ff, axes=(1, 3), norm='ortho').real
    return y.reshape(B, L, FNET_W).astype(f.dtype)


def mla_keys(ckv, kr, w_ukv):
    B, L, _ = ckv.shape
    kv = (ckv @ w_ukv).reshape(B, L, MLA_HEADS, MLA_NOPE + MLA_V)
    k_nope, v = jnp.split(kv, [MLA_NOPE], axis=-1)
    k = jnp.concatenate([k_nope, jnp.broadcast_to(kr, (B, L, MLA_HEADS, MLA_ROPE)).astype(k_nope.dtype)], axis=-1)
    return k, v


def block_attention(q, k, v):
    B, Lq, H, Dh = q.shape
    nb = Lq // ATTN_BLOCK
    scale = Dh ** -0.5
    qb = jnp.moveaxis(q.reshape(B, nb, ATTN_BLOCK, H, Dh), 1, 0)

    def one(qi):
        s = jnp.einsum('bqhd,bkhd->bhqk', qi, k, preferred_element_type=F32) * scale
        pr = jax.nn.softmax(s, axis=-1)
        return jnp.einsum('bhqk,bkhe->bqhe', pr.astype(v.dtype), v)

    o = lax.map(one, qb)
    return jnp.moveaxis(o, 0, 1).reshape(B, Lq, H, v.shape[-1])


def short_conv3(u, w, b):
    up = jnp.pad(u, ((0, 0), (1, 1), (0, 0)))
    return up[:, :-2] * w[0] + up[:, 1:-1] * w[1] + up[:, 2:] * w[2] + b


def hyena_filters(L, w1, b1, w2, b2, w3):
    pos = jnp.arange(L, dtype=F32)
    t = pos / L
    bands = jnp.arange(1, HY_BANDS + 1, dtype=F32)
    ang = (2.0 * math.pi / L) * pos[:, None] * bands[None, :]
    z = jnp.concatenate([t[:, None], jnp.sin(ang), jnp.cos(ang)], axis=-1)
    h = jnp.sin(z @ w1.astype(F32) + b1.astype(F32))
    h = jnp.sin(h @ w2.astype(F32) + b2.astype(F32))
    h = (h @ w3.astype(F32)).reshape(L, HY_ORDER, 2, HY_CH)
    deltas = jnp.abs(jnp.linspace(math.log(HY_TARGET) / HY_SLOW_DECAY,
                                  math.log(HY_TARGET) / HY_FAST_DECAY, HY_CH, dtype=F32))
    window = jnp.exp(-t[:, None] * deltas[None, :])
    return h * window[:, None, None, :]


def long_conv(u, hf, hb):
    L, C = hf.shape
    g = jnp.concatenate([hf, jnp.zeros((1, C), F32), hb[1:][::-1]], axis=0)
    g = g / (jnp.sum(jnp.abs(g), axis=0, keepdims=True) + EPS)
    U = jnp.fft.rfft(u, n=2 * L, axis=1)
    G = jnp.fft.rfft(g, n=2 * L, axis=0)
    return jnp.fft.irfft(U * G[None], n=2 * L, axis=1)[:, :L]


def hyena(u, p):
    B, L, _ = u.shape
    uc = short_conv3(u, p['hy_short_w'], p['hy_short_b']).astype(F32)
    v, x1, x2 = jnp.split(uc, 3, axis=-1)
    h = hyena_filters(L, p['hy_w1'], p['hy_b1'], p['hy_w2'], p['hy_b2'], p['hy_w3'])
    d_skip = p['hy_bias'].astype(F32)
    z = v
    for o, gate in enumerate((x1, x2)):
        z = gate * (long_conv(z, h[:, o, 0], h[:, o, 1]) + d_skip[o] * z)
    return z.astype(u.dtype)


def mixer(h, p, latent, ctx_ckv, ctx_krope, s0):
    B, L, _ = h.shape
    idx = np.cumsum(IN_SIZES)[:-1].tolist()
    rq, rk, rv, rg, fu, cq, ckv, kr, hu = jnp.split(h @ p['w_in'], idx, axis=-1)
    ret_out, s_ret = retention(rq, rk, rv, rg, p['ret_decay'], s0, latent)
    four_out = fourier_mix(fu)
    cq = rmsnorm(cq, p['mla_q_norm'])
    ckv = rmsnorm(ckv, p['mla_kv_norm'])
    q = (cq @ p['mla_w_uq']).reshape(B, L, MLA_HEADS, MLA_NOPE + MLA_ROPE)
    q_nope, q_rope = jnp.split(q, [MLA_NOPE], axis=-1)
    kr_h = kr[:, :, None, :]
    kr_rot = axial_rope(kr_h) if latent else kr_h
    if latent:
        q_rope = axial_rope(q_rope)
    q = jnp.concatenate([q_nope, q_rope], axis=-1)
    k, v = mla_keys(ckv, kr_rot, p['mla_w_ukv'])
    if latent:
        k_c, v_c = mla_keys(ctx_ckv, ctx_krope[:, :, None, :], p['mla_w_ukv'])
        k = jnp.concatenate([k, k_c], axis=1)
        v = jnp.concatenate([v, v_c], axis=1)
    att = block_attention(q, k, v).reshape(B, L, MLA_W)
    hy = hyena(hu, p)
    out = jnp.concatenate([ret_out, four_out, att, hy], axis=-1) @ p['w_out']
    return out, ckv, kr, s_ret


def layer(x, mod, p, latent, ctx_ckv, ctx_krope, s0):
    sh1, sc1, g1, sh2, sc2, g2 = jnp.split(mod.astype(x.dtype), 6, axis=-1)
    norm = p['norm_g']
    h = rmsnorm(x, norm[0]) * (1.0 + sc1) + sh1
    m, ckv, kr, s_ret = mixer(h, p, latent, ctx_ckv, ctx_krope, s0)
    x = x + g1 * rmsnorm(m, norm[1])
    h = rmsnorm(x, norm[2]) * (1.0 + sc2) + sh2
    f = (jax.nn.silu(h @ p['w_gate']) * (h @ p['w_up'])) @ p['w_down']
    x = x + g2 * rmsnorm(f, norm[3])
    return x, ckv, kr, s_ret


def setup_inputs(seed: int = 0) -> dict:
    key = jax.random.key(seed)
    ks = jax.random.split(key, 28)
    nrm = lambda k, shape, s: jax.random.normal(k, shape, F32) * s
    ret_init = jnp.log(jnp.exp2(5.0 + jnp.arange(RET_HEADS, dtype=F32)) - 1.0)
    return {
        'x_prompt': nrm(ks[0], (BATCH, SEQ, D_MODEL), 1.0),
        'x_sample': nrm(ks[1], (DEC_BATCH, DEC_SEQ, D_MODEL), 1.0),
        'cache_ckv': nrm(ks[2], (DEC_BATCH, DEPTH, PAST_LEN, MLA_KV_LORA), 1.0),
        'cache_krope': nrm(ks[3], (DEC_BATCH, DEPTH, PAST_LEN, MLA_ROPE), 1.0),
        'state_ret': nrm(ks[4], (DEC_BATCH, DEPTH, 2, RET_HEADS, RET_DK, RET_DV), 0.5),
        'c': nrm(ks[5], (DEC_BATCH, D_MODEL), 1.0),
        'c_ctx': nrm(ks[6], (D_MODEL,), 1.0),
        'w_ada': nrm(ks[7], (DEPTH, D_MODEL, 6 * D_MODEL), 0.5 * D_MODEL ** -0.5),
        'b_ada': nrm(ks[8], (DEPTH, 6 * D_MODEL), 0.01),
        'norm_g': 1.0 + nrm(ks[9], (DEPTH, 4, D_MODEL), 0.01),
        'w_in': nrm(ks[10], (DEPTH, D_MODEL, IN_W), D_MODEL ** -0.5),
        'w_out': nrm(ks[11], (DEPTH, MIX_W, D_MODEL), MIX_W ** -0.5),
        'ret_decay': ret_init + nrm(ks[12], (DEPTH, 2, RET_HEADS), 0.1),
        'mla_q_norm': 1.0 + nrm(ks[13], (DEPTH, MLA_Q_LORA), 0.01),
        'mla_kv_norm': 1.0 + nrm(ks[14], (DEPTH, MLA_KV_LORA), 0.01),
        'mla_w_uq': nrm(ks[15], (DEPTH, MLA_Q_LORA, MLA_HEADS * (MLA_NOPE + MLA_ROPE)), MLA_Q_LORA ** -0.5),
        'mla_w_ukv': nrm(ks[16], (DEPTH, MLA_KV_LORA, MLA_HEADS * (MLA_NOPE + MLA_V)), MLA_KV_LORA ** -0.5),
        'hy_short_w': nrm(ks[17], (DEPTH, 3, 3 * HY_CH), 3 ** -0.5),
        'hy_short_b': nrm(ks[18], (DEPTH, 3 * HY_CH), 0.01),
        'hy_w1': nrm(ks[19], (DEPTH, HY_EMB, HY_FFN), HY_EMB ** -0.5),
        'hy_b1': nrm(ks[20], (DEPTH, HY_FFN), 0.1),
        'hy_w2': nrm(ks[21], (DEPTH, HY_FFN, HY_FFN), HY_FFN ** -0.5),
        'hy_b2': nrm(ks[22], (DEPTH, HY_FFN), 0.1),
        'hy_w3': nrm(ks[23], (DEPTH, HY_FFN, HY_ORDER * 2 * HY_CH), HY_FFN ** -0.5),
        'hy_bias': nrm(ks[24], (DEPTH, HY_ORDER, HY_CH), 0.1),
        'w_gate': nrm(ks[25], (DEPTH, D_MODEL, D_FF), D_MODEL ** -0.5),
        'w_up': nrm(ks[26], (DEPTH, D_MODEL, D_FF), D_MODEL ** -0.5),
        'w_down': nrm(ks[27], (DEPTH, D_FF, D_MODEL), D_FF ** -0.5),
    }


def reference(x_prompt, x_sample, cache_ckv, cache_krope, state_ret, c, c_ctx,
              w_ada, b_ada, norm_g, w_in, w_out, ret_decay, mla_q_norm, mla_kv_norm,
              mla_w_uq, mla_w_ukv, hy_short_w, hy_short_b, hy_w1, hy_b1, hy_w2, hy_b2,
              hy_w3, hy_bias, w_gate, w_up, w_down):
    xp = x_prompt
    xs = x_sample
    s_zero = jnp.zeros((xp.shape[0], 2, RET_HEADS, RET_DK, RET_DV), F32)
    silu_ctx = jax.nn.silu(c_ctx.astype(F32))
    silu_c = jax.nn.silu(c.astype(F32))
    new_ckv, new_kr, new_s = [], [], []
    for l in range(DEPTH):
        p = {
            'norm_g': norm_g[l], 'w_in': w_in[l], 'w_out': w_out[l], 'ret_decay': ret_decay[l],
            'mla_q_norm': mla_q_norm[l], 'mla_kv_norm': mla_kv_norm[l],
            'mla_w_uq': mla_w_uq[l], 'mla_w_ukv': mla_w_ukv[l],
            'hy_short_w': hy_short_w[l], 'hy_short_b': hy_short_b[l],
            'hy_w1': hy_w1[l], 'hy_b1': hy_b1[l], 'hy_w2': hy_w2[l], 'hy_b2': hy_b2[l],
            'hy_w3': hy_w3[l], 'hy_bias': hy_bias[l],
            'w_gate': w_gate[l], 'w_up': w_up[l], 'w_down': w_down[l],
        }
        wa = w_ada[l].astype(F32)
        ba = b_ada[l].astype(F32)
        mod_ctx = (silu_ctx @ wa + ba)[None, None, :]
        mod_lat = (silu_c @ wa + ba)[:, None, :]
        xp, ckv, kr, s_ret = layer(xp, mod_ctx, p, False, None, None, s_zero)
        new_ckv.append(ckv)
        new_kr.append(kr)
        new_s.append(s_ret.astype(xp.dtype))
        xs, _, _, _ = layer(xs, mod_lat, p, True, cache_ckv[:, l], cache_krope[:, l], state_ret[:, l])
    return (xp, xs, jnp.stack(new_ckv, axis=1), jnp.stack(new_kr, axis=1), jnp.stack(new_s, axis=1))
```

```python
import functools
import math

import numpy as np
import jax
import jax.numpy as jnp
from jax import lax
from jax.experimental import pallas as pl
from jax.experimental.pallas import tpu as pltpu

F32 = jnp.float32
BF16 = jnp.bfloat16

D_MODEL = 1024
DEPTH = 2
GRID_W = 64
EPS = 1e-6
ROPE_BASE = 10000.0
RET_HEADS = 4
RET_DK = 64
RET_DV = 64
RET_W = RET_HEADS * RET_DV
FNET_GROUPS = 4
FNET_GC = 64
FNET_W = FNET_GROUPS * FNET_GC
MLA_HEADS = 4
MLA_Q_LORA = 256
MLA_KV_LORA = 128
MLA_NOPE = 64
MLA_ROPE = 32
MLA_V = 64
MLA_W = MLA_HEADS * MLA_V
MLA_HP = 128
MLA_ROPE_OFF = MLA_NOPE
HY_CH = 256
HY_ORDER = 2
HY_BANDS = 16
HY_EMB = 1 + 2 * HY_BANDS
HY_EMB_PAD = 128
HY_FFN = 64
HY_FAST_DECAY = 0.3
HY_SLOW_DECAY = 1.5
HY_TARGET = 1e-2
D_FF = ((8 * D_MODEL + 3 * 256 - 1) // (3 * 256)) * 256
MOD_ROWS = 16
CTX_ROW = 8

OFF_RET = 0
OFF_FU = 4 * RET_W
OFF_CQ = OFF_FU + FNET_W
OFF_CKV = OFF_CQ + MLA_Q_LORA
OFF_KR = OFF_CKV + MLA_KV_LORA
OFF_HU = OFF_KR + MLA_HP
IN_WP = OFF_HU + 3 * HY_CH

VMEM_LIMIT = 52 * 1024 * 1024


def _cparams(sem):
    return pltpu.CompilerParams(dimension_semantics=sem, vmem_limit_bytes=VMEM_LIMIT)


def _tiles(L):
    return dict(
        tm=min(L, 256),
        chunk=min(L, 256),
        tq=min(L, 256),
        hb=min(L, 512),
        tcol=1024,
    )


def _bf16_const(a):
    return jnp.asarray(np.asarray(a, np.float32)).astype(BF16)


@functools.lru_cache(maxsize=None)
def _rope_np(L, width, seg_off, seg_w, rope_dim):
    row = np.repeat(np.arange(L // GRID_W), GRID_W).astype(np.float64)
    col = np.tile(np.arange(GRID_W), L // GRID_W).astype(np.float64)
    quarter = rope_dim // 4
    inv = ROPE_BASE ** (-np.arange(quarter, dtype=np.float64) / quarter)
    cos = np.ones((L, width), np.float64)
    sin = np.zeros((L, width), np.float64)
    for lane in range(width):
        r = lane % seg_w - seg_off
        if r < 0 or r >= rope_dim:
            continue
        pos = row if r < rope_dim // 2 else col
        ang = pos * inv[r % quarter]
        first = (r % (rope_dim // 2)) < quarter
        cos[:, lane] = np.cos(ang)
        sin[:, lane] = -np.sin(ang) if first else np.sin(ang)
    return cos.astype(np.float32), sin.astype(np.float32)


@functools.lru_cache(maxsize=None)
def _chan_dft_np():
    k = np.arange(FNET_GC)
    ang = 2.0 * np.pi * np.outer(k, k) / FNET_GC
    c = np.kron(np.eye(FNET_GROUPS), np.cos(ang))
    s = np.kron(np.eye(FNET_GROUPS), np.sin(ang))
    return np.concatenate([c, s], axis=1)


@functools.lru_cache(maxsize=None)
def _pos_dft_np(L):
    k = np.arange(L, dtype=np.int64)
    ang = 2.0 * np.pi * (np.outer(k, k) % L) / L
    return np.concatenate([np.cos(ang), -np.sin(ang)], axis=1)


@functools.lru_cache(maxsize=None)
def _hyena_dft_np(L, hb):
    k = np.arange(L, dtype=np.int64)
    ang = np.pi * (np.outer(k, k) % (2 * L)) / L
    cf = np.cos(ang)
    sf = np.sin(ang)
    alt = (-1.0) ** np.arange(L)
    sf[0, :] = alt
    w = np.where(k == 0, 1.0, 2.0)[None, :] / (2.0 * L)
    ainv = cf.T * w
    binv = sf.T / L
    binv[:, 0] = alt / (2.0 * L)
    fw, iv = [], []
    for t in range(L // hb):
        sl = slice(t * hb, (t + 1) * hb)
        fw += [cf[sl], sf[sl]]
        iv += [ainv[:, sl], binv[:, sl]]
    return np.concatenate(fw, axis=0), np.concatenate(iv, axis=1)


@functools.lru_cache(maxsize=None)
def _hyena_feat_np(L):
    pos = np.arange(L, dtype=np.float64)
    t = pos / L
    bands = np.arange(1, HY_BANDS + 1, dtype=np.float64)
    ang = (2.0 * math.pi / L) * pos[:, None] * bands[None, :]
    z = np.zeros((L, HY_EMB_PAD), np.float64)
    z[:, 0] = t
    z[:, 1:1 + HY_BANDS] = np.sin(ang)
    z[:, 1 + HY_BANDS:HY_EMB] = np.cos(ang)
    deltas = np.abs(np.linspace(math.log(HY_TARGET) / HY_SLOW_DECAY,
                                math.log(HY_TARGET) / HY_FAST_DECAY, HY_CH))
    window = np.exp(-t[:, None] * deltas[None, :])
    return z.astype(np.float32), window.astype(np.float32)


def _rms(x, g):
    return x * lax.rsqrt(jnp.mean(x * x, axis=-1, keepdims=True) + EPS) * g


def _silu(x):
    return x * jax.nn.sigmoid(x)


def _bdot(a, b):
    return jnp.dot(a.astype(BF16), b.astype(BF16), preferred_element_type=F32)


def _split(a):
    hi = a.astype(BF16)
    lo = (a - hi.astype(F32)).astype(BF16)
    return hi, lo


def _dot3(a, b):
    ah, al = _split(a)
    bh, bl = _split(b)
    d = lambda x, y: jnp.dot(x, y, preferred_element_type=F32)
    return d(ah, bh) + (d(al, bh) + d(ah, bl))


def _rope(x, cos, sin_signed, quarter):
    w = x.shape[-1]
    lane = lax.broadcasted_iota(jnp.int32, x.shape, 1)
    first = (lane % (2 * quarter)) < quarter
    up = pltpu.roll(x, w - quarter, axis=1)
    dn = pltpu.roll(x, quarter, axis=1)
    return x * cos + jnp.where(first, up, dn) * sin_signed


def _ada_kernel(c_ref, w_ref, b_ref, o_ref):
    s = _silu(c_ref[...])
    o_ref[...] = _bdot(s, w_ref[...]) + b_ref[...]


def _ada(cvec, w_ada, b_ada):
    tn = 1024
    n6 = w_ada.shape[-1]
    return pl.pallas_call(
        _ada_kernel,
        grid=(DEPTH, n6 // tn),
        in_specs=[
            pl.BlockSpec((MOD_ROWS, D_MODEL), lambda l, j: (0, 0)),
            pl.BlockSpec((None, D_MODEL, tn), lambda l, j: (l, 0, j)),
            pl.BlockSpec((None, 1, tn), lambda l, j: (l, 0, j)),
        ],
        out_specs=pl.BlockSpec((None, MOD_ROWS, tn), lambda l, j: (l, 0, j)),
        out_shape=jax.ShapeDtypeStruct((DEPTH, MOD_ROWS, n6), F32),
        compiler_params=_cparams(("parallel", "parallel")),
        name="ada",
    )(cvec, w_ada, b_ada.reshape(DEPTH, 1, n6))


def _in_proj_kernel(*refs, latent):
    (x_ref, mod_ref, g0_ref, w_ref, qn_ref, kvn_ref, wuq_ref, wukv_ref, bd_ref) = refs[:9]
    pos = 9
    if latent:
        cr_ref, sr_ref, cq_ref, sq_ref, ck_ref, sk_ref = refs[pos:pos + 6]
        pos += 6
    qkvg_ref, fx_ref, q_ref, k_ref, v_ref, ckv_ref, kr_ref, hu_ref = refs[pos:pos + 8]

    x = x_ref[...]
    mod = mod_ref[...]
    sh1 = mod[:, 0:D_MODEL]
    sc1 = mod[:, D_MODEL:2 * D_MODEL]
    h = _rms(x, g0_ref[...]) * (1.0 + sc1) + sh1
    proj = jnp.dot(h.astype(BF16), w_ref[...], preferred_element_type=F32)

    rq = proj[:, OFF_RET:OFF_RET + RET_W]
    rk = proj[:, OFF_RET + RET_W:OFF_RET + 2 * RET_W] * (RET_DK ** -0.5)
    if latent:
        rq = _rope(rq, cr_ref[...], sr_ref[...], RET_DK // 4)
        rk = _rope(rk, cr_ref[...], sr_ref[...], RET_DK // 4)
    qkvg_ref[:, 0:RET_W] = rq
    qkvg_ref[:, RET_W:2 * RET_W] = rk
    qkvg_ref[:, 2 * RET_W:4 * RET_W] = proj[:, OFF_RET + 2 * RET_W:OFF_RET + 4 * RET_W]

    fcs = _bdot(proj[:, OFF_FU:OFF_FU + FNET_W], bd_ref[...])
    fx_ref[0] = fcs[:, :FNET_W].astype(fx_ref.dtype)
    fx_ref[1] = fcs[:, FNET_W:].astype(fx_ref.dtype)

    cqn = _rms(proj[:, OFF_CQ:OFF_CQ + MLA_Q_LORA], qn_ref[...])
    q = _bdot(cqn, wuq_ref[...])
    ckvn = _rms(proj[:, OFF_CKV:OFF_CKV + MLA_KV_LORA], kvn_ref[...])
    kv = _bdot(ckvn, wukv_ref[...])
    krp = proj[:, OFF_KR:OFF_KR + MLA_HP]
    ckv_ref[...] = ckvn
    kr_ref[...] = krp
    if latent:
        q = _rope(q, cq_ref[...], sq_ref[...], MLA_ROPE // 4)
        krp = _rope(krp, ck_ref[...], sk_ref[...], MLA_ROPE // 4)
    scale = (MLA_NOPE + MLA_ROPE) ** -0.5
    q_ref[...] = (q * scale).astype(q_ref.dtype)
    k_ref[...] = (kv[:, :MLA_HEADS * MLA_HP] + jnp.concatenate([krp] * MLA_HEADS, axis=1)).astype(k_ref.dtype)
    v_ref[...] = kv[:, MLA_HEADS * MLA_HP:].astype(v_ref.dtype)

    hu_ref[...] = proj[:, OFF_HU:OFF_HU + 3 * HY_CH]


def _in_proj(x, mod, l, p, latent):
    B, L, _ = x.shape
    t = _tiles(L)
    tm = t["tm"]
    nt = L // tm
    row = (lambda b: b) if latent else (lambda b: CTX_ROW)
    full = lambda shape: pl.BlockSpec(shape, lambda b, i: (0,) * len(shape))
    in_specs = [
        pl.BlockSpec((None, tm, D_MODEL), lambda b, i: (b, i, 0)),
        pl.BlockSpec((None, None, 1, 6 * D_MODEL), lambda b, i: (l, row(b), 0, 0)),
        full((1, D_MODEL)),
        full((D_MODEL, IN_WP)),
        full((1, MLA_Q_LORA)),
        full((1, MLA_KV_LORA)),
        full((MLA_Q_LORA, MLA_HEADS * MLA_HP)),
        full((MLA_KV_LORA, MLA_HEADS * MLA_HP + MLA_W)),
        full((FNET_W, 2 * FNET_W)),
    ]
    args = [x, mod, p["g0"], p["w_in"], p["q_norm"], p["kv_norm"], p["w_uq"], p["w_ukv"],
            _bf16_const(_chan_dft_np())]
    if latent:
        tabs = (_rope_np(L, RET_W, 0, RET_DK, RET_DK)
                + _rope_np(L, MLA_HEADS * MLA_HP, MLA_ROPE_OFF, MLA_HP, MLA_ROPE)
                + _rope_np(L, MLA_HP, MLA_ROPE_OFF, MLA_HP, MLA_ROPE))
        for tab in tabs:
            in_specs.append(pl.BlockSpec((tm, tab.shape[1]), lambda b, i: (i, 0)))
            args.append(jnp.asarray(tab))
    bm = lambda w: pl.BlockSpec((None, tm, w), lambda b, i: (b, i, 0))
    out_specs = [
        bm(4 * RET_W),
        pl.BlockSpec((2, tm, FNET_W), lambda b, i: (0, i, b)),
        bm(MLA_HEADS * MLA_HP), bm(MLA_HEADS * MLA_HP), bm(MLA_W),
        bm(MLA_KV_LORA), bm(MLA_HP),
        pl.BlockSpec((tm, 3 * HY_CH), lambda b, i: (i, b)),
    ]
    out_shape = [
        jax.ShapeDtypeStruct((B, L, 4 * RET_W), F32),
        jax.ShapeDtypeStruct((2, L, B * FNET_W), BF16),
        jax.ShapeDtypeStruct((B, L, MLA_HEADS * MLA_HP), BF16),
        jax.ShapeDtypeStruct((B, L, MLA_HEADS * MLA_HP), BF16),
        jax.ShapeDtypeStruct((B, L, MLA_W), BF16),
        jax.ShapeDtypeStruct((B, L, MLA_KV_LORA), F32),
        jax.ShapeDtypeStruct((B, L, MLA_HP), F32),
        jax.ShapeDtypeStruct((L, B * 3 * HY_CH), F32),
    ]
    return pl.pallas_call(
        functools.partial(_in_proj_kernel, latent=latent),
        grid=(B, nt), in_specs=in_specs, out_specs=out_specs, out_shape=out_shape,
        compiler_params=_cparams(("parallel", "parallel")),
        name="in_proj_lat" if latent else "in_proj_ctx",
    )(*args)


def _ctx_kv_kernel(ckv_ref, kr_ref, w_ref, k_ref, v_ref):
    kv = _bdot(ckv_ref[...], w_ref[...])
    k_ref[...] = (kv[:, :MLA_HEADS * MLA_HP]
                  + jnp.concatenate([kr_ref[...]] * MLA_HEADS, axis=1)).astype(k_ref.dtype)
    v_ref[...] = kv[:, MLA_HEADS * MLA_HP:].astype(v_ref.dtype)


def _ctx_kv(cache_ckv, krope_pad, l, w_ukv):
    B, _, P, _ = cache_ckv.shape
    return pl.pallas_call(
        _ctx_kv_kernel,
        grid=(B,),
        in_specs=[
            pl.BlockSpec((None, None, P, MLA_KV_LORA), lambda b: (b, l, 0, 0)),
            pl.BlockSpec((None, None, P, MLA_HP), lambda b: (b, l, 0, 0)),
            pl.BlockSpec(w_ukv.shape, lambda b: (0, 0)),
        ],
        out_specs=[
            pl.BlockSpec((None, P, MLA_HEADS * MLA_HP), lambda b: (b, 0, 0)),
            pl.BlockSpec((None, P, MLA_W), lambda b: (b, 0, 0)),
        ],
        out_shape=[
            jax.ShapeDtypeStruct((B, P, MLA_HEADS * MLA_HP), BF16),
            jax.ShapeDtypeStruct((B, P, MLA_W), BF16),
        ],
        compiler_params=_cparams(("parallel",)),
        name="ctx_kv",
    )(cache_ckv, krope_pad, w_ukv)


def _ret_kernel(*refs, L, C, has_s0, emit_state):
    q_ref, k_ref, v_ref, g_ref, dec_ref = refs[:5]
    pos = 5
    if has_s0:
        s0_ref = refs[pos]
        pos += 1
    o_ref = refs[pos]
    pos += 1
    if emit_state:
        sn_ref = refs[pos]
        pos += 1
    sf_ref, sb_ref, dm_ref, dec_tab = refs[pos:pos + 4]
    n = L // C
    H = RET_HEADS

    d = dec_ref[...]
    lg = jnp.minimum(d, 0.0) - jnp.log1p(jnp.exp(-jnp.abs(d)))
    ii = lax.broadcasted_iota(jnp.int32, (C, C), 0)
    jj = lax.broadcasted_iota(jnp.int32, (C, C), 1)
    diff = (ii - jj).astype(F32)
    ci = lax.broadcasted_iota(jnp.int32, (C, RET_DK), 0).astype(F32)
    for h in range(H):
        lf = lg[h:h + 1, 0:1]
        lb = lg[H + h:H + h + 1, 0:1]
        dm_ref[h] = jnp.where(diff > 0, jnp.exp(diff * lf),
                              jnp.where(diff < 0, jnp.exp(-diff * lb), 2.0))
        dec_tab[0, h] = jnp.exp((ci + 1.0) * lf)
        dec_tab[1, h] = jnp.exp((C - ci) * lb)
        dec_tab[2, h] = jnp.exp((C - 1.0 - ci) * lf)
        dec_tab[3, h] = jnp.exp(ci * lb)
        if has_s0:
            sf_ref[0, h] = s0_ref[0, h]
            sb_ref[n, h] = s0_ref[1, h]
        else:
            sf_ref[0, h] = jnp.zeros((RET_DK, RET_DV), F32)
            sb_ref[n, h] = jnp.zeros((RET_DK, RET_DV), F32)

    tdot = lambda a, b: lax.dot_general(a.astype(BF16), b.astype(BF16), (((0,), (0,)), ((), ())),
                                        preferred_element_type=F32)

    def states(j, carry):
        jb = n - 1 - j
        rf = pl.multiple_of(j * C, C)
        rb = pl.multiple_of(jb * C, C)
        for h in range(H):
            hk = slice(h * RET_DK, (h + 1) * RET_DK)
            hv = slice(h * RET_DV, (h + 1) * RET_DV)
            g_f = jnp.exp(C * lg[h:h + 1, 0:1])
            g_b = jnp.exp(C * lg[H + h:H + h + 1, 0:1])
            kf = k_ref[pl.ds(rf, C), hk] * dec_tab[2, h]
            sf_ref[j + 1, h] = g_f * sf_ref[j, h] + tdot(kf, v_ref[pl.ds(rf, C), hv])
            kb = k_ref[pl.ds(rb, C), hk] * dec_tab[3, h]
            sb_ref[jb, h] = g_b * sb_ref[jb + 1, h] + tdot(kb, v_ref[pl.ds(rb, C), hv])
        return carry

    lax.fori_loop(0, n, states, 0)

    def chunk(j, carry):
        r0 = pl.multiple_of(j * C, C)
        for h in range(H):
            hk = slice(h * RET_DK, (h + 1) * RET_DK)
            hv = slice(h * RET_DV, (h + 1) * RET_DV)
            qb = q_ref[pl.ds(r0, C), hk].astype(BF16)
            kb = k_ref[pl.ds(r0, C), hk].astype(BF16)
            vb = v_ref[pl.ds(r0, C), hv].astype(BF16)
            s = lax.dot_general(qb, kb, (((1,), (1,)), ((), ())), preferred_element_type=F32)
            o = jnp.dot((s * dm_ref[h]).astype(BF16), vb, preferred_element_type=F32)
            o = o + dec_tab[0, h] * jnp.dot(qb, sf_ref[j, h].astype(BF16), preferred_element_type=F32)
            o = o + dec_tab[1, h] * jnp.dot(qb, sb_ref[j + 1, h].astype(BF16), preferred_element_type=F32)
            mu = jnp.mean(o, axis=-1, keepdims=True)
            dlt = o - mu
            var = jnp.mean(dlt * dlt, axis=-1, keepdims=True)
            gate = g_ref[pl.ds(r0, C), hv]
            o_ref[pl.ds(r0, C), hv] = (_silu(gate) * (dlt * lax.rsqrt(var + EPS))).astype(o_ref.dtype)
        return carry

    lax.fori_loop(0, n, chunk, 0)

    if emit_state:
        for h in range(H):
            sn_ref[0, h] = sf_ref[n, h]
            sn_ref[1, h] = sb_ref[0, h]


def _retention(qkvg, dec_rows, s0, l, emit_state):
    B, L, _ = qkvg.shape
    C = _tiles(L)["chunk"]
    n = L // C
    has_s0 = s0 is not None
    def colspec(j):
        return pl.BlockSpec((None, L, RET_W), lambda b: (b, 0, j))

    in_specs = [colspec(j) for j in range(4)]
    in_specs.append(pl.BlockSpec((None, 8, 128), lambda b: (l, 0, 0)))
    args = [qkvg, qkvg, qkvg, qkvg, dec_rows]
    st_block = (None, None, 2, RET_HEADS, RET_DK, RET_DV)
    if has_s0:
        in_specs.append(pl.BlockSpec(st_block, lambda b: (b, l, 0, 0, 0, 0)))
        args.append(s0)
    out_specs = [pl.BlockSpec((None, L, RET_W), lambda b: (b, 0, 0))]
    out_shape = [jax.ShapeDtypeStruct((B, L, RET_W), BF16)]
    if emit_state:
        out_specs.append(pl.BlockSpec((None, 2, RET_HEADS, RET_DK, RET_DV), lambda b: (b, 0, 0, 0, 0)))
        out_shape.append(jax.ShapeDtypeStruct((B, 2, RET_HEADS, RET_DK, RET_DV), F32))
    outs = pl.pallas_call(
        functools.partial(_ret_kernel, L=L, C=C, has_s0=has_s0, emit_state=emit_state),
        grid=(B,), in_specs=in_specs, out_specs=out_specs, out_shape=out_shape,
        scratch_shapes=[
            pltpu.VMEM((n + 1, RET_HEADS, RET_DK, RET_DV), F32),
            pltpu.VMEM((n + 1, RET_HEADS, RET_DK, RET_DV), F32),
            pltpu.VMEM((RET_HEADS, C, C), F32),
            pltpu.VMEM((4, RET_HEADS, C, RET_DK), F32),
        ],
        compiler_params=_cparams(("parallel",)),
        name="retention_lat" if has_s0 else "retention_ctx",
    )(*args)
    return outs if emit_state else (outs[0], None)


def _mm_kernel(a_ref, b_ref, o_ref, *, scale):
    o_ref[...] = (jnp.dot(a_ref[...], b_ref[...], preferred_element_type=F32) * scale).astype(o_ref.dtype)


def _mm(a, b, scale, out_dtype, tm, tn, name):
    M, K = a.shape
    _, N = b.shape
    tm, tn = min(tm, M), min(tn, N)
    return pl.pallas_call(
        functools.partial(_mm_kernel, scale=scale),
        grid=(N // tn, M // tm),
        in_specs=[pl.BlockSpec((tm, K), lambda j, i: (i, 0)),
                  pl.BlockSpec((K, tn), lambda j, i: (0, j))],
        out_specs=pl.BlockSpec((tm, tn), lambda j, i: (i, j)),
        out_shape=jax.ShapeDtypeStruct((M, N), out_dtype),
        compiler_params=_cparams(("parallel", "parallel")),
        name=name,
    )(a, b)


def _attn_kernel(*refs, has_ctx):
    if has_ctx:
        q_ref, k_ref, v_ref, kc_ref, vc_ref, o_ref = refs
    else:
        q_ref, k_ref, v_ref, o_ref = refs
    nt = lambda a, b: lax.dot_general(a, b, (((1,), (1,)), ((), ())), preferred_element_type=F32)
    for h in range(MLA_HEADS):
        hq = slice(h * MLA_HP, (h + 1) * MLA_HP)
        hv = slice(h * MLA_V, (h + 1) * MLA_V)
        qh = q_ref[:, hq]
        s = nt(qh, k_ref[:, hq])
        m = jnp.max(s, axis=-1, keepdims=True)
        if has_ctx:
            sc = nt(qh, kc_ref[:, hq])
            m = jnp.maximum(m, jnp.max(sc, axis=-1, keepdims=True))
        p = jnp.exp(s - m)
        den = jnp.sum(p, axis=-1, keepdims=True)
        o = jnp.dot(p.astype(BF16), v_ref[...], preferred_element_type=F32)
        if has_ctx:
            pc = jnp.exp(sc - m)
            den = den + jnp.sum(pc, axis=-1, keepdims=True)
            o = o + jnp.dot(pc.astype(BF16), vc_ref[...], preferred_element_type=F32)
        o_ref[:, hv] = (o[:, hv] / den).astype(o_ref.dtype)


def _attention(q, k, v, kc, vc):
    B, L, _ = q.shape
    tq = _tiles(L)["tq"]
    has_ctx = kc is not None
    whole = lambda a: pl.BlockSpec((None,) + a.shape[1:], lambda b, i: (b, 0, 0))
    in_specs = [pl.BlockSpec((None, tq, q.shape[2]), lambda b, i: (b, i, 0)), whole(k), whole(v)]
    args = [q, k, v]
    if has_ctx:
        in_specs += [whole(kc), whole(vc)]
        args += [kc, vc]
    return pl.pallas_call(
        functools.partial(_attn_kernel, has_ctx=has_ctx),
        grid=(B, L // tq), in_specs=in_specs,
        out_specs=pl.BlockSpec((None, tq, MLA_W), lambda b, i: (b, i, 0)),
        out_shape=jax.ShapeDtypeStruct((B, L, MLA_W), BF16),
        compiler_params=_cparams(("parallel", "parallel")),
        name="attention_lat" if has_ctx else "attention_ctx",
    )(*args)


def _hy_filter_kernel(z_ref, w1_ref, b1_ref, w2_ref, b2_ref, w3_ref, win_ref, x_ref):
    h1 = jnp.sin(_dot3(z_ref[...], w1_ref[...]) + b1_ref[...])
    h2 = jnp.sin(_dot3(h1, w2_ref[...]) + b2_ref[...])
    h = _dot3(h2, w3_ref[...])
    win = win_ref[...]
    hf = h[:, :HY_CH] * win
    hb = h[:, HY_CH:] * win
    row = lax.broadcasted_iota(jnp.int32, hb.shape, 0)
    hb = jnp.where(row == 0, 0.0, hb)
    nrm = jnp.sum(jnp.abs(hf), axis=0, keepdims=True) + jnp.sum(jnp.abs(hb), axis=0, keepdims=True) + EPS
    inv = 1.0 / nrm
    x_ref[:, :HY_CH] = (hf + hb) * inv
    x_ref[:, HY_CH:] = (hf - hb) * inv


def _hy_filter(L, p):
    z, win = _hyena_feat_np(L)
    full = lambda shape: pl.BlockSpec(shape, lambda o: (0,) * len(shape))
    return pl.pallas_call(
        _hy_filter_kernel,
        grid=(HY_ORDER,),
        in_specs=[full((L, HY_EMB_PAD)), full((HY_EMB_PAD, HY_FFN)), full((1, HY_FFN)),
                  full((HY_FFN, HY_FFN)), full((1, HY_FFN)),
                  pl.BlockSpec((HY_FFN, 2 * HY_CH), lambda o: (0, o)),
                  full((L, HY_CH))],
        out_specs=pl.BlockSpec((L, 2 * HY_CH), lambda o: (0, o)),
        out_shape=jax.ShapeDtypeStruct((L, HY_ORDER * 2 * HY_CH), F32),
        compiler_params=_cparams(("parallel",)),
        name="hy_filter",
    )(jnp.asarray(z), p["hy_w1"], p["hy_b1"], p["hy_w2"], p["hy_b2"], p["hy_w3"], jnp.asarray(win))


def _hy_gspec_kernel(f_ref, x_ref, g_ref, *, hb):
    xh, xl = _split(x_ref[...])
    f = f_ref[...]
    r = jnp.dot(f, xh, preferred_element_type=F32) + jnp.dot(f, xl, preferred_element_type=F32)
    ga = r[:hb, :HY_CH]
    gb = r[hb:, HY_CH:]
    nyq = r[hb:hb + 1, :HY_CH]
    k = pl.program_id(1) * hb + lax.broadcasted_iota(jnp.int32, (hb, HY_CH), 0)
    g_ref[0] = ga
    g_ref[1] = jnp.where(k == 0, 0.0, gb)
    g_ref[2] = jnp.where(k == 0, jnp.broadcast_to(nyq, ga.shape), ga)


def _hy_gspec(fwd, x, L, hb):
    return pl.pallas_call(
        functools.partial(_hy_gspec_kernel, hb=hb),
        grid=(HY_ORDER, L // hb),
        in_specs=[pl.BlockSpec((2 * hb, L), lambda o, t: (t, 0)),
                  pl.BlockSpec((L, 2 * HY_CH), lambda o, t: (0, o))],
        out_specs=pl.BlockSpec((3, hb, HY_CH), lambda o, t: (0, t, o)),
        out_shape=jax.ShapeDtypeStruct((3, L, HY_ORDER * HY_CH), F32),
        compiler_params=_cparams(("parallel", "parallel")),
        name="hy_gspec",
    )(fwd, x)


def _hy_conv_kernel(u_ref, w_ref, b_ref, v_ref, vb_ref, x1_ref, x2_ref):
    u = u_ref[...]
    L = u.shape[0]
    row = lax.broadcasted_iota(jnp.int32, u.shape, 0)
    prev = jnp.where(row == 0, 0.0, pltpu.roll(u, 1, axis=0))
    nxt = jnp.where(row == L - 1, 0.0, pltpu.roll(u, L - 1, axis=0))
    w = w_ref[...]
    uc = prev * w[0:1] + u * w[1:2] + nxt * w[2:3] + b_ref[...]
    v_ref[...] = uc[:, :HY_CH]
    vb_ref[...] = uc[:, :HY_CH].astype(vb_ref.dtype)
    x1_ref[...] = uc[:, HY_CH:2 * HY_CH]
    x2_ref[...] = uc[:, 2 * HY_CH:]


def _hy_conv(hu, B, w, b):
    L = hu.shape[0]
    col = pl.BlockSpec((L, HY_CH), lambda bb: (0, bb))
    sds = lambda dt: jax.ShapeDtypeStruct((L, B * HY_CH), dt)
    return pl.pallas_call(
        _hy_conv_kernel,
        grid=(B,),
        in_specs=[pl.BlockSpec((L, 3 * HY_CH), lambda bb: (0, bb)),
                  pl.BlockSpec((3, 3 * HY_CH), lambda bb: (0, 0)),
                  pl.BlockSpec((1, 3 * HY_CH), lambda bb: (0, 0))],
        out_specs=[col, col, col, col],
        out_shape=[sds(F32), sds(BF16), sds(F32), sds(F32)],
        compiler_params=_cparams(("parallel",)),
        name="hy_conv",
    )(hu, w, b)


def _hy_fwd_kernel(f_ref, u_ref, g_ref, y_ref, *, hb):
    acc = jnp.dot(f_ref[...], u_ref[...], preferred_element_type=F32)
    ga, gb, gd = g_ref[0], g_ref[1], g_ref[2]
    for c in range(u_ref.shape[1] // HY_CH):
        sl = slice(c * HY_CH, (c + 1) * HY_CH)
        ur = acc[:hb, sl]
        us = acc[hb:, sl]
        y_ref[:hb, sl] = (ur * ga - us * gb).astype(y_ref.dtype)
        y_ref[hb:, sl] = (ur * gb + us * gd).astype(y_ref.dtype)


def _hy_fwd(fwd, u, g, o, hb, tn):
    M, L = fwd.shape
    N = u.shape[1]
    tn = min(tn, N)
    return pl.pallas_call(
        functools.partial(_hy_fwd_kernel, hb=hb),
        grid=(N // tn, L // hb),
        in_specs=[pl.BlockSpec((2 * hb, L), lambda j, t: (t, 0)),
                  pl.BlockSpec((L, tn), lambda j, t: (0, j)),
                  pl.BlockSpec((3, hb, HY_CH), lambda j, t: (0, t, o))],
        out_specs=pl.BlockSpec((2 * hb, tn), lambda j, t: (t, j)),
        out_shape=jax.ShapeDtypeStruct((M, N), BF16),
        compiler_params=_cparams(("parallel", "parallel")),
        name="hy_fwd",
    )(fwd, u, g)


def _hy_inv_kernel(f_ref, y_ref, gate_ref, z_ref, d_ref, *out_refs):
    y = jnp.dot(f_ref[...], y_ref[...], preferred_element_type=F32)
    d = d_ref[...]
    for c in range(y.shape[1] // HY_CH):
        sl = slice(c * HY_CH, (c + 1) * HY_CH)
        z = gate_ref[:, sl] * (y[:, sl] + d * z_ref[:, sl])
        for r in out_refs:
            r[:, sl] = z.astype(r.dtype)


def _hy_inv(inv, y, gate, zprev, hy_bias, o, tm, tn, out_dtypes):
    L, K = inv.shape
    N = y.shape[1]
    tm, tn = min(tm, L), min(tn, N)
    tile = pl.BlockSpec((tm, tn), lambda j, i: (i, j))
    return pl.pallas_call(
        _hy_inv_kernel,
        grid=(N // tn, L // tm),
        in_specs=[pl.BlockSpec((tm, K), lambda j, i: (i, 0)),
                  pl.BlockSpec((K, tn), lambda j, i: (0, j)),
                  tile, tile,
                  pl.BlockSpec((1, HY_CH), lambda j, i: (0, 0))],
        out_specs=[tile] * len(out_dtypes),
        out_shape=[jax.ShapeDtypeStruct((L, N), dt) for dt in out_dtypes],
        compiler_params=_cparams(("parallel", "parallel")),
        name="hy_inv",
    )(inv, y, gate, zprev, hy_bias[o:o + 1])


def _hyena(hu, B, p):
    L = hu.shape[0]
    t = _tiles(L)
    hb, tn = t["hb"], t["tcol"]
    fw_np, iv_np = _hyena_dft_np(L, hb)
    fwd, inv = _bf16_const(fw_np), _bf16_const(iv_np)
    g = _hy_gspec(fwd, _hy_filter(L, p), L, hb)
    v, vb, x1, x2 = _hy_conv(hu, B, p["hy_short_w"], p["hy_short_b"])
    y0 = _hy_fwd(fwd, vb, g, 0, hb, tn)
    z1, z1b = _hy_inv(inv, y0, x1, v, p["hy_bias"], 0, hb, tn, (F32, BF16))
    y1 = _hy_fwd(fwd, z1b, g, 1, hb, tn)
    (z2,) = _hy_inv(inv, y1, x2, z1, p["hy_bias"], 1, hb, tn, (BF16,))
    return z2


def _out_proj_kernel(ret_ref, four_ref, att_ref, hy_ref, w_ref, x_ref, mod_ref, g_ref, x1_ref, h2_ref):
    d = lambda a, i: jnp.dot(a[...], w_ref[i * 256:(i + 1) * 256, :], preferred_element_type=F32)
    m = d(ret_ref, 0) + d(four_ref, 1) + d(att_ref, 2) + d(hy_ref, 3)
    mod = mod_ref[...]
    g1 = mod[:, 2 * D_MODEL:3 * D_MODEL]
    sh2 = mod[:, 3 * D_MODEL:4 * D_MODEL]
    sc2 = mod[:, 4 * D_MODEL:5 * D_MODEL]
    g = g_ref[...]
    x1 = x_ref[...] + g1 * _rms(m, g[1:2])
    x1_ref[...] = x1
    h2_ref[...] = (_rms(x1, g[2:3]) * (1.0 + sc2) + sh2).astype(h2_ref.dtype)


def _out_proj(ret, four, att, hy, x, mod, l, p, latent):
    B, L, _ = x.shape
    tm = _tiles(L)["tm"]
    row = (lambda b: b) if latent else (lambda b: CTX_ROW)
    bm = lambda w: pl.BlockSpec((None, tm, w), lambda b, i: (b, i, 0))
    tmj = pl.BlockSpec((tm, 256), lambda b, i: (i, b))
    return pl.pallas_call(
        _out_proj_kernel,
        grid=(B, L // tm),
        in_specs=[bm(RET_W), tmj, bm(MLA_W), tmj,
                  pl.BlockSpec((4 * 256, D_MODEL), lambda b, i: (0, 0)),
                  bm(D_MODEL),
                  pl.BlockSpec((None, None, 1, 6 * D_MODEL), lambda b, i: (l, row(b), 0, 0)),
                  pl.BlockSpec((None, 4, D_MODEL), lambda b, i: (l, 0, 0))],
        out_specs=[bm(D_MODEL), bm(D_MODEL)],
        out_shape=[jax.ShapeDtypeStruct((B, L, D_MODEL), F32),
                   jax.ShapeDtypeStruct((B, L, D_MODEL), BF16)],
        compiler_params=_cparams(("parallel", "parallel")),
        name="out_proj",
    )(ret, four, att, hy, p["w_out"], x, mod, p["norm_g"])


def _ffn_kernel(h_ref, wg_ref, wu_ref, wd_ref, x_ref, mod_ref, g_ref, o_ref, acc_ref):
    j = pl.program_id(1)
    h = h_ref[...]
    a = jnp.dot(h, wg_ref[...], preferred_element_type=F32)
    u = jnp.dot(h, wu_ref[...], preferred_element_type=F32)
    part = jnp.dot((_silu(a) * u).astype(BF16), wd_ref[...], preferred_element_type=F32)

    @pl.when(j == 0)
    def _():
        acc_ref[...] = part

    @pl.when(j > 0)
    def _():
        acc_ref[...] += part

    @pl.when(j == pl.num_programs(1) - 1)
    def _():
        g2 = mod_ref[...][:, 5 * D_MODEL:6 * D_MODEL]
        o_ref[...] = x_ref[...] + g2 * _rms(acc_ref[...], g_ref[...][3:4])


def _ffn(h2, x1, mod, l, p, latent):
    B, L, _ = x1.shape
    T = B * L
    tm = 512
    tf = D_FF // 2
    per_b = L // tm if latent else 0
    row = (lambda i: i // per_b) if latent else (lambda i: CTX_ROW)
    rows = pl.BlockSpec((tm, D_MODEL), lambda i, j: (i, 0))
    out = pl.pallas_call(
        _ffn_kernel,
        grid=(T // tm, D_FF // tf),
        in_specs=[rows,
                  pl.BlockSpec((D_MODEL, tf), lambda i, j: (0, j)),
                  pl.BlockSpec((D_MODEL, tf), lambda i, j: (0, j)),
                  pl.BlockSpec((tf, D_MODEL), lambda i, j: (j, 0)),
                  rows,
                  pl.BlockSpec((None, None, 1, 6 * D_MODEL), lambda i, j: (l, row(i), 0, 0)),
                  pl.BlockSpec((None, 4, D_MODEL), lambda i, j: (l, 0, 0))],
        out_specs=rows,
        out_shape=jax.ShapeDtypeStruct((T, D_MODEL), F32),
        scratch_shapes=[pltpu.VMEM((tm, D_MODEL), F32)],
        compiler_params=_cparams(("parallel", "arbitrary")),
        name="ffn",
    )(h2.reshape(T, D_MODEL), p["w_gate"], p["w_up"], p["w_down"], x1.reshape(T, D_MODEL), mod, p["norm_g"])
    return out.reshape(B, L, D_MODEL)


def _layer(x, mod, l, p, latent, ctx_kv, s0):
    B, L, _ = x.shape
    t = _tiles(L)
    qkvg, fx, q, k, v, ckv, krp, hu = _in_proj(x, mod, l, p, latent)
    ret, s_new = _retention(qkvg, p["dec_rows"], s0, l, emit_state=not latent)
    four = _mm(_bf16_const(_pos_dft_np(L)), fx.reshape(2 * L, B * FNET_W),
               (FNET_GC * L) ** -0.5, BF16, t["hb"], t["tcol"], "fourier_pos")
    kc, vc = ctx_kv if latent else (None, None)
    att = _attention(q, k, v, kc, vc)
    hy = _hyena(hu, B, p)
    x1, h2 = _out_proj(ret, four, att, hy, x, mod, l, p, latent)
    return _ffn(h2, x1, mod, l, p, latent), ckv, krp, s_new


def _prep_layer(l, w):
    zeros = lambda r, c: jnp.zeros((r, c), F32)
    w_in = w["w_in"][l]
    n0 = OFF_KR
    w_in_p = jnp.concatenate(
        [w_in[:, :n0], zeros(D_MODEL, MLA_ROPE_OFF), w_in[:, n0:n0 + MLA_ROPE],
         zeros(D_MODEL, MLA_HP - MLA_ROPE_OFF - MLA_ROPE), w_in[:, n0 + MLA_ROPE:]], axis=1)
    uq = w["mla_w_uq"][l].reshape(MLA_Q_LORA, MLA_HEADS, MLA_NOPE + MLA_ROPE)
    uq = jnp.pad(uq, ((0, 0), (0, 0), (0, MLA_HP - MLA_NOPE - MLA_ROPE))).reshape(MLA_Q_LORA, -1)
    ukv = w["mla_w_ukv"][l].reshape(MLA_KV_LORA, MLA_HEADS, MLA_NOPE + MLA_V)
    uk = jnp.pad(ukv[:, :, :MLA_NOPE], ((0, 0), (0, 0), (0, MLA_HP - MLA_NOPE))).reshape(MLA_KV_LORA, -1)
    uv = ukv[:, :, MLA_NOPE:].reshape(MLA_KV_LORA, -1)
    return dict(
        g0=w["norm_g"][l, 0:1], norm_g=w["norm_g"],
        w_in=w_in_p.astype(BF16), w_out=w["w_out"][l].astype(BF16),
        q_norm=w["mla_q_norm"][l][None], kv_norm=w["mla_kv_norm"][l][None],
        w_uq=uq.astype(BF16), w_ukv=jnp.concatenate([uk, uv], axis=1).astype(BF16),
        dec_rows=w["dec_rows"],
        hy_short_w=w["hy_short_w"][l], hy_short_b=w["hy_short_b"][l][None],
        hy_w1=jnp.pad(w["hy_w1"][l], ((0, HY_EMB_PAD - HY_EMB), (0, 0))), hy_b1=w["hy_b1"][l][None],
        hy_w2=w["hy_w2"][l], hy_b2=w["hy_b2"][l][None], hy_w3=w["hy_w3"][l],
        hy_bias=w["hy_bias"][l],
        w_gate=w["w_gate"][l].astype(BF16), w_up=w["w_up"][l].astype(BF16),
        w_down=w["w_down"][l].astype(BF16),
    )


def kernel(x_prompt, x_sample, cache_ckv, cache_krope, state_ret, c, c_ctx, w_ada, b_ada, norm_g, w_in, w_out,
           ret_decay, mla_q_norm, mla_kv_norm, mla_w_uq, mla_w_ukv, hy_short_w, hy_short_b, hy_w1, hy_b1,
           hy_w2, hy_b2, hy_w3, hy_bias, w_gate, w_up, w_down):
    nb = c.shape[0]
    cvec = jnp.concatenate([c, c_ctx[None], jnp.zeros((MOD_ROWS - nb - 1, D_MODEL), F32)], axis=0)
    mod = _ada(cvec, w_ada, b_ada).reshape(DEPTH, MOD_ROWS, 1, 6 * D_MODEL)
    dec_rows = jnp.broadcast_to(ret_decay.reshape(DEPTH, 2 * RET_HEADS, 1), (DEPTH, 2 * RET_HEADS, 128))
    krope_pad = jnp.pad(cache_krope, ((0, 0), (0, 0), (0, 0),
                                      (MLA_ROPE_OFF, MLA_HP - MLA_ROPE_OFF - MLA_ROPE)))
    w = dict(norm_g=norm_g, w_in=w_in, w_out=w_out, dec_rows=dec_rows, mla_q_norm=mla_q_norm,
             mla_kv_norm=mla_kv_norm, mla_w_uq=mla_w_uq, mla_w_ukv=mla_w_ukv, hy_short_w=hy_short_w,
             hy_short_b=hy_short_b, hy_w1=hy_w1, hy_b1=hy_b1, hy_w2=hy_w2, hy_b2=hy_b2, hy_w3=hy_w3,
             hy_bias=hy_bias, w_gate=w_gate, w_up=w_up, w_down=w_down)
    xp, xs = x_prompt, x_sample
    new_ckv, new_kr, new_s = [], [], []
    for l in range(DEPTH):
        p = _prep_layer(l, w)
        xp, ckv, krp, s_ret = _layer(xp, mod, l, p, False, None, None)
        new_ckv.append(ckv)
        new_kr.append(krp[:, :, MLA_ROPE_OFF:MLA_ROPE_OFF + MLA_ROPE])
        new_s.append(s_ret)
        ctx_kv = _ctx_kv(cache_ckv, krope_pad, l, p["w_ukv"])
        xs, _, _, _ = _layer(xs, mod, l, p, True, ctx_kv, state_ret)
    return (xp, xs, jnp.stack(new_ckv, axis=1), jnp.stack(new_kr, axis=1), jnp.stack(new_s, axis=1))
```

```python
import functools
import math

import numpy as np
import jax
import jax.numpy as jnp
from jax import lax
from jax.experimental import pallas as pl
from jax.experimental.pallas import tpu as pltpu

F32 = jnp.float32
BF16 = jnp.bfloat16

D_MODEL = 1024
DEPTH = 2
GRID_W = 64
EPS = 1e-6
ROPE_BASE = 10000.0
RET_HEADS = 4
RET_DK = 64
RET_DV = 64
RET_W = RET_HEADS * RET_DV
FNET_GROUPS = 4
FNET_GC = 64
FNET_W = FNET_GROUPS * FNET_GC
MLA_HEADS = 4
MLA_Q_LORA = 256
MLA_KV_LORA = 128
MLA_NOPE = 64
MLA_ROPE = 32
MLA_V = 64
MLA_W = MLA_HEADS * MLA_V
MLA_HP = 128
MLA_ROPE_OFF = MLA_NOPE
HY_CH = 256
HY_ORDER = 2
HY_BANDS = 16
HY_EMB = 1 + 2 * HY_BANDS
HY_EMB_PAD = 128
HY_FFN = 64
HY_FAST_DECAY = 0.3
HY_SLOW_DECAY = 1.5
HY_TARGET = 1e-2
D_FF = ((8 * D_MODEL + 3 * 256 - 1) // (3 * 256)) * 256
FFN_CHUNK = 256
MOD_ROWS = 16
CTX_ROW = 8

OFF_RET = 0
OFF_FU = 4 * RET_W
OFF_CQ = OFF_FU + FNET_W
OFF_CKV = OFF_CQ + MLA_Q_LORA
OFF_KR = OFF_CKV + MLA_KV_LORA
OFF_HU = OFF_KR + MLA_HP
IN_WP = OFF_HU + 3 * HY_CH

VMEM_LIMIT = 52 * 1024 * 1024


def _cparams(sem):
    return pltpu.CompilerParams(dimension_semantics=sem, vmem_limit_bytes=VMEM_LIMIT)


def _tiles(L):
    return dict(
        tm=min(L, 512),
        tm_ffn=min(L, 512),
        chunk=min(L, 256),
        tq=min(L, 256),
        hb=min(L, 512),
        tcol=1024,
    )


def _bf16_const(a):
    return jnp.asarray(np.asarray(a, np.float32)).astype(BF16)


@functools.lru_cache(maxsize=None)
def _rope_np(L, width, seg_off, seg_w, rope_dim):
    row = np.repeat(np.arange(L // GRID_W), GRID_W).astype(np.float64)
    col = np.tile(np.arange(GRID_W), L // GRID_W).astype(np.float64)
    quarter = rope_dim // 4
    inv = ROPE_BASE ** (-np.arange(quarter, dtype=np.float64) / quarter)
    cos = np.ones((L, width), np.float64)
    sin = np.zeros((L, width), np.float64)
    for lane in range(width):
        r = lane % seg_w - seg_off
        if r < 0 or r >= rope_dim:
            continue
        pos = row if r < rope_dim // 2 else col
        ang = pos * inv[r % quarter]
        first = (r % (rope_dim // 2)) < quarter
        cos[:, lane] = np.cos(ang)
        sin[:, lane] = -np.sin(ang) if first else np.sin(ang)
    return cos.astype(np.float32), sin.astype(np.float32)


@functools.lru_cache(maxsize=None)
def _chan_dft_np():
    k = np.arange(FNET_GC)
    ang = 2.0 * np.pi * np.outer(k, k) / FNET_GC
    c = np.kron(np.eye(FNET_GROUPS), np.cos(ang))
    s = np.kron(np.eye(FNET_GROUPS), np.sin(ang))
    return np.concatenate([c, s], axis=1)


@functools.lru_cache(maxsize=None)
def _pos_dft_np(L):
    k = np.arange(L, dtype=np.int64)
    ang = 2.0 * np.pi * (np.outer(k, k) % L) / L
    return np.concatenate([np.cos(ang), -np.sin(ang)], axis=1)


@functools.lru_cache(maxsize=None)
def _hyena_dft_np(L, hb):
    k = np.arange(L, dtype=np.int64)
    ang = np.pi * (np.outer(k, k) % (2 * L)) / L
    cf = np.cos(ang)
    sf = np.sin(ang)
    alt = (-1.0) ** np.arange(L)
    sf[0, :] = alt
    w = np.where(k == 0, 1.0, 2.0)[None, :] / (2.0 * L)
    ainv = cf.T * w
    binv = sf.T / L
    binv[:, 0] = alt / (2.0 * L)
    fw, iv = [], []
    for t in range(L // hb):
        sl = slice(t * hb, (t + 1) * hb)
        fw += [cf[sl], sf[sl]]
        iv += [ainv[:, sl], binv[:, sl]]
    return np.concatenate(fw, axis=0), np.concatenate(iv, axis=1)


@functools.lru_cache(maxsize=None)
def _hyena_feat_np(L):
    pos = np.arange(L, dtype=np.float64)
    t = pos / L
    bands = np.arange(1, HY_BANDS + 1, dtype=np.float64)
    ang = (2.0 * math.pi / L) * pos[:, None] * bands[None, :]
    z = np.zeros((L, HY_EMB_PAD), np.float64)
    z[:, 0] = t
    z[:, 1:1 + HY_BANDS] = np.sin(ang)
    z[:, 1 + HY_BANDS:HY_EMB] = np.cos(ang)
    deltas = np.abs(np.linspace(math.log(HY_TARGET) / HY_SLOW_DECAY,
                                math.log(HY_TARGET) / HY_FAST_DECAY, HY_CH))
    window = np.exp(-t[:, None] * deltas[None, :])
    return z.astype(np.float32), window.astype(np.float32)


def _rms(x, g):
    return x * lax.rsqrt(jnp.mean(x * x, axis=-1, keepdims=True) + EPS) * g


def _silu(x):
    return x * jax.nn.sigmoid(x)


def _bdot(a, b):
    return jnp.dot(a.astype(BF16), b.astype(BF16), preferred_element_type=F32)


def _split(a):
    hi = a.astype(BF16)
    lo = (a - hi.astype(F32)).astype(BF16)
    return hi, lo


def _dot3(a, b):
    ah, al = _split(a)
    bh, bl = _split(b)
    d = lambda x, y: jnp.dot(x, y, preferred_element_type=F32)
    return d(ah, bh) + (d(al, bh) + d(ah, bl))


def _rope(x, cos, sin_signed, quarter):
    w = x.shape[-1]
    lane = lax.broadcasted_iota(jnp.int32, x.shape, 1)
    first = (lane % (2 * quarter)) < quarter
    up = pltpu.roll(x, w - quarter, axis=1)
    dn = pltpu.roll(x, quarter, axis=1)
    return x * cos + jnp.where(first, up, dn) * sin_signed


def _ada_kernel(c_ref, w_ref, b_ref, o_ref):
    s = _silu(c_ref[...])
    o_ref[...] = _bdot(s, w_ref[...]) + b_ref[...]


def _ada(cvec, w_ada, b_ada):
    tn = 1024
    n6 = w_ada.shape[-1]
    return pl.pallas_call(
        _ada_kernel,
        grid=(DEPTH, n6 // tn),
        in_specs=[
            pl.BlockSpec((MOD_ROWS, D_MODEL), lambda l, j: (0, 0)),
            pl.BlockSpec((None, D_MODEL, tn), lambda l, j: (l, 0, j)),
            pl.BlockSpec((None, 1, tn), lambda l, j: (l, 0, j)),
        ],
        out_specs=pl.BlockSpec((None, MOD_ROWS, tn), lambda l, j: (l, 0, j)),
        out_shape=jax.ShapeDtypeStruct((DEPTH, MOD_ROWS, n6), F32),
        compiler_params=_cparams(("parallel", "parallel")),
        name="ada",
    )(cvec, w_ada, b_ada.reshape(DEPTH, 1, n6))


def _in_proj_kernel(*refs, latent):
    (x_ref, mod_ref, g0_ref, w_ref, qn_ref, kvn_ref, wuq_ref, wukv_ref, bd_ref) = refs[:9]
    pos = 9
    if latent:
        cr_ref, sr_ref, cm_ref, sm_ref = refs[pos:pos + 4]
        pos += 4
        qkvg_ref, fx_ref, q_ref, k_ref, v_ref, hu_ref = refs[pos:pos + 6]
        lanes = lambda ref, reps: jnp.concatenate([ref[...]] * reps, axis=1)
    else:
        qkvg_ref, fx_ref, q_ref, k_ref, v_ref, hu_ref, ckv_ref, kr_ref = refs[pos:pos + 8]

    x = x_ref[...]
    mod = mod_ref[...]
    sh1 = mod[:, 0:D_MODEL]
    sc1 = mod[:, D_MODEL:2 * D_MODEL]
    h = _rms(x, g0_ref[...]) * (1.0 + sc1) + sh1
    proj = jnp.dot(h.astype(BF16), w_ref[...], preferred_element_type=F32)

    rq = proj[:, OFF_RET:OFF_RET + RET_W]
    rk = proj[:, OFF_RET + RET_W:OFF_RET + 2 * RET_W] * (RET_DK ** -0.5)
    if latent:
        cos_r, sin_r = lanes(cr_ref, RET_W // 128), lanes(sr_ref, RET_W // 128)
        rq = _rope(rq, cos_r, sin_r, RET_DK // 4)
        rk = _rope(rk, cos_r, sin_r, RET_DK // 4)
    qkvg_ref[:, 0:RET_W] = rq
    qkvg_ref[:, RET_W:2 * RET_W] = rk
    qkvg_ref[:, 2 * RET_W:4 * RET_W] = proj[:, OFF_RET + 2 * RET_W:OFF_RET + 4 * RET_W]

    fcs = _bdot(proj[:, OFF_FU:OFF_FU + FNET_W], bd_ref[...])
    fx_ref[0] = fcs[:, :FNET_W].astype(fx_ref.dtype)
    fx_ref[1] = fcs[:, FNET_W:].astype(fx_ref.dtype)

    cqn = _rms(proj[:, OFF_CQ:OFF_CQ + MLA_Q_LORA], qn_ref[...])
    q = _bdot(cqn, wuq_ref[...])
    ckvn = _rms(proj[:, OFF_CKV:OFF_CKV + MLA_KV_LORA], kvn_ref[...])
    kv = _bdot(ckvn, wukv_ref[...])
    krp = proj[:, OFF_KR:OFF_KR + MLA_HP]
    if latent:
        q = _rope(q, lanes(cm_ref, MLA_HEADS), lanes(sm_ref, MLA_HEADS), MLA_ROPE // 4)
        krp = _rope(krp, cm_ref[...], sm_ref[...], MLA_ROPE // 4)
    else:
        ckv_ref[...] = ckvn
        kr_ref[...] = krp
    scale = (MLA_NOPE + MLA_ROPE) ** -0.5
    q_ref[...] = (q * scale).astype(q_ref.dtype)
    k_ref[...] = (kv[:, :MLA_HEADS * MLA_HP] + jnp.concatenate([krp] * MLA_HEADS, axis=1)).astype(k_ref.dtype)
    v_ref[...] = kv[:, MLA_HEADS * MLA_HP:].astype(v_ref.dtype)

    hu_ref[...] = proj[:, OFF_HU:OFF_HU + 3 * HY_CH]


def _in_proj(x, mod, l, p, latent):
    B, L, _ = x.shape
    t = _tiles(L)
    tm = t["tm"]
    nt = L // tm
    row = (lambda b: b) if latent else (lambda b: CTX_ROW)
    full = lambda shape: pl.BlockSpec(shape, lambda i, b: (0,) * len(shape), pipeline_mode=pl.Buffered(1))
    in_specs = [
        pl.BlockSpec((None, tm, D_MODEL), lambda i, b: (b, i, 0)),
        pl.BlockSpec((None, None, 1, 6 * D_MODEL), lambda i, b: (l, row(b), 0, 0)),
        full((1, D_MODEL)),
        full((D_MODEL, IN_WP)),
        full((1, MLA_Q_LORA)),
        full((1, MLA_KV_LORA)),
        full((MLA_Q_LORA, MLA_HEADS * MLA_HP)),
        full((MLA_KV_LORA, MLA_HEADS * MLA_HP + MLA_W)),
        full((FNET_W, 2 * FNET_W)),
    ]
    args = [x, mod, p["g0"], p["w_in"], p["q_norm"], p["kv_norm"], p["w_uq"], p["w_ukv"],
            _bf16_const(_chan_dft_np())]
    if latent:
        tabs = (_rope_np(L, 128, 0, RET_DK, RET_DK)
                + _rope_np(L, MLA_HP, MLA_ROPE_OFF, MLA_HP, MLA_ROPE))
        for tab in tabs:
            in_specs.append(pl.BlockSpec((tm, tab.shape[1]), lambda i, b: (i, 0)))
            args.append(jnp.asarray(tab))
    bm = lambda w: pl.BlockSpec((None, tm, w), lambda i, b: (b, i, 0))
    sds = jax.ShapeDtypeStruct
    out_specs = [
        bm(4 * RET_W),
        pl.BlockSpec((2, tm, FNET_W), lambda i, b: (0, i, b)),
        bm(MLA_HEADS * MLA_HP), bm(MLA_HEADS * MLA_HP), bm(MLA_W),
        pl.BlockSpec((tm, 3 * HY_CH), lambda i, b: (i, b)),
    ]
    out_shape = [
        sds((B, L, 4 * RET_W), F32),
        sds((2, L, B * FNET_W), BF16),
        sds((B, L, MLA_HEADS * MLA_HP), BF16),
        sds((B, L, MLA_HEADS * MLA_HP), BF16),
        sds((B, L, MLA_W), BF16),
        sds((L, B * 3 * HY_CH), F32),
    ]
    if not latent:
        out_specs += [bm(MLA_KV_LORA), bm(MLA_HP)]
        out_shape += [sds((B, L, MLA_KV_LORA), F32), sds((B, L, MLA_HP), F32)]
    outs = pl.pallas_call(
        functools.partial(_in_proj_kernel, latent=latent),
        grid=(nt, B), in_specs=in_specs, out_specs=out_specs, out_shape=out_shape,
        compiler_params=_cparams(("parallel", "parallel")),
        name="in_proj_lat" if latent else "in_proj_ctx",
    )(*args)
    return tuple(outs) + ((None, None) if latent else ())


def _ctx_kv_kernel(ckv_ref, kr_ref, w_ref, k_ref, v_ref):
    kv = _bdot(ckv_ref[...], w_ref[...])
    k_ref[...] = (kv[:, :MLA_HEADS * MLA_HP]
                  + jnp.concatenate([kr_ref[...]] * MLA_HEADS, axis=1)).astype(k_ref.dtype)
    v_ref[...] = kv[:, MLA_HEADS * MLA_HP:].astype(v_ref.dtype)


def _ctx_kv(cache_ckv, krope_pad, l, w_ukv):
    B, _, P, _ = cache_ckv.shape
    return pl.pallas_call(
        _ctx_kv_kernel,
        grid=(B,),
        in_specs=[
            pl.BlockSpec((None, None, P, MLA_KV_LORA), lambda b: (b, l, 0, 0)),
            pl.BlockSpec((None, None, P, MLA_HP), lambda b: (b, l, 0, 0)),
            pl.BlockSpec(w_ukv.shape, lambda b: (0, 0)),
        ],
        out_specs=[
            pl.BlockSpec((None, P, MLA_HEADS * MLA_HP), lambda b: (b, 0, 0)),
            pl.BlockSpec((None, P, MLA_W), lambda b: (b, 0, 0)),
        ],
        out_shape=[
            jax.ShapeDtypeStruct((B, P, MLA_HEADS * MLA_HP), BF16),
            jax.ShapeDtypeStruct((B, P, MLA_W), BF16),
        ],
        compiler_params=_cparams(("parallel",)),
        name="ctx_kv",
    )(cache_ckv, krope_pad, w_ukv)


def _ret_kernel(*refs, L, C, has_s0, emit_state):
    q_ref, k_ref, v_ref, g_ref, dec_ref = refs[:5]
    pos = 5
    if has_s0:
        s0_ref = refs[pos]
        pos += 1
    o_ref = refs[pos]
    pos += 1
    if emit_state:
        sn_ref = refs[pos]
        pos += 1
    sf_ref, sb_ref, dm_ref, tab_ref = refs[pos:pos + 4]
    n = L // C
    H, W = RET_HEADS, RET_W

    d = dec_ref[0:2, :]
    lg = jnp.minimum(d, 0.0) - jnp.log1p(jnp.exp(-jnp.abs(d)))
    lf, lb = lg[0:1, :], lg[1:2, :]
    ii = lax.broadcasted_iota(jnp.int32, (C, C), 0)
    jj = lax.broadcasted_iota(jnp.int32, (C, C), 1)
    diff = (ii - jj).astype(F32)
    for h in range(H):
        lfh = lf[:, h * RET_DV:h * RET_DV + 1]
        lbh = lb[:, h * RET_DV:h * RET_DV + 1]
        dm_ref[h] = jnp.where(diff > 0, jnp.exp(diff * lfh),
                              jnp.where(diff < 0, jnp.exp(-diff * lbh), 2.0))
    ci = lax.broadcasted_iota(jnp.int32, (C, W), 0).astype(F32)
    tab_ref[0] = jnp.exp((ci + 1.0) * lf)
    tab_ref[1] = jnp.exp((C - ci) * lb)
    tab_ref[2] = jnp.exp((C - 1.0 - ci) * lf)
    tab_ref[3] = jnp.exp(ci * lb)
    g_f = jnp.exp(C * lf)
    g_b = jnp.exp(C * lb)
    hr = lax.broadcasted_iota(jnp.int32, (W, W), 0) // RET_DK
    hc = lax.broadcasted_iota(jnp.int32, (W, W), 1) // RET_DV
    same_head = hr == hc
    avg = jnp.where(same_head, 1.0 / RET_DV, 0.0).astype(BF16)

    sf_ref[0] = jnp.zeros((W, W), F32)
    sb_ref[n] = jnp.zeros((W, W), F32)
    if has_s0:
        for h in range(H):
            blk = slice(h * RET_DK, (h + 1) * RET_DK)
            sf_ref[0, blk, blk] = s0_ref[0, h]
            sb_ref[n, blk, blk] = s0_ref[1, h]

    tdot = lambda a, b: lax.dot_general(a.astype(BF16), b.astype(BF16), (((0,), (0,)), ((), ())),
                                        preferred_element_type=F32)

    def states(j, carry):
        jb = n - 1 - j
        rf = pl.multiple_of(j * C, C)
        rb = pl.multiple_of(jb * C, C)
        upd = tdot(k_ref[pl.ds(rf, C), :] * tab_ref[2], v_ref[pl.ds(rf, C), :])
        sf_ref[j + 1] = g_f * sf_ref[j] + jnp.where(same_head, upd, 0.0)
        upd = tdot(k_ref[pl.ds(rb, C), :] * tab_ref[3], v_ref[pl.ds(rb, C), :])
        sb_ref[jb] = g_b * sb_ref[jb + 1] + jnp.where(same_head, upd, 0.0)
        return carry

    lax.fori_loop(0, n, states, 0)

    lane_head = lax.broadcasted_iota(jnp.int32, (C, W), 1) // RET_DV

    def group_mean(x):
        hi, lo = _split(x)
        return (jnp.dot(hi, avg, preferred_element_type=F32)
                + jnp.dot(lo, avg, preferred_element_type=F32))

    def chunk(j, carry):
        r0 = pl.multiple_of(j * C, C)
        q = q_ref[pl.ds(r0, C), :]
        qb = q.astype(BF16)
        kb = k_ref[pl.ds(r0, C), :].astype(BF16)
        vb = v_ref[pl.ds(r0, C), :].astype(BF16)
        o = tab_ref[0] * jnp.dot(qb, sf_ref[j].astype(BF16), preferred_element_type=F32)
        o = o + tab_ref[1] * jnp.dot(qb, sb_ref[j + 1].astype(BF16), preferred_element_type=F32)
        inner = jnp.zeros((C, W), F32)
        for h in range(H):
            in_head = lane_head == h
            qh = jnp.where(in_head, q, 0.0).astype(BF16)
            s = lax.dot_general(qh, kb, (((1,), (1,)), ((), ())), preferred_element_type=F32)
            oh = jnp.dot((s * dm_ref[h]).astype(BF16), vb, preferred_element_type=F32)
            inner = jnp.where(in_head, oh, inner)
        o = o + inner
        dlt = o - group_mean(o)
        var = group_mean(dlt * dlt)
        gate = g_ref[pl.ds(r0, C), :]
        o_ref[pl.ds(r0, C), :] = (_silu(gate) * (dlt * lax.rsqrt(var + EPS))).astype(o_ref.dtype)
        return carry

    lax.fori_loop(0, n, chunk, 0)

    if emit_state:
        for h in range(H):
            blk = slice(h * RET_DK, (h + 1) * RET_DK)
            sn_ref[0, h] = sf_ref[n, blk, blk]
            sn_ref[1, h] = sb_ref[0, blk, blk]


def _retention(qkvg, dec_rows, s0, l, emit_state):
    B, L, _ = qkvg.shape
    C = _tiles(L)["chunk"]
    n = L // C
    has_s0 = s0 is not None
    def colspec(j):
        return pl.BlockSpec((None, L, RET_W), lambda b: (b, 0, j))

    in_specs = [colspec(j) for j in range(4)]
    in_specs.append(pl.BlockSpec((None, 8, RET_W), lambda b: (l, 0, 0)))
    args = [qkvg, qkvg, qkvg, qkvg, dec_rows]
    st_block = (None, None, 2, RET_HEADS, RET_DK, RET_DV)
    if has_s0:
        in_specs.append(pl.BlockSpec(st_block, lambda b: (b, l, 0, 0, 0, 0)))
        args.append(s0)
    out_specs = [pl.BlockSpec((None, L, RET_W), lambda b: (b, 0, 0))]
    out_shape = [jax.ShapeDtypeStruct((B, L, RET_W), BF16)]
    if emit_state:
        out_specs.append(pl.BlockSpec((None, 2, RET_HEADS, RET_DK, RET_DV), lambda b: (b, 0, 0, 0, 0)))
        out_shape.append(jax.ShapeDtypeStruct((B, 2, RET_HEADS, RET_DK, RET_DV), F32))
    outs = pl.pallas_call(
        functools.partial(_ret_kernel, L=L, C=C, has_s0=has_s0, emit_state=emit_state),
        grid=(B,), in_specs=in_specs, out_specs=out_specs, out_shape=out_shape,
        scratch_shapes=[
            pltpu.VMEM((n + 1, RET_W, RET_W), F32),
            pltpu.VMEM((n + 1, RET_W, RET_W), F32),
            pltpu.VMEM((RET_HEADS, C, C), F32),
            pltpu.VMEM((4, C, RET_W), F32),
        ],
        compiler_params=_cparams(("parallel",)),
        name="retention_lat" if has_s0 else "retention_ctx",
    )(*args)
    return outs if emit_state else (outs[0], None)


def _mm_kernel(a_ref, b_ref, o_ref, *, scale):
    o_ref[...] = (jnp.dot(a_ref[...], b_ref[...], preferred_element_type=F32) * scale).astype(o_ref.dtype)


def _mm(a, b, scale, out_dtype, tm, tn, name):
    M, K = a.shape
    _, N = b.shape
    tm, tn = min(tm, M), min(tn, N)
    return pl.pallas_call(
        functools.partial(_mm_kernel, scale=scale),
        grid=(N // tn, M // tm),
        in_specs=[pl.BlockSpec((tm, K), lambda j, i: (i, 0)),
                  pl.BlockSpec((K, tn), lambda j, i: (0, j))],
        out_specs=pl.BlockSpec((tm, tn), lambda j, i: (i, j)),
        out_shape=jax.ShapeDtypeStruct((M, N), out_dtype),
        compiler_params=_cparams(("parallel", "parallel")),
        name=name,
    )(a, b)


def _attn_kernel(*refs, has_ctx):
    if has_ctx:
        q_ref, k_ref, v_ref, kc_ref, vc_ref, o_ref = refs
    else:
        q_ref, k_ref, v_ref, o_ref = refs
    nt = lambda a, b: lax.dot_general(a, b, (((1,), (1,)), ((), ())), preferred_element_type=F32)
    for h in range(MLA_HEADS):
        hq = slice(h * MLA_HP, (h + 1) * MLA_HP)
        hv = slice(h * MLA_V, (h + 1) * MLA_V)
        qh = q_ref[:, hq]
        s = nt(qh, k_ref[:, hq])
        m = jnp.max(s, axis=-1, keepdims=True)
        if has_ctx:
            sc = nt(qh, kc_ref[:, hq])
            m = jnp.maximum(m, jnp.max(sc, axis=-1, keepdims=True))
        p = jnp.exp(s - m)
        den = jnp.sum(p, axis=-1, keepdims=True)
        o = jnp.dot(p.astype(BF16), v_ref[...], preferred_element_type=F32)
        if has_ctx:
            pc = jnp.exp(sc - m)
            den = den + jnp.sum(pc, axis=-1, keepdims=True)
            o = o + jnp.dot(pc.astype(BF16), vc_ref[...], preferred_element_type=F32)
        o_ref[:, hv] = (o[:, hv] / den).astype(o_ref.dtype)


def _attention(q, k, v, kc, vc):
    B, L, _ = q.shape
    tq = _tiles(L)["tq"]
    has_ctx = kc is not None
    whole = lambda a: pl.BlockSpec((None,) + a.shape[1:], lambda b, i: (b, 0, 0))
    in_specs = [pl.BlockSpec((None, tq, q.shape[2]), lambda b, i: (b, i, 0)), whole(k), whole(v)]
    args = [q, k, v]
    if has_ctx:
        in_specs += [whole(kc), whole(vc)]
        args += [kc, vc]
    return pl.pallas_call(
        functools.partial(_attn_kernel, has_ctx=has_ctx),
        grid=(B, L // tq), in_specs=in_specs,
        out_specs=pl.BlockSpec((None, tq, MLA_W), lambda b, i: (b, i, 0)),
        out_shape=jax.ShapeDtypeStruct((B, L, MLA_W), BF16),
        compiler_params=_cparams(("parallel", "parallel")),
        name="attention_lat" if has_ctx else "attention_ctx",
    )(*args)


def _hy_filter_kernel(z_ref, w1_ref, b1_ref, w2_ref, b2_ref, w3_ref, win_ref, x_ref):
    h1 = jnp.sin(_dot3(z_ref[...], w1_ref[...]) + b1_ref[...])
    h2 = jnp.sin(_dot3(h1, w2_ref[...]) + b2_ref[...])
    h = _dot3(h2, w3_ref[...])
    win = win_ref[...]
    hf = h[:, :HY_CH] * win
    hb = h[:, HY_CH:] * win
    row = lax.broadcasted_iota(jnp.int32, hb.shape, 0)
    hb = jnp.where(row == 0, 0.0, hb)
    nrm = jnp.sum(jnp.abs(hf), axis=0, keepdims=True) + jnp.sum(jnp.abs(hb), axis=0, keepdims=True) + EPS
    inv = 1.0 / nrm
    x_ref[:, :HY_CH] = (hf + hb) * inv
    x_ref[:, HY_CH:] = (hf - hb) * inv


def _hy_filter(L, p):
    z, win = _hyena_feat_np(L)
    full = lambda shape: pl.BlockSpec(shape, lambda o: (0,) * len(shape))
    return pl.pallas_call(
        _hy_filter_kernel,
        grid=(HY_ORDER,),
        in_specs=[full((L, HY_EMB_PAD)), full((HY_EMB_PAD, HY_FFN)), full((1, HY_FFN)),
                  full((HY_FFN, HY_FFN)), full((1, HY_FFN)),
                  pl.BlockSpec((HY_FFN, 2 * HY_CH), lambda o: (0, o)),
                  full((L, HY_CH))],
        out_specs=pl.BlockSpec((L, 2 * HY_CH), lambda o: (0, o)),
        out_shape=jax.ShapeDtypeStruct((L, HY_ORDER * 2 * HY_CH), F32),
        compiler_params=_cparams(("parallel",)),
        name="hy_filter",
    )(jnp.asarray(z), p["hy_w1"], p["hy_b1"], p["hy_w2"], p["hy_b2"], p["hy_w3"], jnp.asarray(win))


def _hy_gspec_kernel(f_ref, x_ref, g_ref, *, hb):
    xh, xl = _split(x_ref[...])
    f = f_ref[...]
    r = jnp.dot(f, xh, preferred_element_type=F32) + jnp.dot(f, xl, preferred_element_type=F32)
    ga = r[:hb, :HY_CH]
    gb = r[hb:, HY_CH:]
    nyq = r[hb:hb + 1, :HY_CH]
    k = pl.program_id(1) * hb + lax.broadcasted_iota(jnp.int32, (hb, HY_CH), 0)
    g_ref[0] = ga
    g_ref[1] = jnp.where(k == 0, 0.0, gb)
    g_ref[2] = jnp.where(k == 0, jnp.broadcast_to(nyq, ga.shape), ga)


def _hy_gspec(fwd, x, L, hb):
    return pl.pallas_call(
        functools.partial(_hy_gspec_kernel, hb=hb),
        grid=(HY_ORDER, L // hb),
        in_specs=[pl.BlockSpec((2 * hb, L), lambda o, t: (t, 0)),
                  pl.BlockSpec((L, 2 * HY_CH), lambda o, t: (0, o))],
        out_specs=pl.BlockSpec((3, hb, HY_CH), lambda o, t: (0, t, o)),
        out_shape=jax.ShapeDtypeStruct((3, L, HY_ORDER * HY_CH), F32),
        compiler_params=_cparams(("parallel", "parallel")),
        name="hy_gspec",
    )(fwd, x)


def _hy_conv_kernel(u_ref, w_ref, b_ref, v_ref, vb_ref, x1_ref, x2_ref):
    u = u_ref[...]
    L = u.shape[0]
    row = lax.broadcasted_iota(jnp.int32, u.shape, 0)
    prev = jnp.where(row == 0, 0.0, pltpu.roll(u, 1, axis=0))
    nxt = jnp.where(row == L - 1, 0.0, pltpu.roll(u, L - 1, axis=0))
    w = w_ref[...]
    uc = prev * w[0:1] + u * w[1:2] + nxt * w[2:3] + b_ref[...]
    v_ref[...] = uc[:, :HY_CH]
    vb_ref[...] = uc[:, :HY_CH].astype(vb_ref.dtype)
    x1_ref[...] = uc[:, HY_CH:2 * HY_CH]
    x2_ref[...] = uc[:, 2 * HY_CH:]


def _hy_conv(hu, B, w, b):
    L = hu.shape[0]
    col = pl.BlockSpec((L, HY_CH), lambda bb: (0, bb))
    sds = lambda dt: jax.ShapeDtypeStruct((L, B * HY_CH), dt)
    return pl.pallas_call(
        _hy_conv_kernel,
        grid=(B,),
        in_specs=[pl.BlockSpec((L, 3 * HY_CH), lambda bb: (0, bb)),
                  pl.BlockSpec((3, 3 * HY_CH), lambda bb: (0, 0)),
                  pl.BlockSpec((1, 3 * HY_CH), lambda bb: (0, 0))],
        out_specs=[col, col, col, col],
        out_shape=[sds(F32), sds(BF16), sds(F32), sds(F32)],
        compiler_params=_cparams(("parallel",)),
        name="hy_conv",
    )(hu, w, b)


def _hy_fwd_kernel(f_ref, u_ref, g_ref, y_ref, *, hb):
    acc = jnp.dot(f_ref[...], u_ref[...], preferred_element_type=F32)
    ga, gb, gd = g_ref[0], g_ref[1], g_ref[2]
    for c in range(u_ref.shape[1] // HY_CH):
        sl = slice(c * HY_CH, (c + 1) * HY_CH)
        ur = acc[:hb, sl]
        us = acc[hb:, sl]
        y_ref[:hb, sl] = (ur * ga - us * gb).astype(y_ref.dtype)
        y_ref[hb:, sl] = (ur * gb + us * gd).astype(y_ref.dtype)


def _hy_fwd(fwd, u, g, o, hb, tn):
    M, L = fwd.shape
    N = u.shape[1]
    tn = min(tn, N)
    return pl.pallas_call(
        functools.partial(_hy_fwd_kernel, hb=hb),
        grid=(N // tn, L // hb),
        in_specs=[pl.BlockSpec((2 * hb, L), lambda j, t: (t, 0)),
                  pl.BlockSpec((L, tn), lambda j, t: (0, j)),
                  pl.BlockSpec((3, hb, HY_CH), lambda j, t: (0, t, o))],
        out_specs=pl.BlockSpec((2 * hb, tn), lambda j, t: (t, j)),
        out_shape=jax.ShapeDtypeStruct((M, N), BF16),
        compiler_params=_cparams(("parallel", "parallel")),
        name="hy_fwd",
    )(fwd, u, g)


def _hy_inv_kernel(f_ref, y_ref, gate_ref, z_ref, d_ref, *out_refs):
    y = jnp.dot(f_ref[...], y_ref[...], preferred_element_type=F32)
    d = d_ref[...]
    for c in range(y.shape[1] // HY_CH):
        sl = slice(c * HY_CH, (c + 1) * HY_CH)
        z = gate_ref[:, sl] * (y[:, sl] + d * z_ref[:, sl])
        for r in out_refs:
            r[:, sl] = z.astype(r.dtype)


def _hy_inv(inv, y, gate, zprev, hy_bias, o, tm, tn, out_dtypes):
    L, K = inv.shape
    N = y.shape[1]
    tm, tn = min(tm, L), min(tn, N)
    tile = pl.BlockSpec((tm, tn), lambda j, i: (i, j))
    return pl.pallas_call(
        _hy_inv_kernel,
        grid=(N // tn, L // tm),
        in_specs=[pl.BlockSpec((tm, K), lambda j, i: (i, 0)),
                  pl.BlockSpec((K, tn), lambda j, i: (0, j)),
                  tile, tile,
                  pl.BlockSpec((1, HY_CH), lambda j, i: (0, 0))],
        out_specs=[tile] * len(out_dtypes),
        out_shape=[jax.ShapeDtypeStruct((L, N), dt) for dt in out_dtypes],
        compiler_params=_cparams(("parallel", "parallel")),
        name="hy_inv",
    )(inv, y, gate, zprev, hy_bias[o:o + 1])


def _hyena(hu, B, p):
    L = hu.shape[0]
    t = _tiles(L)
    hb, tn = t["hb"], t["tcol"]
    fw_np, iv_np = _hyena_dft_np(L, hb)
    fwd, inv = _bf16_const(fw_np), _bf16_const(iv_np)
    g = _hy_gspec(fwd, _hy_filter(L, p), L, hb)
    v, vb, x1, x2 = _hy_conv(hu, B, p["hy_short_w"], p["hy_short_b"])
    y0 = _hy_fwd(fwd, vb, g, 0, hb, tn)
    z1, z1b = _hy_inv(inv, y0, x1, v, p["hy_bias"], 0, hb, tn, (F32, BF16))
    y1 = _hy_fwd(fwd, z1b, g, 1, hb, tn)
    (z2,) = _hy_inv(inv, y1, x2, z1, p["hy_bias"], 1, hb, tn, (BF16,))
    return z2


def _mix_ffn_kernel(ret_ref, four_ref, att_ref, hy_ref, wo_ref, x_ref, mod_ref, g_ref,
                    wg_ref, wu_ref, wd_ref, o_ref, hm_ref):
    d = lambda a, i: jnp.dot(a[...], wo_ref[i * 256:(i + 1) * 256, :], preferred_element_type=F32)
    m = d(ret_ref, 0) + d(four_ref, 1) + d(att_ref, 2) + d(hy_ref, 3)
    mod = mod_ref[...]
    g1 = mod[:, 2 * D_MODEL:3 * D_MODEL]
    sh2 = mod[:, 3 * D_MODEL:4 * D_MODEL]
    sc2 = mod[:, 4 * D_MODEL:5 * D_MODEL]
    g2 = mod[:, 5 * D_MODEL:6 * D_MODEL]
    g = g_ref[...]
    x1 = x_ref[...] + g1 * _rms(m, g[1:2])
    o_ref[...] = x1
    h2 = (_rms(x1, g[2:3]) * (1.0 + sc2) + sh2).astype(BF16)
    for c in range(D_FF // FFN_CHUNK):
        cs = slice(c * FFN_CHUNK, (c + 1) * FFN_CHUNK)
        a = jnp.dot(h2, wg_ref[:, cs], preferred_element_type=F32)
        u = jnp.dot(h2, wu_ref[:, cs], preferred_element_type=F32)
        hm_ref[:, cs] = (_silu(a) * u).astype(BF16)
    f = jnp.dot(hm_ref[...], wd_ref[...], preferred_element_type=F32)
    o_ref[...] = o_ref[...] + g2 * _rms(f, g[3:4])


def _mix_ffn(ret, four, att, hy, x, mod, l, p, latent):
    B, L, _ = x.shape
    tm = _tiles(L)["tm_ffn"]
    row = (lambda b: b) if latent else (lambda b: CTX_ROW)
    bm = lambda w: pl.BlockSpec((None, tm, w), lambda b, i: (b, i, 0))
    tmj = pl.BlockSpec((tm, 256), lambda b, i: (i, b))
    resident = lambda shape: pl.BlockSpec(shape, lambda b, i: (0, 0), pipeline_mode=pl.Buffered(1))
    return pl.pallas_call(
        _mix_ffn_kernel,
        grid=(B, L // tm),
        in_specs=[bm(RET_W), tmj, bm(MLA_W), tmj,
                  resident((4 * 256, D_MODEL)),
                  bm(D_MODEL),
                  pl.BlockSpec((None, None, 1, 6 * D_MODEL), lambda b, i: (l, row(b), 0, 0)),
                  pl.BlockSpec((None, 4, D_MODEL), lambda b, i: (l, 0, 0)),
                  resident((D_MODEL, D_FF)), resident((D_MODEL, D_FF)), resident((D_FF, D_MODEL))],
        out_specs=bm(D_MODEL),
        out_shape=jax.ShapeDtypeStruct((B, L, D_MODEL), F32),
        scratch_shapes=[pltpu.VMEM((tm, D_FF), BF16)],
        compiler_params=_cparams(("parallel", "parallel")),
        name="mix_ffn",
    )(ret, four, att, hy, p["w_out"], x, mod, p["norm_g"], p["w_gate"], p["w_up"], p["w_down"])


def _layer(x, mod, l, p, latent, ctx_kv, s0):
    B, L, _ = x.shape
    t = _tiles(L)
    qkvg, fx, q, k, v, hu, ckv, krp = _in_proj(x, mod, l, p, latent)
    ret, s_new = _retention(qkvg, p["dec_rows"], s0, l, emit_state=not latent)
    four = _mm(_bf16_const(_pos_dft_np(L)), fx.reshape(2 * L, B * FNET_W),
               (FNET_GC * L) ** -0.5, BF16, t["hb"], t["tcol"], "fourier_pos")
    kc, vc = ctx_kv if latent else (None, None)
    att = _attention(q, k, v, kc, vc)
    hy = _hyena(hu, B, p)
    return _mix_ffn(ret, four, att, hy, x, mod, l, p, latent), ckv, krp, s_new


def _prep_layer(l, w):
    zeros = lambda r, c: jnp.zeros((r, c), F32)
    w_in = w["w_in"][l]
    n0 = OFF_KR
    w_in_p = jnp.concatenate(
        [w_in[:, :n0], zeros(D_MODEL, MLA_ROPE_OFF), w_in[:, n0:n0 + MLA_ROPE],
         zeros(D_MODEL, MLA_HP - MLA_ROPE_OFF - MLA_ROPE), w_in[:, n0 + MLA_ROPE:]], axis=1)
    uq = w["mla_w_uq"][l].reshape(MLA_Q_LORA, MLA_HEADS, MLA_NOPE + MLA_ROPE)
    uq = jnp.pad(uq, ((0, 0), (0, 0), (0, MLA_HP - MLA_NOPE - MLA_ROPE))).reshape(MLA_Q_LORA, -1)
    ukv = w["mla_w_ukv"][l].reshape(MLA_KV_LORA, MLA_HEADS, MLA_NOPE + MLA_V)
    uk = jnp.pad(ukv[:, :, :MLA_NOPE], ((0, 0), (0, 0), (0, MLA_HP - MLA_NOPE))).reshape(MLA_KV_LORA, -1)
    uv = ukv[:, :, MLA_NOPE:].reshape(MLA_KV_LORA, -1)
    return dict(
        g0=w["norm_g"][l, 0:1], norm_g=w["norm_g"],
        w_in=w_in_p.astype(BF16), w_out=w["w_out"][l].astype(BF16),
        q_norm=w["mla_q_norm"][l][None], kv_norm=w["mla_kv_norm"][l][None],
        w_uq=uq.astype(BF16), w_ukv=jnp.concatenate([uk, uv], axis=1).astype(BF16),
        dec_rows=w["dec_rows"],
        hy_short_w=w["hy_short_w"][l], hy_short_b=w["hy_short_b"][l][None],
        hy_w1=jnp.pad(w["hy_w1"][l], ((0, HY_EMB_PAD - HY_EMB), (0, 0))), hy_b1=w["hy_b1"][l][None],
        hy_w2=w["hy_w2"][l], hy_b2=w["hy_b2"][l][None], hy_w3=w["hy_w3"][l],
        hy_bias=w["hy_bias"][l],
        w_gate=w["w_gate"][l].astype(BF16), w_up=w["w_up"][l].astype(BF16),
        w_down=w["w_down"][l].astype(BF16),
    )


def kernel(x_prompt, x_sample, cache_ckv, cache_krope, state_ret, c, c_ctx, w_ada, b_ada, norm_g, w_in, w_out,
           ret_decay, mla_q_norm, mla_kv_norm, mla_w_uq, mla_w_ukv, hy_short_w, hy_short_b, hy_w1, hy_b1,
           hy_w2, hy_b2, hy_w3, hy_bias, w_gate, w_up, w_down):
    nb = c.shape[0]
    cvec = jnp.concatenate([c, c_ctx[None], jnp.zeros((MOD_ROWS - nb - 1, D_MODEL), F32)], axis=0)
    mod = _ada(cvec, w_ada, b_ada).reshape(DEPTH, MOD_ROWS, 1, 6 * D_MODEL)
    dec_rows = jnp.pad(jnp.repeat(ret_decay, RET_DV, axis=-1), ((0, 0), (0, 6), (0, 0)))
    krope_pad = jnp.pad(cache_krope, ((0, 0), (0, 0), (0, 0),
                                      (MLA_ROPE_OFF, MLA_HP - MLA_ROPE_OFF - MLA_ROPE)))
    w = dict(norm_g=norm_g, w_in=w_in, w_out=w_out, dec_rows=dec_rows, mla_q_norm=mla_q_norm,
             mla_kv_norm=mla_kv_norm, mla_w_uq=mla_w_uq, mla_w_ukv=mla_w_ukv, hy_short_w=hy_short_w,
             hy_short_b=hy_short_b, hy_w1=hy_w1, hy_b1=hy_b1, hy_w2=hy_w2, hy_b2=hy_b2, hy_w3=hy_w3,
             hy_bias=hy_bias, w_gate=w_gate, w_up=w_up, w_down=w_down)
    xp, xs = x_prompt, x_sample
    new_ckv, new_kr, new_s = [], [], []
    for l in range(DEPTH):
        p = _prep_layer(l, w)
        xp, ckv, krp, s_ret = _layer(xp, mod, l, p, False, None, None)
        new_ckv.append(ckv)
        new_kr.append(krp[:, :, MLA_ROPE_OFF:MLA_ROPE_OFF + MLA_ROPE])
        new_s.append(s_ret)
        ctx_kv = _ctx_kv(cache_ckv, krope_pad, l, p["w_ukv"])
        xs, _, _, _ = _layer(xs, mod, l, p, True, ctx_kv, state_ret)
    return (xp, xs, jnp.stack(new_ckv, axis=1), jnp.stack(new_kr, axis=1), jnp.stack(new_s, axis=1))
```

```python
import functools
import math

import numpy as np
import jax
import jax.numpy as jnp
from jax import lax
from jax.experimental import pallas as pl
from jax.experimental.pallas import tpu as pltpu

F32 = jnp.float32
BF16 = jnp.bfloat16

D_MODEL = 1024
DEPTH = 2
GRID_W = 64
EPS = 1e-6
ROPE_BASE = 10000.0
RET_HEADS = 4
RET_DK = 64
RET_DV = 64
RET_W = RET_HEADS * RET_DV
FNET_GROUPS = 4
FNET_GC = 64
FNET_W = FNET_GROUPS * FNET_GC
MLA_HEADS = 4
MLA_Q_LORA = 256
MLA_KV_LORA = 128
MLA_NOPE = 64
MLA_ROPE = 32
MLA_V = 64
MLA_W = MLA_HEADS * MLA_V
MLA_HP = 128
MLA_ROPE_OFF = MLA_NOPE
ATTN_CK = 256
HY_CH = 256
HY_ORDER = 2
HY_BANDS = 16
HY_EMB = 1 + 2 * HY_BANDS
HY_EMB_PAD = 128
HY_FFN = 64
HY_FAST_DECAY = 0.3
HY_SLOW_DECAY = 1.5
HY_TARGET = 1e-2
D_FF = ((8 * D_MODEL + 3 * 256 - 1) // (3 * 256)) * 256
FFN_CHUNK = 256
MOD_ROWS = 16
CTX_ROW = 8

OFF_RET = 0
OFF_FU = 4 * RET_W
OFF_CQ = OFF_FU + FNET_W
OFF_CKV = OFF_CQ + MLA_Q_LORA
OFF_KR = OFF_CKV + MLA_KV_LORA
OFF_HU = OFF_KR + MLA_HP
IN_WP = OFF_HU + 3 * HY_CH

VMEM_LIMIT = 52 * 1024 * 1024


def _cparams(sem):
    return pltpu.CompilerParams(dimension_semantics=sem, vmem_limit_bytes=VMEM_LIMIT)


def _tiles(L):
    return dict(
        tm=min(L, 512),
        tm_ffn=min(L, 512),
        chunk=min(L, 256),
        tq=min(L, 256),
        hb=min(L, 512),
        tcol=1024,
    )


def _bf16_const(a):
    return jnp.asarray(np.asarray(a, np.float32)).astype(BF16)


@functools.lru_cache(maxsize=None)
def _rope_np(L, width, seg_off, seg_w, rope_dim):
    row = np.repeat(np.arange(L // GRID_W), GRID_W).astype(np.float64)
    col = np.tile(np.arange(GRID_W), L // GRID_W).astype(np.float64)
    quarter = rope_dim // 4
    inv = ROPE_BASE ** (-np.arange(quarter, dtype=np.float64) / quarter)
    cos = np.ones((L, width), np.float64)
    sin = np.zeros((L, width), np.float64)
    for lane in range(width):
        r = lane % seg_w - seg_off
        if r < 0 or r >= rope_dim:
            continue
        pos = row if r < rope_dim // 2 else col
        ang = pos * inv[r % quarter]
        first = (r % (rope_dim // 2)) < quarter
        cos[:, lane] = np.cos(ang)
        sin[:, lane] = -np.sin(ang) if first else np.sin(ang)
    return cos.astype(np.float32), sin.astype(np.float32)


@functools.lru_cache(maxsize=None)
def _chan_dft_np():
    k = np.arange(FNET_GC)
    ang = 2.0 * np.pi * np.outer(k, k) / FNET_GC
    c = np.kron(np.eye(FNET_GROUPS), np.cos(ang))
    s = np.kron(np.eye(FNET_GROUPS), np.sin(ang))
    return np.concatenate([c, s], axis=1)


@functools.lru_cache(maxsize=None)
def _pos_dft_np(L):
    k = np.arange(L, dtype=np.int64)
    ang = 2.0 * np.pi * (np.outer(k, k) % L) / L
    return np.concatenate([np.cos(ang), -np.sin(ang)], axis=1)


@functools.lru_cache(maxsize=None)
def _hyena_dft_np(L, hb):
    k = np.arange(L, dtype=np.int64)
    ang = np.pi * (np.outer(k, k) % (2 * L)) / L
    cf = np.cos(ang)
    sf = np.sin(ang)
    alt = (-1.0) ** np.arange(L)
    sf[0, :] = alt
    w = np.where(k == 0, 1.0, 2.0)[None, :] / (2.0 * L)
    ainv = cf.T * w
    binv = sf.T / L
    binv[:, 0] = alt / (2.0 * L)
    fw, iv = [], []
    for t in range(L // hb):
        sl = slice(t * hb, (t + 1) * hb)
        fw += [cf[sl], sf[sl]]
        iv += [ainv[:, sl], binv[:, sl]]
    return np.concatenate(fw, axis=0), np.concatenate(iv, axis=1)


@functools.lru_cache(maxsize=None)
def _hyena_feat_np(L):
    pos = np.arange(L, dtype=np.float64)
    t = pos / L
    bands = np.arange(1, HY_BANDS + 1, dtype=np.float64)
    ang = (2.0 * math.pi / L) * pos[:, None] * bands[None, :]
    z = np.zeros((L, HY_EMB_PAD), np.float64)
    z[:, 0] = t
    z[:, 1:1 + HY_BANDS] = np.sin(ang)
    z[:, 1 + HY_BANDS:HY_EMB] = np.cos(ang)
    deltas = np.abs(np.linspace(math.log(HY_TARGET) / HY_SLOW_DECAY,
                                math.log(HY_TARGET) / HY_FAST_DECAY, HY_CH))
    window = np.exp(-t[:, None] * deltas[None, :])
    return z.astype(np.float32), window.astype(np.float32)


def _rms(x, g):
    return x * lax.rsqrt(jnp.mean(x * x, axis=-1, keepdims=True) + EPS) * g


def _silu(x):
    return x * jax.nn.sigmoid(x)


def _bdot(a, b):
    return jnp.dot(a.astype(BF16), b.astype(BF16), preferred_element_type=F32)


def _split(a):
    hi = a.astype(BF16)
    lo = (a - hi.astype(F32)).astype(BF16)
    return hi, lo


def _dot3(a, b):
    ah, al = _split(a)
    bh, bl = _split(b)
    d = lambda x, y: jnp.dot(x, y, preferred_element_type=F32)
    return d(ah, bh) + (d(al, bh) + d(ah, bl))


def _store_kv(kv, kr_pad, k_ref, v_ref):
    wk = MLA_HEADS * MLA_HP
    k_ref[...] = (kv[:, :wk] + jnp.concatenate([kr_pad] * MLA_HEADS, axis=1)).astype(k_ref.dtype)
    lane = lax.broadcasted_iota(jnp.int32, (1, wk), 1)
    ones = jnp.where(lane % MLA_HP >= MLA_V, 1.0, 0.0)
    v_ref[...] = (kv[:, wk:] + ones).astype(v_ref.dtype)


def _rope(x, cos, sin_signed, quarter):
    w = x.shape[-1]
    lane = lax.broadcasted_iota(jnp.int32, x.shape, 1)
    first = (lane % (2 * quarter)) < quarter
    up = pltpu.roll(x, w - quarter, axis=1)
    dn = pltpu.roll(x, quarter, axis=1)
    return x * cos + jnp.where(first, up, dn) * sin_signed


def _ada_kernel(c_ref, w_ref, b_ref, o_ref):
    s = _silu(c_ref[...])
    o_ref[...] = _bdot(s, w_ref[...]) + b_ref[...]


def _ada(cvec, w_ada, b_ada):
    tn = 1024
    n6 = w_ada.shape[-1]
    return pl.pallas_call(
        _ada_kernel,
        grid=(DEPTH, n6 // tn),
        in_specs=[
            pl.BlockSpec((MOD_ROWS, D_MODEL), lambda l, j: (0, 0)),
            pl.BlockSpec((None, D_MODEL, tn), lambda l, j: (l, 0, j)),
            pl.BlockSpec((None, 1, tn), lambda l, j: (l, 0, j)),
        ],
        out_specs=pl.BlockSpec((None, MOD_ROWS, tn), lambda l, j: (l, 0, j)),
        out_shape=jax.ShapeDtypeStruct((DEPTH, MOD_ROWS, n6), F32),
        compiler_params=_cparams(("parallel", "parallel")),
        name="ada",
    )(cvec, w_ada, b_ada.reshape(DEPTH, 1, n6))


def _in_proj_kernel(*refs, latent):
    (x_ref, mod_ref, g0_ref, w_ref, qn_ref, kvn_ref, wuq_ref, wukv_ref, bd_ref) = refs[:9]
    pos = 9
    if latent:
        cr_ref, sr_ref, cm_ref, sm_ref = refs[pos:pos + 4]
        pos += 4
        qkvg_ref, fx_ref, q_ref, k_ref, v_ref, hu_ref = refs[pos:pos + 6]
        lanes = lambda ref, reps: jnp.concatenate([ref[...]] * reps, axis=1)
    else:
        qkvg_ref, fx_ref, q_ref, k_ref, v_ref, hu_ref, ckv_ref, kr_ref = refs[pos:pos + 8]

    x = x_ref[...]
    mod = mod_ref[...]
    sh1 = mod[:, 0:D_MODEL]
    sc1 = mod[:, D_MODEL:2 * D_MODEL]
    h = _rms(x, g0_ref[...]) * (1.0 + sc1) + sh1
    proj = jnp.dot(h.astype(BF16), w_ref[...], preferred_element_type=F32)

    rq = proj[:, OFF_RET:OFF_RET + RET_W]
    rk = proj[:, OFF_RET + RET_W:OFF_RET + 2 * RET_W] * (RET_DK ** -0.5)
    if latent:
        cos_r, sin_r = lanes(cr_ref, RET_W // 128), lanes(sr_ref, RET_W // 128)
        rq = _rope(rq, cos_r, sin_r, RET_DK // 4)
        rk = _rope(rk, cos_r, sin_r, RET_DK // 4)
    qkvg_ref[:, 0:RET_W] = rq
    qkvg_ref[:, RET_W:2 * RET_W] = rk
    qkvg_ref[:, 2 * RET_W:4 * RET_W] = proj[:, OFF_RET + 2 * RET_W:OFF_RET + 4 * RET_W]

    fcs = _bdot(proj[:, OFF_FU:OFF_FU + FNET_W], bd_ref[...])
    fx_ref[0] = fcs[:, :FNET_W].astype(fx_ref.dtype)
    fx_ref[1] = fcs[:, FNET_W:].astype(fx_ref.dtype)

    cqn = _rms(proj[:, OFF_CQ:OFF_CQ + MLA_Q_LORA], qn_ref[...])
    q = _bdot(cqn, wuq_ref[...])
    ckvn = _rms(proj[:, OFF_CKV:OFF_CKV + MLA_KV_LORA], kvn_ref[...])
    kv = _bdot(ckvn, wukv_ref[...])
    krp = proj[:, OFF_KR:OFF_KR + MLA_HP]
    if latent:
        q = _rope(q, lanes(cm_ref, MLA_HEADS), lanes(sm_ref, MLA_HEADS), MLA_ROPE // 4)
        krp = _rope(krp, cm_ref[...], sm_ref[...], MLA_ROPE // 4)
    else:
        ckv_ref[...] = ckvn
        kr_ref[...] = krp
    q_ref[...] = (q * ((MLA_NOPE + MLA_ROPE) ** -0.5 * math.log2(math.e))).astype(q_ref.dtype)
    _store_kv(kv, krp, k_ref, v_ref)

    hu_ref[...] = proj[:, OFF_HU:OFF_HU + 3 * HY_CH]


def _in_proj(x, mod, l, p, latent):
    B, L, _ = x.shape
    t = _tiles(L)
    tm = t["tm"]
    nt = L // tm
    row = (lambda b: b) if latent else (lambda b: CTX_ROW)
    full = lambda shape: pl.BlockSpec(shape, lambda i, b: (0,) * len(shape), pipeline_mode=pl.Buffered(1))
    in_specs = [
        pl.BlockSpec((None, tm, D_MODEL), lambda i, b: (b, i, 0)),
        pl.BlockSpec((None, None, 1, 6 * D_MODEL), lambda i, b: (l, row(b), 0, 0)),
        full((1, D_MODEL)),
        full((D_MODEL, IN_WP)),
        full((1, MLA_Q_LORA)),
        full((1, MLA_KV_LORA)),
        full((MLA_Q_LORA, MLA_HEADS * MLA_HP)),
        full((MLA_KV_LORA, 2 * MLA_HEADS * MLA_HP)),
        full((FNET_W, 2 * FNET_W)),
    ]
    args = [x, mod, p["g0"], p["w_in"], p["q_norm"], p["kv_norm"], p["w_uq"], p["w_ukv"],
            _bf16_const(_chan_dft_np())]
    if latent:
        tabs = (_rope_np(L, 128, 0, RET_DK, RET_DK)
                + _rope_np(L, MLA_HP, MLA_ROPE_OFF, MLA_HP, MLA_ROPE))
        for tab in tabs:
            in_specs.append(pl.BlockSpec((tm, tab.shape[1]), lambda i, b: (i, 0)))
            args.append(jnp.asarray(tab))
    bm = lambda w: pl.BlockSpec((None, tm, w), lambda i, b: (b, i, 0))
    sds = jax.ShapeDtypeStruct
    out_specs = [
        bm(4 * RET_W),
        pl.BlockSpec((2, tm, FNET_W), lambda i, b: (0, i, b)),
        bm(MLA_HEADS * MLA_HP), bm(MLA_HEADS * MLA_HP), bm(MLA_HEADS * MLA_HP),
        pl.BlockSpec((tm, 3 * HY_CH), lambda i, b: (i, b)),
    ]
    out_shape = [
        sds((B, L, 4 * RET_W), F32),
        sds((2, L, B * FNET_W), BF16),
        sds((B, L, MLA_HEADS * MLA_HP), BF16),
        sds((B, L, MLA_HEADS * MLA_HP), BF16),
        sds((B, L, MLA_HEADS * MLA_HP), BF16),
        sds((L, B * 3 * HY_CH), F32),
    ]
    if not latent:
        out_specs += [bm(MLA_KV_LORA), bm(MLA_HP)]
        out_shape += [sds((B, L, MLA_KV_LORA), F32), sds((B, L, MLA_HP), F32)]
    outs = pl.pallas_call(
        functools.partial(_in_proj_kernel, latent=latent),
        grid=(nt, B), in_specs=in_specs, out_specs=out_specs, out_shape=out_shape,
        compiler_params=_cparams(("parallel", "parallel")),
        name="in_proj_lat" if latent else "in_proj_ctx",
    )(*args)
    return tuple(outs) + ((None, None) if latent else ())


def _ctx_kv_kernel(ckv_ref, kr_ref, w_ref, k_ref, v_ref):
    _store_kv(_bdot(ckv_ref[...], w_ref[...]), kr_ref[...], k_ref, v_ref)


def _ctx_kv(cache_ckv, krope_pad, l, w_ukv):
    B, _, P, _ = cache_ckv.shape
    return pl.pallas_call(
        _ctx_kv_kernel,
        grid=(B,),
        in_specs=[
            pl.BlockSpec((None, None, P, MLA_KV_LORA), lambda b: (b, l, 0, 0)),
            pl.BlockSpec((None, None, P, MLA_HP), lambda b: (b, l, 0, 0)),
            pl.BlockSpec(w_ukv.shape, lambda b: (0, 0)),
        ],
        out_specs=[
            pl.BlockSpec((None, P, MLA_HEADS * MLA_HP), lambda b: (b, 0, 0)),
            pl.BlockSpec((None, P, MLA_HEADS * MLA_HP), lambda b: (b, 0, 0)),
        ],
        out_shape=[
            jax.ShapeDtypeStruct((B, P, MLA_HEADS * MLA_HP), BF16),
            jax.ShapeDtypeStruct((B, P, MLA_HEADS * MLA_HP), BF16),
        ],
        compiler_params=_cparams(("parallel",)),
        name="ctx_kv",
    )(cache_ckv, krope_pad, w_ukv)


def _ret_kernel(*refs, L, C, has_s0, emit_state):
    q_ref, k_ref, v_ref, g_ref, dec_ref = refs[:5]
    pos = 5
    if has_s0:
        s0_ref = refs[pos]
        pos += 1
    o_ref = refs[pos]
    pos += 1
    if emit_state:
        sn_ref = refs[pos]
        pos += 1
    sf_ref, sb_ref, dm_ref, tab_ref = refs[pos:pos + 4]
    n = L // C
    H, W = RET_HEADS, RET_W

    d = dec_ref[0:2, :]
    lg = jnp.minimum(d, 0.0) - jnp.log1p(jnp.exp(-jnp.abs(d)))
    lf, lb = lg[0:1, :], lg[1:2, :]
    ii = lax.broadcasted_iota(jnp.int32, (C, C), 0)
    jj = lax.broadcasted_iota(jnp.int32, (C, C), 1)
    diff = (ii - jj).astype(F32)
    for h in range(H):
        lfh = lf[:, h * RET_DV:h * RET_DV + 1]
        lbh = lb[:, h * RET_DV:h * RET_DV + 1]
        dm_ref[h] = jnp.where(diff > 0, jnp.exp(diff * lfh),
                              jnp.where(diff < 0, jnp.exp(-diff * lbh), 2.0))
    ci = lax.broadcasted_iota(jnp.int32, (C, W), 0).astype(F32)
    tab_ref[0] = jnp.exp((ci + 1.0) * lf)
    tab_ref[1] = jnp.exp((C - ci) * lb)
    tab_ref[2] = jnp.exp((C - 1.0 - ci) * lf)
    tab_ref[3] = jnp.exp(ci * lb)
    g_f = jnp.exp(C * lf)
    g_b = jnp.exp(C * lb)
    hr = lax.broadcasted_iota(jnp.int32, (W, W), 0) // RET_DK
    hc = lax.broadcasted_iota(jnp.int32, (W, W), 1) // RET_DV
    same_head = hr == hc
    avg = jnp.where(same_head, 1.0 / RET_DV, 0.0).astype(BF16)

    sf_ref[0] = jnp.zeros((W, W), F32)
    sb_ref[n] = jnp.zeros((W, W), F32)
    if has_s0:
        for h in range(H):
            blk = slice(h * RET_DK, (h + 1) * RET_DK)
            sf_ref[0, blk, blk] = s0_ref[0, h]
            sb_ref[n, blk, blk] = s0_ref[1, h]

    tdot = lambda a, b: lax.dot_general(a.astype(BF16), b.astype(BF16), (((0,), (0,)), ((), ())),
                                        preferred_element_type=F32)

    def states(j, carry):
        jb = n - 1 - j
        rf = pl.multiple_of(j * C, C)
        rb = pl.multiple_of(jb * C, C)
        upd = tdot(k_ref[pl.ds(rf, C), :] * tab_ref[2], v_ref[pl.ds(rf, C), :])
        sf_ref[j + 1] = g_f * sf_ref[j] + jnp.where(same_head, upd, 0.0)
        upd = tdot(k_ref[pl.ds(rb, C), :] * tab_ref[3], v_ref[pl.ds(rb, C), :])
        sb_ref[jb] = g_b * sb_ref[jb + 1] + jnp.where(same_head, upd, 0.0)
        return carry

    lax.fori_loop(0, n, states, 0)

    lane_head = lax.broadcasted_iota(jnp.int32, (C, W), 1) // RET_DV

    def group_mean(x):
        hi, lo = _split(x)
        return (jnp.dot(hi, avg, preferred_element_type=F32)
                + jnp.dot(lo, avg, preferred_element_type=F32))

    def chunk(j, carry):
        r0 = pl.multiple_of(j * C, C)
        q = q_ref[pl.ds(r0, C), :]
        qb = q.astype(BF16)
        kb = k_ref[pl.ds(r0, C), :].astype(BF16)
        vb = v_ref[pl.ds(r0, C), :].astype(BF16)
        o = tab_ref[0] * jnp.dot(qb, sf_ref[j].astype(BF16), preferred_element_type=F32)
        o = o + tab_ref[1] * jnp.dot(qb, sb_ref[j + 1].astype(BF16), preferred_element_type=F32)
        inner = jnp.zeros((C, W), F32)
        for h in range(H):
            in_head = lane_head == h
            qh = jnp.where(in_head, q, 0.0).astype(BF16)
            s = lax.dot_general(qh, kb, (((1,), (1,)), ((), ())), preferred_element_type=F32)
            oh = jnp.dot((s * dm_ref[h]).astype(BF16), vb, preferred_element_type=F32)
            inner = jnp.where(in_head, oh, inner)
        o = o + inner
        dlt = o - group_mean(o)
        var = group_mean(dlt * dlt)
        gate = g_ref[pl.ds(r0, C), :]
        o_ref[pl.ds(r0, C), :] = (_silu(gate) * (dlt * lax.rsqrt(var + EPS))).astype(o_ref.dtype)
        return carry

    lax.fori_loop(0, n, chunk, 0)

    if emit_state:
        for h in range(H):
            blk = slice(h * RET_DK, (h + 1) * RET_DK)
            sn_ref[0, h] = sf_ref[n, blk, blk]
            sn_ref[1, h] = sb_ref[0, blk, blk]


def _retention(qkvg, dec_rows, s0, l, emit_state):
    B, L, _ = qkvg.shape
    C = _tiles(L)["chunk"]
    n = L // C
    has_s0 = s0 is not None
    def colspec(j):
        return pl.BlockSpec((None, L, RET_W), lambda b: (b, 0, j))

    in_specs = [colspec(j) for j in range(4)]
    in_specs.append(pl.BlockSpec((None, 8, RET_W), lambda b: (l, 0, 0)))
    args = [qkvg, qkvg, qkvg, qkvg, dec_rows]
    st_block = (None, None, 2, RET_HEADS, RET_DK, RET_DV)
    if has_s0:
        in_specs.append(pl.BlockSpec(st_block, lambda b: (b, l, 0, 0, 0, 0)))
        args.append(s0)
    out_specs = [pl.BlockSpec((None, L, RET_W), lambda b: (b, 0, 0))]
    out_shape = [jax.ShapeDtypeStruct((B, L, RET_W), BF16)]
    if emit_state:
        out_specs.append(pl.BlockSpec((None, 2, RET_HEADS, RET_DK, RET_DV), lambda b: (b, 0, 0, 0, 0)))
        out_shape.append(jax.ShapeDtypeStruct((B, 2, RET_HEADS, RET_DK, RET_DV), F32))
    outs = pl.pallas_call(
        functools.partial(_ret_kernel, L=L, C=C, has_s0=has_s0, emit_state=emit_state),
        grid=(B,), in_specs=in_specs, out_specs=out_specs, out_shape=out_shape,
        scratch_shapes=[
            pltpu.VMEM((n + 1, RET_W, RET_W), F32),
            pltpu.VMEM((n + 1, RET_W, RET_W), F32),
            pltpu.VMEM((RET_HEADS, C, C), F32),
            pltpu.VMEM((4, C, RET_W), F32),
        ],
        compiler_params=_cparams(("parallel",)),
        name="retention_lat" if has_s0 else "retention_ctx",
    )(*args)
    return outs if emit_state else (outs[0], None)


def _mm_kernel(a_ref, b_ref, o_ref, *, scale):
    o_ref[...] = (jnp.dot(a_ref[...], b_ref[...], preferred_element_type=F32) * scale).astype(o_ref.dtype)


def _mm(a, b, scale, out_dtype, tm, tn, name):
    M, K = a.shape
    _, N = b.shape
    tm, tn = min(tm, M), min(tn, N)
    return pl.pallas_call(
        functools.partial(_mm_kernel, scale=scale),
        grid=(N // tn, M // tm),
        in_specs=[pl.BlockSpec((tm, K), lambda j, i: (i, 0)),
                  pl.BlockSpec((K, tn), lambda j, i: (0, j))],
        out_specs=pl.BlockSpec((tm, tn), lambda j, i: (i, j)),
        out_shape=jax.ShapeDtypeStruct((M, N), out_dtype),
        compiler_params=_cparams(("parallel", "parallel")),
        name=name,
    )(a, b)


def _attn_kernel(*refs, has_ctx):
    if has_ctx:
        q_ref, k_ref, v_ref, kc_ref, vc_ref, o_ref, s_ref = refs
        srcs = ((k_ref, v_ref), (kc_ref, vc_ref))
    else:
        q_ref, k_ref, v_ref, o_ref, s_ref = refs
        srcs = ((k_ref, v_ref),)
    nt = lambda a, b: lax.dot_general(a, b, (((1,), (1,)), ((), ())), preferred_element_type=F32)
    chunks = [(kr, vr, c0) for kr, vr in srcs for c0 in range(0, kr.shape[0], ATTN_CK)]
    def scores(h):
        hq = slice(h * MLA_HP, (h + 1) * MLA_HP)
        qh = q_ref[:, hq]
        mx = None
        for n, (kr, _, c0) in enumerate(chunks):
            s = nt(qh, kr[c0:c0 + ATTN_CK, hq])
            s_ref[h % 2, :, n * ATTN_CK:(n + 1) * ATTN_CK] = s
            for g in range(ATTN_CK // 128):
                part = s[:, g * 128:(g + 1) * 128]
                mx = part if mx is None else jnp.maximum(mx, part)
        return jnp.max(mx, axis=-1, keepdims=True)

    def weighted_values(h, m):
        hq = slice(h * MLA_HP, (h + 1) * MLA_HP)
        acc = None
        for n, (_, vr, c0) in enumerate(chunks):
            p = jnp.exp2(s_ref[h % 2, :, n * ATTN_CK:(n + 1) * ATTN_CK] - m).astype(BF16)
            d = jnp.dot(p, vr[c0:c0 + ATTN_CK, hq], preferred_element_type=F32)
            acc = d if acc is None else acc + d
        o_ref[:, h * MLA_V:(h + 1) * MLA_V] = (acc[:, :MLA_V] / acc[:, MLA_V:2 * MLA_V]).astype(o_ref.dtype)

    m = scores(0)
    for h in range(MLA_HEADS):
        m_next = scores(h + 1) if h + 1 < MLA_HEADS else None
        weighted_values(h, m)
        m = m_next


def _attention(q, k, v, kc, vc):
    B, L, _ = q.shape
    tq = _tiles(L)["tq"]
    has_ctx = kc is not None
    whole = lambda a: pl.BlockSpec((None,) + a.shape[1:], lambda b, i: (b, 0, 0))
    in_specs = [pl.BlockSpec((None, tq, q.shape[2]), lambda b, i: (b, i, 0)), whole(k), whole(v)]
    args = [q, k, v]
    n_keys = L
    if has_ctx:
        in_specs += [whole(kc), whole(vc)]
        args += [kc, vc]
        n_keys += kc.shape[1]
    return pl.pallas_call(
        functools.partial(_attn_kernel, has_ctx=has_ctx),
        grid=(B, L // tq), in_specs=in_specs,
        out_specs=pl.BlockSpec((None, tq, MLA_W), lambda b, i: (b, i, 0)),
        out_shape=jax.ShapeDtypeStruct((B, L, MLA_W), BF16),
        scratch_shapes=[pltpu.VMEM((2, tq, n_keys), F32)],
        compiler_params=_cparams(("parallel", "parallel")),
        name="attention_lat" if has_ctx else "attention_ctx",
    )(*args)


def _hy_filter_kernel(z_ref, w1_ref, b1_ref, w2_ref, b2_ref, w3_ref, win_ref, x_ref):
    h1 = jnp.sin(_dot3(z_ref[...], w1_ref[...]) + b1_ref[...])
    h2 = jnp.sin(_dot3(h1, w2_ref[...]) + b2_ref[...])
    h = _dot3(h2, w3_ref[...])
    win = win_ref[...]
    hf = h[:, :HY_CH] * win
    hb = h[:, HY_CH:] * win
    row = lax.broadcasted_iota(jnp.int32, hb.shape, 0)
    hb = jnp.where(row == 0, 0.0, hb)
    nrm = jnp.sum(jnp.abs(hf), axis=0, keepdims=True) + jnp.sum(jnp.abs(hb), axis=0, keepdims=True) + EPS
    inv = 1.0 / nrm
    x_ref[:, :HY_CH] = (hf + hb) * inv
    x_ref[:, HY_CH:] = (hf - hb) * inv


def _hy_filter(L, p):
    z, win = _hyena_feat_np(L)
    full = lambda shape: pl.BlockSpec(shape, lambda o: (0,) * len(shape))
    return pl.pallas_call(
        _hy_filter_kernel,
        grid=(HY_ORDER,),
        in_specs=[full((L, HY_EMB_PAD)), full((HY_EMB_PAD, HY_FFN)), full((1, HY_FFN)),
                  full((HY_FFN, HY_FFN)), full((1, HY_FFN)),
                  pl.BlockSpec((HY_FFN, 2 * HY_CH), lambda o: (0, o)),
                  full((L, HY_CH))],
        out_specs=pl.BlockSpec((L, 2 * HY_CH), lambda o: (0, o)),
        out_shape=jax.ShapeDtypeStruct((L, HY_ORDER * 2 * HY_CH), F32),
        compiler_params=_cparams(("parallel",)),
        name="hy_filter",
    )(jnp.asarray(z), p["hy_w1"], p["hy_b1"], p["hy_w2"], p["hy_b2"], p["hy_w3"], jnp.asarray(win))


def _hy_gspec_kernel(f_ref, x_ref, g_ref, *, hb):
    xh, xl = _split(x_ref[...])
    f = f_ref[...]
    r = jnp.dot(f, xh, preferred_element_type=F32) + jnp.dot(f, xl, preferred_element_type=F32)
    ga = r[:hb, :HY_CH]
    gb = r[hb:, HY_CH:]
    nyq = r[hb:hb + 1, :HY_CH]
    k = pl.program_id(1) * hb + lax.broadcasted_iota(jnp.int32, (hb, HY_CH), 0)
    g_ref[0] = ga
    g_ref[1] = jnp.where(k == 0, 0.0, gb)
    g_ref[2] = jnp.where(k == 0, jnp.broadcast_to(nyq, ga.shape), ga)


def _hy_gspec(fwd, x, L, hb):
    return pl.pallas_call(
        functools.partial(_hy_gspec_kernel, hb=hb),
        grid=(HY_ORDER, L // hb),
        in_specs=[pl.BlockSpec((2 * hb, L), lambda o, t: (t, 0)),
                  pl.BlockSpec((L, 2 * HY_CH), lambda o, t: (0, o))],
        out_specs=pl.BlockSpec((3, hb, HY_CH), lambda o, t: (0, t, o)),
        out_shape=jax.ShapeDtypeStruct((3, L, HY_ORDER * HY_CH), F32),
        compiler_params=_cparams(("parallel", "parallel")),
        name="hy_gspec",
    )(fwd, x)


def _hy_conv_kernel(u_ref, w_ref, b_ref, v_ref, vb_ref, x1_ref, x2_ref):
    u = u_ref[...]
    L = u.shape[0]
    row = lax.broadcasted_iota(jnp.int32, u.shape, 0)
    prev = jnp.where(row == 0, 0.0, pltpu.roll(u, 1, axis=0))
    nxt = jnp.where(row == L - 1, 0.0, pltpu.roll(u, L - 1, axis=0))
    w = w_ref[...]
    uc = prev * w[0:1] + u * w[1:2] + nxt * w[2:3] + b_ref[...]
    v_ref[...] = uc[:, :HY_CH]
    vb_ref[...] = uc[:, :HY_CH].astype(vb_ref.dtype)
    x1_ref[...] = uc[:, HY_CH:2 * HY_CH]
    x2_ref[...] = uc[:, 2 * HY_CH:]


def _hy_conv(hu, B, w, b):
    L = hu.shape[0]
    col = pl.BlockSpec((L, HY_CH), lambda bb: (0, bb))
    sds = lambda dt: jax.ShapeDtypeStruct((L, B * HY_CH), dt)
    return pl.pallas_call(
        _hy_conv_kernel,
        grid=(B,),
        in_specs=[pl.BlockSpec((L, 3 * HY_CH), lambda bb: (0, bb)),
                  pl.BlockSpec((3, 3 * HY_CH), lambda bb: (0, 0)),
                  pl.BlockSpec((1, 3 * HY_CH), lambda bb: (0, 0))],
        out_specs=[col, col, col, col],
        out_shape=[sds(F32), sds(BF16), sds(F32), sds(F32)],
        compiler_params=_cparams(("parallel",)),
        name="hy_conv",
    )(hu, w, b)


def _hy_fwd_kernel(f_ref, u_ref, g_ref, y_ref, *, hb):
    acc = jnp.dot(f_ref[...], u_ref[...], preferred_element_type=F32)
    ga, gb, gd = g_ref[0], g_ref[1], g_ref[2]
    for c in range(u_ref.shape[1] // HY_CH):
        sl = slice(c * HY_CH, (c + 1) * HY_CH)
        ur = acc[:hb, sl]
        us = acc[hb:, sl]
        y_ref[:hb, sl] = (ur * ga - us * gb).astype(y_ref.dtype)
        y_ref[hb:, sl] = (ur * gb + us * gd).astype(y_ref.dtype)


def _hy_fwd(fwd, u, g, o, hb, tn):
    M, L = fwd.shape
    N = u.shape[1]
    tn = min(tn, N)
    return pl.pallas_call(
        functools.partial(_hy_fwd_kernel, hb=hb),
        grid=(N // tn, L // hb),
        in_specs=[pl.BlockSpec((2 * hb, L), lambda j, t: (t, 0)),
                  pl.BlockSpec((L, tn), lambda j, t: (0, j)),
                  pl.BlockSpec((3, hb, HY_CH), lambda j, t: (0, t, o))],
        out_specs=pl.BlockSpec((2 * hb, tn), lambda j, t: (t, j)),
        out_shape=jax.ShapeDtypeStruct((M, N), BF16),
        compiler_params=_cparams(("parallel", "parallel")),
        name="hy_fwd",
    )(fwd, u, g)


def _hy_inv_kernel(f_ref, y_ref, gate_ref, z_ref, d_ref, *out_refs):
    y = jnp.dot(f_ref[...], y_ref[...], preferred_element_type=F32)
    d = d_ref[...]
    for c in range(y.shape[1] // HY_CH):
        sl = slice(c * HY_CH, (c + 1) * HY_CH)
        z = gate_ref[:, sl] * (y[:, sl] + d * z_ref[:, sl])
        for r in out_refs:
            r[:, sl] = z.astype(r.dtype)


def _hy_inv(inv, y, gate, zprev, hy_bias, o, tm, tn, out_dtypes):
    L, K = inv.shape
    N = y.shape[1]
    tm, tn = min(tm, L), min(tn, N)
    tile = pl.BlockSpec((tm, tn), lambda j, i: (i, j))
    return pl.pallas_call(
        _hy_inv_kernel,
        grid=(N // tn, L // tm),
        in_specs=[pl.BlockSpec((tm, K), lambda j, i: (i, 0)),
                  pl.BlockSpec((K, tn), lambda j, i: (0, j)),
                  tile, tile,
                  pl.BlockSpec((1, HY_CH), lambda j, i: (0, 0))],
        out_specs=[tile] * len(out_dtypes),
        out_shape=[jax.ShapeDtypeStruct((L, N), dt) for dt in out_dtypes],
        compiler_params=_cparams(("parallel", "parallel")),
        name="hy_inv",
    )(inv, y, gate, zprev, hy_bias[o:o + 1])


def _hyena(hu, B, p):
    L = hu.shape[0]
    t = _tiles(L)
    hb, tn = t["hb"], t["tcol"]
    fw_np, iv_np = _hyena_dft_np(L, hb)
    fwd, inv = _bf16_const(fw_np), _bf16_const(iv_np)
    g = _hy_gspec(fwd, _hy_filter(L, p), L, hb)
    v, vb, x1, x2 = _hy_conv(hu, B, p["hy_short_w"], p["hy_short_b"])
    y0 = _hy_fwd(fwd, vb, g, 0, hb, tn)
    z1, z1b = _hy_inv(inv, y0, x1, v, p["hy_bias"], 0, hb, tn, (F32, BF16))
    y1 = _hy_fwd(fwd, z1b, g, 1, hb, tn)
    (z2,) = _hy_inv(inv, y1, x2, z1, p["hy_bias"], 1, hb, tn, (BF16,))
    return z2


def _mix_ffn_kernel(ret_ref, four_ref, att_ref, hy_ref, wo_ref, x_ref, mod_ref, g_ref,
                    wg_ref, wu_ref, wd_ref, o_ref, hm_ref):
    mod = mod_ref[...]
    g1 = mod[:, 2 * D_MODEL:3 * D_MODEL]
    sh2 = mod[:, 3 * D_MODEL:4 * D_MODEL]
    sc2 = mod[:, 4 * D_MODEL:5 * D_MODEL]
    g2 = mod[:, 5 * D_MODEL:6 * D_MODEL]
    g = g_ref[...]

    def mix(rows):
        d = lambda a, i: jnp.dot(a[rows, :], wo_ref[i * 256:(i + 1) * 256, :], preferred_element_type=F32)
        m = d(ret_ref, 0) + d(four_ref, 1) + d(att_ref, 2) + d(hy_ref, 3)
        x1 = x_ref[rows, :] + g1 * _rms(m, g[1:2])
        o_ref[rows, :] = x1
        return (_rms(x1, g[2:3]) * (1.0 + sc2) + sh2).astype(BF16)

    def hidden(rows, h2):
        for c in range(D_FF // FFN_CHUNK):
            cs = slice(c * FFN_CHUNK, (c + 1) * FFN_CHUNK)
            a = jnp.dot(h2, wg_ref[:, cs], preferred_element_type=F32)
            u = jnp.dot(h2, wu_ref[:, cs], preferred_element_type=F32)
            hm_ref[rows, cs] = (_silu(a) * u).astype(BF16)

    def down(rows):
        f = jnp.dot(hm_ref[rows, :], wd_ref[...], preferred_element_type=F32)
        o_ref[rows, :] = o_ref[rows, :] + g2 * _rms(f, g[3:4])

    half = o_ref.shape[0] // 2
    ra, rb = slice(0, half), slice(half, 2 * half)
    hidden(ra, mix(ra))
    h2b = mix(rb)
    down(ra)
    hidden(rb, h2b)
    down(rb)


def _mix_ffn(ret, four, att, hy, x, mod, l, p, latent):
    B, L, _ = x.shape
    tm = _tiles(L)["tm_ffn"]
    row = (lambda b: b) if latent else (lambda b: CTX_ROW)
    bm = lambda w: pl.BlockSpec((None, tm, w), lambda b, i: (b, i, 0))
    tmj = pl.BlockSpec((tm, 256), lambda b, i: (i, b))
    resident = lambda shape: pl.BlockSpec(shape, lambda b, i: (0, 0), pipeline_mode=pl.Buffered(1))
    return pl.pallas_call(
        _mix_ffn_kernel,
        grid=(B, L // tm),
        in_specs=[bm(RET_W), tmj, bm(MLA_W), tmj,
                  resident((4 * 256, D_MODEL)),
                  bm(D_MODEL),
                  pl.BlockSpec((None, None, 1, 6 * D_MODEL), lambda b, i: (l, row(b), 0, 0)),
                  pl.BlockSpec((None, 4, D_MODEL), lambda b, i: (l, 0, 0)),
                  resident((D_MODEL, D_FF)), resident((D_MODEL, D_FF)), resident((D_FF, D_MODEL))],
        out_specs=bm(D_MODEL),
        out_shape=jax.ShapeDtypeStruct((B, L, D_MODEL), F32),
        scratch_shapes=[pltpu.VMEM((tm, D_FF), BF16)],
        compiler_params=_cparams(("parallel", "parallel")),
        name="mix_ffn",
    )(ret, four, att, hy, p["w_out"], x, mod, p["norm_g"], p["w_gate"], p["w_up"], p["w_down"])


def _layer(x, mod, l, p, latent, ctx_kv, s0):
    B, L, _ = x.shape
    t = _tiles(L)
    qkvg, fx, q, k, v, hu, ckv, krp = _in_proj(x, mod, l, p, latent)
    ret, s_new = _retention(qkvg, p["dec_rows"], s0, l, emit_state=not latent)
    four = _mm(_bf16_const(_pos_dft_np(L)), fx.reshape(2 * L, B * FNET_W),
               (FNET_GC * L) ** -0.5, BF16, t["hb"], t["tcol"], "fourier_pos")
    kc, vc = ctx_kv if latent else (None, None)
    att = _attention(q, k, v, kc, vc)
    hy = _hyena(hu, B, p)
    return _mix_ffn(ret, four, att, hy, x, mod, l, p, latent), ckv, krp, s_new


def _prep_layer(l, w):
    zeros = lambda r, c: jnp.zeros((r, c), F32)
    w_in = w["w_in"][l]
    n0 = OFF_KR
    w_in_p = jnp.concatenate(
        [w_in[:, :n0], zeros(D_MODEL, MLA_ROPE_OFF), w_in[:, n0:n0 + MLA_ROPE],
         zeros(D_MODEL, MLA_HP - MLA_ROPE_OFF - MLA_ROPE), w_in[:, n0 + MLA_ROPE:]], axis=1)
    uq = w["mla_w_uq"][l].reshape(MLA_Q_LORA, MLA_HEADS, MLA_NOPE + MLA_ROPE)
    uq = jnp.pad(uq, ((0, 0), (0, 0), (0, MLA_HP - MLA_NOPE - MLA_ROPE))).reshape(MLA_Q_LORA, -1)
    ukv = w["mla_w_ukv"][l].reshape(MLA_KV_LORA, MLA_HEADS, MLA_NOPE + MLA_V)
    uk = jnp.pad(ukv[:, :, :MLA_NOPE], ((0, 0), (0, 0), (0, MLA_HP - MLA_NOPE))).reshape(MLA_KV_LORA, -1)
    uv = jnp.pad(ukv[:, :, MLA_NOPE:], ((0, 0), (0, 0), (0, MLA_HP - MLA_V))).reshape(MLA_KV_LORA, -1)
    return dict(
        g0=w["norm_g"][l, 0:1], norm_g=w["norm_g"],
        w_in=w_in_p.astype(BF16), w_out=w["w_out"][l].astype(BF16),
        q_norm=w["mla_q_norm"][l][None], kv_norm=w["mla_kv_norm"][l][None],
        w_uq=uq.astype(BF16), w_ukv=jnp.concatenate([uk, uv], axis=1).astype(BF16),
        dec_rows=w["dec_rows"],
        hy_short_w=w["hy_short_w"][l], hy_short_b=w["hy_short_b"][l][None],
        hy_w1=jnp.pad(w["hy_w1"][l], ((0, HY_EMB_PAD - HY_EMB), (0, 0))), hy_b1=w["hy_b1"][l][None],
        hy_w2=w["hy_w2"][l], hy_b2=w["hy_b2"][l][None], hy_w3=w["hy_w3"][l],
        hy_bias=w["hy_bias"][l],
        w_gate=w["w_gate"][l].astype(BF16), w_up=w["w_up"][l].astype(BF16),
        w_down=w["w_down"][l].astype(BF16),
    )


def kernel(x_prompt, x_sample, cache_ckv, cache_krope, state_ret, c, c_ctx, w_ada, b_ada, norm_g, w_in, w_out,
           ret_decay, mla_q_norm, mla_kv_norm, mla_w_uq, mla_w_ukv, hy_short_w, hy_short_b, hy_w1, hy_b1,
           hy_w2, hy_b2, hy_w3, hy_bias, w_gate, w_up, w_down):
    nb = c.shape[0]
    cvec = jnp.concatenate([c, c_ctx[None], jnp.zeros((MOD_ROWS - nb - 1, D_MODEL), F32)], axis=0)
    mod = _ada(cvec, w_ada, b_ada).reshape(DEPTH, MOD_ROWS, 1, 6 * D_MODEL)
    dec_rows = jnp.pad(jnp.repeat(ret_decay, RET_DV, axis=-1), ((0, 0), (0, 6), (0, 0)))
    krope_pad = jnp.pad(cache_krope, ((0, 0), (0, 0), (0, 0),
                                      (MLA_ROPE_OFF, MLA_HP - MLA_ROPE_OFF - MLA_ROPE)))
    w = dict(norm_g=norm_g, w_in=w_in, w_out=w_out, dec_rows=dec_rows, mla_q_norm=mla_q_norm,
             mla_kv_norm=mla_kv_norm, mla_w_uq=mla_w_uq, mla_w_ukv=mla_w_ukv, hy_short_w=hy_short_w,
             hy_short_b=hy_short_b, hy_w1=hy_w1, hy_b1=hy_b1, hy_w2=hy_w2, hy_b2=hy_b2, hy_w3=hy_w3,
             hy_bias=hy_bias, w_gate=w_gate, w_up=w_up, w_down=w_down)
    xp, xs = x_prompt, x_sample
    new_ckv, new_kr, new_s = [], [], []
    for l in range(DEPTH):
        p = _prep_layer(l, w)
        xp, ckv, krp, s_ret = _layer(xp, mod, l, p, False, None, None)
        new_ckv.append(ckv)
        new_kr.append(krp[:, :, MLA_ROPE_OFF:MLA_ROPE_OFF + MLA_ROPE])
        new_s.append(s_ret)
        ctx_kv = _ctx_kv(cache_ckv, krope_pad, l, p["w_ukv"])
        xs, _, _, _ = _layer(xs, mod, l, p, True, ctx_kv, state_ret)
    return (xp, xs, jnp.stack(new_ckv, axis=1), jnp.stack(new_kr, axis=1), jnp.stack(new_s, axis=1))
```

```python
import functools
import math

import numpy as np
import jax
import jax.numpy as jnp
from jax import lax
from jax.experimental import pallas as pl
from jax.experimental.pallas import tpu as pltpu

F32 = jnp.float32
BF16 = jnp.bfloat16

D_MODEL = 1024
DEPTH = 2
GRID_W = 64
EPS = 1e-6
ROPE_BASE = 10000.0
RET_HEADS = 4
RET_DK = 64
RET_DV = 64
RET_W = RET_HEADS * RET_DV
FNET_GROUPS = 4
FNET_GC = 64
FNET_W = FNET_GROUPS * FNET_GC
MLA_HEADS = 4
MLA_Q_LORA = 256
MLA_KV_LORA = 128
MLA_NOPE = 64
MLA_ROPE = 32
MLA_V = 64
MLA_W = MLA_HEADS * MLA_V
MLA_HP = 128
MLA_ROPE_OFF = MLA_NOPE
ATTN_CK = 256
SUBL = 8
FFT_TN = 256
FFT_S = 64
FFT_MIN_L = 1024
HY_CH = 256
HY_ORDER = 2
HY_BANDS = 16
HY_EMB = 1 + 2 * HY_BANDS
HY_EMB_PAD = 128
HY_FFN = 64
HY_FAST_DECAY = 0.3
HY_SLOW_DECAY = 1.5
HY_TARGET = 1e-2
D_FF = ((8 * D_MODEL + 3 * 256 - 1) // (3 * 256)) * 256
FFN_CHUNK = 256
MOD_ROWS = 16
CTX_ROW = 8

OFF_RET = 0
OFF_FU = 4 * RET_W
OFF_CQ = OFF_FU + FNET_W
OFF_CKV = OFF_CQ + MLA_Q_LORA
OFF_KR = OFF_CKV + MLA_KV_LORA
OFF_HU = OFF_KR + MLA_HP
IN_WP = OFF_HU + 3 * HY_CH

VMEM_LIMIT = 52 * 1024 * 1024


def _cparams(sem):
    return pltpu.CompilerParams(dimension_semantics=sem, vmem_limit_bytes=VMEM_LIMIT)


def _tiles(L):
    return dict(
        tm=min(L, 512),
        tm_ffn=min(L, 512),
        chunk=min(L, 256),
        tq=min(L, 256),
        hb=min(L, 512),
        tcol=1024,
    )


def _bf16_const(a):
    return jnp.asarray(np.asarray(a, np.float32)).astype(BF16)


@functools.lru_cache(maxsize=None)
def _rope_np(L, width, seg_off, seg_w, rope_dim):
    row = np.repeat(np.arange(L // GRID_W), GRID_W).astype(np.float64)
    col = np.tile(np.arange(GRID_W), L // GRID_W).astype(np.float64)
    quarter = rope_dim // 4
    inv = ROPE_BASE ** (-np.arange(quarter, dtype=np.float64) / quarter)
    cos = np.ones((L, width), np.float64)
    sin = np.zeros((L, width), np.float64)
    for lane in range(width):
        r = lane % seg_w - seg_off
        if r < 0 or r >= rope_dim:
            continue
        pos = row if r < rope_dim // 2 else col
        ang = pos * inv[r % quarter]
        first = (r % (rope_dim // 2)) < quarter
        cos[:, lane] = np.cos(ang)
        sin[:, lane] = -np.sin(ang) if first else np.sin(ang)
    return cos.astype(np.float32), sin.astype(np.float32)


@functools.lru_cache(maxsize=None)
def _chan_dft_np():
    k = np.arange(FNET_GC)
    ang = 2.0 * np.pi * np.outer(k, k) / FNET_GC
    c = np.kron(np.eye(FNET_GROUPS), np.cos(ang))
    s = np.kron(np.eye(FNET_GROUPS), np.sin(ang))
    return np.concatenate([c, s], axis=1)


@functools.lru_cache(maxsize=None)
def _pos_dft_np(L):
    k = np.arange(L, dtype=np.int64)
    ang = 2.0 * np.pi * (np.outer(k, k) % L) / L
    return np.concatenate([np.cos(ang), -np.sin(ang)], axis=1)


@functools.lru_cache(maxsize=None)
def _hyena_dft_np(L, hb):
    k = np.arange(L, dtype=np.int64)
    ang = np.pi * (np.outer(k, k) % (2 * L)) / L
    cf = np.cos(ang)
    sf = np.sin(ang)
    alt = (-1.0) ** np.arange(L)
    sf[0, :] = alt
    w = np.where(k == 0, 1.0, 2.0)[None, :] / (2.0 * L)
    ainv = cf.T * w
    binv = sf.T / L
    binv[:, 0] = alt / (2.0 * L)
    fw, iv = [], []
    for t in range(L // hb):
        sl = slice(t * hb, (t + 1) * hb)
        fw += [cf[sl], sf[sl]]
        iv += [ainv[:, sl], binv[:, sl]]
    return np.concatenate(fw, axis=0), np.concatenate(iv, axis=1)


@functools.lru_cache(maxsize=None)
def _hyena_feat_np(L):
    pos = np.arange(L, dtype=np.float64)
    t = pos / L
    bands = np.arange(1, HY_BANDS + 1, dtype=np.float64)
    ang = (2.0 * math.pi / L) * pos[:, None] * bands[None, :]
    z = np.zeros((L, HY_EMB_PAD), np.float64)
    z[:, 0] = t
    z[:, 1:1 + HY_BANDS] = np.sin(ang)
    z[:, 1 + HY_BANDS:HY_EMB] = np.cos(ang)
    deltas = np.abs(np.linspace(math.log(HY_TARGET) / HY_SLOW_DECAY,
                                math.log(HY_TARGET) / HY_FAST_DECAY, HY_CH))
    window = np.exp(-t[:, None] * deltas[None, :])
    return z.astype(np.float32), window.astype(np.float32)


@functools.lru_cache(maxsize=None)
def _hy_fwd_tabs_np(L, S):
    R, Q, D = 2 * L // S, L // S, S // 2
    c, a = np.arange(R)[:, None], np.arange(Q)[None, :]
    ang = 2 * np.pi * ((c * a) % R) / R
    f1 = np.concatenate([np.cos(ang), -np.sin(ang)], axis=0)
    cc, d, b = np.arange(R)[:, None, None], np.arange(D)[None, :, None], np.arange(S)[None, None, :]
    th = np.pi * (((cc + R * d) * b) % (2 * L)) / L
    ur = np.concatenate([np.cos(th), np.sin(th)], axis=2)
    us = np.concatenate([np.sin(th), -np.cos(th)], axis=2)
    us[0, 0, :S] = (-1.0) ** np.arange(S)
    us[0, 0, S:] = 0.0
    return f1, np.concatenate([ur, us], axis=1)


@functools.lru_cache(maxsize=None)
def _hy_inv_tabs_np(L, S):
    R, Q, D = 2 * L // S, L // S, S // 2
    cc, f, d = np.arange(R)[:, None, None], np.arange(S)[None, :, None], np.arange(D)[None, None, :]
    k = cc + R * d
    be = np.pi * ((k * f) % (2 * L)) / L
    w = np.where(k == 0, 1.0, 2.0) / (2.0 * L) + 0.0 * be
    bc = np.concatenate([w * np.cos(be), w * np.sin(be)], axis=2)
    bs = np.concatenate([-w * np.sin(be), w * np.cos(be)], axis=2)
    bc[0, :, D] = (-1.0) ** np.arange(S) / (2.0 * L)
    bs[0, :, D] = 0.0
    e, c = np.arange(Q)[:, None], np.arange(R)[None, :]
    al = 2 * np.pi * ((e * c) % R) / R
    return np.concatenate([bc, bs], axis=1), np.concatenate([np.cos(al), np.sin(al)], axis=1)


@functools.lru_cache(maxsize=None)
def _four_tabs_np(L, S):
    R = L // S
    c, a = np.arange(R)[:, None], np.arange(R)[None, :]
    ang = 2 * np.pi * ((c * a) % R) / R
    f1 = np.concatenate([np.concatenate([np.cos(ang), -np.sin(ang)], axis=1),
                         np.concatenate([-np.sin(ang), -np.cos(ang)], axis=1)], axis=0)
    cc, d, b = np.arange(R)[:, None, None], np.arange(S)[None, :, None], np.arange(S)[None, None, :]
    th = 2 * np.pi * (((cc + R * d) * b) % L) / L
    return f1, np.concatenate([np.cos(th), np.sin(th)], axis=2)


def _rms(x, g):
    return x * lax.rsqrt(jnp.mean(x * x, axis=-1, keepdims=True) + EPS) * g


def _silu(x):
    return x * jax.nn.sigmoid(x)


def _bdot(a, b):
    return jnp.dot(a.astype(BF16), b.astype(BF16), preferred_element_type=F32)


def _split(a):
    hi = a.astype(BF16)
    lo = (a - hi.astype(F32)).astype(BF16)
    return hi, lo


def _dot3(a, b):
    ah, al = _split(a)
    bh, bl = _split(b)
    d = lambda x, y: jnp.dot(x, y, preferred_element_type=F32)
    return d(ah, bh) + (d(al, bh) + d(ah, bl))


def _store_kv(kv, kr_pad, k_ref, v_ref):
    wk = MLA_HEADS * MLA_HP
    k_ref[...] = (kv[:, :wk] + jnp.concatenate([kr_pad] * MLA_HEADS, axis=1)).astype(k_ref.dtype)
    lane = lax.broadcasted_iota(jnp.int32, (1, wk), 1)
    ones = jnp.where(lane % MLA_HP >= MLA_V, 1.0, 0.0)
    v_ref[...] = (kv[:, wk:] + ones).astype(v_ref.dtype)


def _rope(x, cos, sin_signed, quarter):
    w = x.shape[-1]
    lane = lax.broadcasted_iota(jnp.int32, x.shape, 1)
    first = (lane % (2 * quarter)) < quarter
    up = pltpu.roll(x, w - quarter, axis=1)
    dn = pltpu.roll(x, quarter, axis=1)
    return x * cos + jnp.where(first, up, dn) * sin_signed


def _ada_kernel(c_ref, w_ref, b_ref, o_ref):
    s = _silu(c_ref[...])
    o_ref[...] = _bdot(s, w_ref[...]) + b_ref[...]


def _ada(cvec, w_ada, b_ada):
    tn = 1024
    n6 = w_ada.shape[-1]
    return pl.pallas_call(
        _ada_kernel,
        grid=(DEPTH, n6 // tn),
        in_specs=[
            pl.BlockSpec((MOD_ROWS, D_MODEL), lambda l, j: (0, 0)),
            pl.BlockSpec((None, D_MODEL, tn), lambda l, j: (l, 0, j)),
            pl.BlockSpec((None, 1, tn), lambda l, j: (l, 0, j)),
        ],
        out_specs=pl.BlockSpec((None, MOD_ROWS, tn), lambda l, j: (l, 0, j)),
        out_shape=jax.ShapeDtypeStruct((DEPTH, MOD_ROWS, n6), F32),
        compiler_params=_cparams(("parallel", "parallel")),
        name="ada",
    )(cvec, w_ada, b_ada.reshape(DEPTH, 1, n6))


def _in_proj_kernel(*refs, latent):
    (x_ref, mod_ref, g0_ref, w_ref, qn_ref, kvn_ref, wuq_ref, wukv_ref, bd_ref) = refs[:9]
    pos = 9
    if latent:
        cr_ref, sr_ref, cm_ref, sm_ref = refs[pos:pos + 4]
        pos += 4
        qkvg_ref, fx_ref, q_ref, k_ref, v_ref, hu_ref = refs[pos:pos + 6]
        lanes = lambda ref, reps: jnp.concatenate([ref[...]] * reps, axis=1)
    else:
        qkvg_ref, fx_ref, q_ref, k_ref, v_ref, hu_ref, ckv_ref, kr_ref = refs[pos:pos + 8]

    x = x_ref[...]
    mod = mod_ref[...]
    sh1 = mod[:, 0:D_MODEL]
    sc1 = mod[:, D_MODEL:2 * D_MODEL]
    h = _rms(x, g0_ref[...]) * (1.0 + sc1) + sh1
    proj = jnp.dot(h.astype(BF16), w_ref[...], preferred_element_type=F32)

    rq = proj[:, OFF_RET:OFF_RET + RET_W]
    rk = proj[:, OFF_RET + RET_W:OFF_RET + 2 * RET_W] * (RET_DK ** -0.5)
    if latent:
        cos_r, sin_r = lanes(cr_ref, RET_W // 128), lanes(sr_ref, RET_W // 128)
        rq = _rope(rq, cos_r, sin_r, RET_DK // 4)
        rk = _rope(rk, cos_r, sin_r, RET_DK // 4)
    qkvg_ref[:, 0:RET_W] = rq
    qkvg_ref[:, RET_W:2 * RET_W] = rk
    qkvg_ref[:, 2 * RET_W:4 * RET_W] = proj[:, OFF_RET + 2 * RET_W:OFF_RET + 4 * RET_W]

    fcs = _bdot(proj[:, OFF_FU:OFF_FU + FNET_W], bd_ref[...])
    fx_ref[0] = fcs[:, :FNET_W].astype(fx_ref.dtype)
    fx_ref[1] = fcs[:, FNET_W:].astype(fx_ref.dtype)

    cqn = _rms(proj[:, OFF_CQ:OFF_CQ + MLA_Q_LORA], qn_ref[...])
    q = _bdot(cqn, wuq_ref[...])
    ckvn = _rms(proj[:, OFF_CKV:OFF_CKV + MLA_KV_LORA], kvn_ref[...])
    kv = _bdot(ckvn, wukv_ref[...])
    krp = proj[:, OFF_KR:OFF_KR + MLA_HP]
    if latent:
        q = _rope(q, lanes(cm_ref, MLA_HEADS), lanes(sm_ref, MLA_HEADS), MLA_ROPE // 4)
        krp = _rope(krp, cm_ref[...], sm_ref[...], MLA_ROPE // 4)
    else:
        ckv_ref[...] = ckvn
        kr_ref[...] = krp
    q_ref[...] = (q * ((MLA_NOPE + MLA_ROPE) ** -0.5 * math.log2(math.e))).astype(q_ref.dtype)
    _store_kv(kv, krp, k_ref, v_ref)

    hu_ref[...] = proj[:, OFF_HU:OFF_HU + 3 * HY_CH]


def _in_proj(x, mod, l, p, latent):
    B, L, _ = x.shape
    t = _tiles(L)
    tm = t["tm"]
    nt = L // tm
    row = (lambda b: b) if latent else (lambda b: CTX_ROW)
    full = lambda shape: pl.BlockSpec(shape, lambda i, b: (0,) * len(shape), pipeline_mode=pl.Buffered(1))
    in_specs = [
        pl.BlockSpec((None, tm, D_MODEL), lambda i, b: (b, i, 0)),
        pl.BlockSpec((None, None, 1, 6 * D_MODEL), lambda i, b: (l, row(b), 0, 0)),
        full((1, D_MODEL)),
        full((D_MODEL, IN_WP)),
        full((1, MLA_Q_LORA)),
        full((1, MLA_KV_LORA)),
        full((MLA_Q_LORA, MLA_HEADS * MLA_HP)),
        full((MLA_KV_LORA, 2 * MLA_HEADS * MLA_HP)),
        full((FNET_W, 2 * FNET_W)),
    ]
    args = [x, mod, p["g0"], p["w_in"], p["q_norm"], p["kv_norm"], p["w_uq"], p["w_ukv"],
            _bf16_const(_chan_dft_np())]
    if latent:
        tabs = (_rope_np(L, 128, 0, RET_DK, RET_DK)
                + _rope_np(L, MLA_HP, MLA_ROPE_OFF, MLA_HP, MLA_ROPE))
        for tab in tabs:
            in_specs.append(pl.BlockSpec((tm, tab.shape[1]), lambda i, b: (i, 0)))
            args.append(jnp.asarray(tab))
    bm = lambda w: pl.BlockSpec((None, tm, w), lambda i, b: (b, i, 0))
    sds = jax.ShapeDtypeStruct
    out_specs = [
        bm(4 * RET_W),
        pl.BlockSpec((2, tm, FNET_W), lambda i, b: (0, i, b)),
        bm(MLA_HEADS * MLA_HP), bm(MLA_HEADS * MLA_HP), bm(MLA_HEADS * MLA_HP),
        pl.BlockSpec((tm, 3 * HY_CH), lambda i, b: (i, b)),
    ]
    out_shape = [
        sds((B, L, 4 * RET_W), F32),
        sds((2, L, B * FNET_W), F32 if _use_fft(L) else BF16),
        sds((B, L, MLA_HEADS * MLA_HP), BF16),
        sds((B, L, MLA_HEADS * MLA_HP), BF16),
        sds((B, L, MLA_HEADS * MLA_HP), BF16),
        sds((L, B * 3 * HY_CH), F32),
    ]
    if not latent:
        out_specs += [bm(MLA_KV_LORA), bm(MLA_HP)]
        out_shape += [sds((B, L, MLA_KV_LORA), F32), sds((B, L, MLA_HP), F32)]
    outs = pl.pallas_call(
        functools.partial(_in_proj_kernel, latent=latent),
        grid=(nt, B), in_specs=in_specs, out_specs=out_specs, out_shape=out_shape,
        compiler_params=_cparams(("parallel", "parallel")),
        name="in_proj_lat" if latent else "in_proj_ctx",
    )(*args)
    return tuple(outs) + ((None, None) if latent else ())


def _ctx_kv_kernel(ckv_ref, kr_ref, w_ref, k_ref, v_ref):
    _store_kv(_bdot(ckv_ref[...], w_ref[...]), kr_ref[...], k_ref, v_ref)


def _ctx_kv(cache_ckv, krope_pad, l, w_ukv):
    B, _, P, _ = cache_ckv.shape
    return pl.pallas_call(
        _ctx_kv_kernel,
        grid=(B,),
        in_specs=[
            pl.BlockSpec((None, None, P, MLA_KV_LORA), lambda b: (b, l, 0, 0)),
            pl.BlockSpec((None, None, P, MLA_HP), lambda b: (b, l, 0, 0)),
            pl.BlockSpec(w_ukv.shape, lambda b: (0, 0)),
        ],
        out_specs=[
            pl.BlockSpec((None, P, MLA_HEADS * MLA_HP), lambda b: (b, 0, 0)),
            pl.BlockSpec((None, P, MLA_HEADS * MLA_HP), lambda b: (b, 0, 0)),
        ],
        out_shape=[
            jax.ShapeDtypeStruct((B, P, MLA_HEADS * MLA_HP), BF16),
            jax.ShapeDtypeStruct((B, P, MLA_HEADS * MLA_HP), BF16),
        ],
        compiler_params=_cparams(("parallel",)),
        name="ctx_kv",
    )(cache_ckv, krope_pad, w_ukv)


def _ret_kernel(*refs, L, C, has_s0, emit_state):
    q_ref, k_ref, v_ref, g_ref, dec_ref = refs[:5]
    pos = 5
    if has_s0:
        s0_ref = refs[pos]
        pos += 1
    o_ref = refs[pos]
    pos += 1
    if emit_state:
        sn_ref = refs[pos]
        pos += 1
    sf_ref, sb_ref, dm_ref, tab_ref = refs[pos:pos + 4]
    n = L // C
    H, W = RET_HEADS, RET_W

    d = dec_ref[0:2, :]
    lg = jnp.minimum(d, 0.0) - jnp.log1p(jnp.exp(-jnp.abs(d)))
    lf, lb = lg[0:1, :], lg[1:2, :]
    ii = lax.broadcasted_iota(jnp.int32, (C, C), 0)
    jj = lax.broadcasted_iota(jnp.int32, (C, C), 1)
    diff = (ii - jj).astype(F32)
    for h in range(H):
        lfh = lf[:, h * RET_DV:h * RET_DV + 1]
        lbh = lb[:, h * RET_DV:h * RET_DV + 1]
        dm_ref[h] = jnp.where(diff > 0, jnp.exp(diff * lfh),
                              jnp.where(diff < 0, jnp.exp(-diff * lbh), 2.0))
    ci = lax.broadcasted_iota(jnp.int32, (C, W), 0).astype(F32)
    tab_ref[0] = jnp.exp((ci + 1.0) * lf)
    tab_ref[1] = jnp.exp((C - ci) * lb)
    tab_ref[2] = jnp.exp((C - 1.0 - ci) * lf)
    tab_ref[3] = jnp.exp(ci * lb)
    g_f = jnp.exp(C * lf)
    g_b = jnp.exp(C * lb)
    hr = lax.broadcasted_iota(jnp.int32, (W, W), 0) // RET_DK
    hc = lax.broadcasted_iota(jnp.int32, (W, W), 1) // RET_DV
    same_head = hr == hc
    avg = jnp.where(same_head, 1.0 / RET_DV, 0.0).astype(BF16)

    sf_ref[0] = jnp.zeros((W, W), F32)
    sb_ref[n] = jnp.zeros((W, W), F32)
    if has_s0:
        for h in range(H):
            blk = slice(h * RET_DK, (h + 1) * RET_DK)
            sf_ref[0, blk, blk] = s0_ref[0, h]
            sb_ref[n, blk, blk] = s0_ref[1, h]

    tdot = lambda a, b: lax.dot_general(a.astype(BF16), b.astype(BF16), (((0,), (0,)), ((), ())),
                                        preferred_element_type=F32)

    def states(j, carry):
        jb = n - 1 - j
        rf = pl.multiple_of(j * C, C)
        rb = pl.multiple_of(jb * C, C)
        upd = tdot(k_ref[pl.ds(rf, C), :] * tab_ref[2], v_ref[pl.ds(rf, C), :])
        sf_ref[j + 1] = g_f * sf_ref[j] + jnp.where(same_head, upd, 0.0)
        upd = tdot(k_ref[pl.ds(rb, C), :] * tab_ref[3], v_ref[pl.ds(rb, C), :])
        sb_ref[jb] = g_b * sb_ref[jb + 1] + jnp.where(same_head, upd, 0.0)
        return carry

    lax.fori_loop(0, n, states, 0)

    lane_head = lax.broadcasted_iota(jnp.int32, (C, W), 1) // RET_DV

    def group_mean(x):
        hi, lo = _split(x)
        return (jnp.dot(hi, avg, preferred_element_type=F32)
                + jnp.dot(lo, avg, preferred_element_type=F32))

    def chunk(j, carry):
        r0 = pl.multiple_of(j * C, C)
        q = q_ref[pl.ds(r0, C), :]
        qb = q.astype(BF16)
        kb = k_ref[pl.ds(r0, C), :].astype(BF16)
        vb = v_ref[pl.ds(r0, C), :].astype(BF16)
        o = tab_ref[0] * jnp.dot(qb, sf_ref[j].astype(BF16), preferred_element_type=F32)
        o = o + tab_ref[1] * jnp.dot(qb, sb_ref[j + 1].astype(BF16), preferred_element_type=F32)
        inner = jnp.zeros((C, W), F32)
        for h in range(H):
            in_head = lane_head == h
            qh = jnp.where(in_head, q, 0.0).astype(BF16)
            s = lax.dot_general(qh, kb, (((1,), (1,)), ((), ())), preferred_element_type=F32)
            oh = jnp.dot((s * dm_ref[h]).astype(BF16), vb, preferred_element_type=F32)
            inner = jnp.where(in_head, oh, inner)
        o = o + inner
        dlt = o - group_mean(o)
        var = group_mean(dlt * dlt)
        gate = g_ref[pl.ds(r0, C), :]
        o_ref[pl.ds(r0, C), :] = (_silu(gate) * (dlt * lax.rsqrt(var + EPS))).astype(o_ref.dtype)
        return carry

    lax.fori_loop(0, n, chunk, 0)

    if emit_state:
        for h in range(H):
            blk = slice(h * RET_DK, (h + 1) * RET_DK)
            sn_ref[0, h] = sf_ref[n, blk, blk]
            sn_ref[1, h] = sb_ref[0, blk, blk]


def _retention(qkvg, dec_rows, s0, l, emit_state):
    B, L, _ = qkvg.shape
    C = _tiles(L)["chunk"]
    n = L // C
    has_s0 = s0 is not None
    def colspec(j):
        return pl.BlockSpec((None, L, RET_W), lambda b: (b, 0, j))

    in_specs = [colspec(j) for j in range(4)]
    in_specs.append(pl.BlockSpec((None, 8, RET_W), lambda b: (l, 0, 0)))
    args = [qkvg, qkvg, qkvg, qkvg, dec_rows]
    st_block = (None, None, 2, RET_HEADS, RET_DK, RET_DV)
    if has_s0:
        in_specs.append(pl.BlockSpec(st_block, lambda b: (b, l, 0, 0, 0, 0)))
        args.append(s0)
    out_specs = [pl.BlockSpec((None, L, RET_W), lambda b: (b, 0, 0))]
    out_shape = [jax.ShapeDtypeStruct((B, L, RET_W), BF16)]
    if emit_state:
        out_specs.append(pl.BlockSpec((None, 2, RET_HEADS, RET_DK, RET_DV), lambda b: (b, 0, 0, 0, 0)))
        out_shape.append(jax.ShapeDtypeStruct((B, 2, RET_HEADS, RET_DK, RET_DV), F32))
    outs = pl.pallas_call(
        functools.partial(_ret_kernel, L=L, C=C, has_s0=has_s0, emit_state=emit_state),
        grid=(B,), in_specs=in_specs, out_specs=out_specs, out_shape=out_shape,
        scratch_shapes=[
            pltpu.VMEM((n + 1, RET_W, RET_W), F32),
            pltpu.VMEM((n + 1, RET_W, RET_W), F32),
            pltpu.VMEM((RET_HEADS, C, C), F32),
            pltpu.VMEM((4, C, RET_W), F32),
        ],
        compiler_params=_cparams(("parallel",)),
        name="retention_lat" if has_s0 else "retention_ctx",
    )(*args)
    return outs if emit_state else (outs[0], None)


def _mm_kernel(a_ref, b_ref, o_ref, *, scale):
    o_ref[...] = (jnp.dot(a_ref[...], b_ref[...], preferred_element_type=F32) * scale).astype(o_ref.dtype)


def _mm(a, b, scale, out_dtype, tm, tn, name):
    M, K = a.shape
    _, N = b.shape
    tm, tn = min(tm, M), min(tn, N)
    return pl.pallas_call(
        functools.partial(_mm_kernel, scale=scale),
        grid=(N // tn, M // tm),
        in_specs=[pl.BlockSpec((tm, K), lambda j, i: (i, 0)),
                  pl.BlockSpec((K, tn), lambda j, i: (0, j))],
        out_specs=pl.BlockSpec((tm, tn), lambda j, i: (i, j)),
        out_shape=jax.ShapeDtypeStruct((M, N), out_dtype),
        compiler_params=_cparams(("parallel", "parallel")),
        name=name,
    )(a, b)


def _attn_kernel(*refs, has_ctx):
    if has_ctx:
        q_ref, k_ref, v_ref, kc_ref, vc_ref, o_ref, s_ref = refs
        srcs = ((k_ref, v_ref), (kc_ref, vc_ref))
    else:
        q_ref, k_ref, v_ref, o_ref, s_ref = refs
        srcs = ((k_ref, v_ref),)
    nt = lambda a, b: lax.dot_general(a, b, (((1,), (1,)), ((), ())), preferred_element_type=F32)
    chunks = [(kr, vr, c0) for kr, vr in srcs for c0 in range(0, kr.shape[0], ATTN_CK)]
    def scores(h):
        hq = slice(h * MLA_HP, (h + 1) * MLA_HP)
        qh = q_ref[:, hq]
        mx = None
        for n, (kr, _, c0) in enumerate(chunks):
            s = nt(qh, kr[c0:c0 + ATTN_CK, hq])
            s_ref[h % 2, :, n * ATTN_CK:(n + 1) * ATTN_CK] = s
            for g in range(ATTN_CK // 128):
                part = s[:, g * 128:(g + 1) * 128]
                mx = part if mx is None else jnp.maximum(mx, part)
        return jnp.max(mx, axis=-1, keepdims=True)

    def weighted_values(h, m):
        hq = slice(h * MLA_HP, (h + 1) * MLA_HP)
        acc = None
        for n, (_, vr, c0) in enumerate(chunks):
            p = jnp.exp2(s_ref[h % 2, :, n * ATTN_CK:(n + 1) * ATTN_CK] - m).astype(BF16)
            d = jnp.dot(p, vr[c0:c0 + ATTN_CK, hq], preferred_element_type=F32)
            acc = d if acc is None else acc + d
        o_ref[:, h * MLA_V:(h + 1) * MLA_V] = (acc[:, :MLA_V] / acc[:, MLA_V:2 * MLA_V]).astype(o_ref.dtype)

    m = scores(0)
    for h in range(MLA_HEADS):
        m_next = scores(h + 1) if h + 1 < MLA_HEADS else None
        weighted_values(h, m)
        m = m_next


def _attention(q, k, v, kc, vc):
    B, L, _ = q.shape
    tq = _tiles(L)["tq"]
    has_ctx = kc is not None
    whole = lambda a: pl.BlockSpec((None,) + a.shape[1:], lambda b, i: (b, 0, 0))
    in_specs = [pl.BlockSpec((None, tq, q.shape[2]), lambda b, i: (b, i, 0)), whole(k), whole(v)]
    args = [q, k, v]
    n_keys = L
    if has_ctx:
        in_specs += [whole(kc), whole(vc)]
        args += [kc, vc]
        n_keys += kc.shape[1]
    return pl.pallas_call(
        functools.partial(_attn_kernel, has_ctx=has_ctx),
        grid=(B, L // tq), in_specs=in_specs,
        out_specs=pl.BlockSpec((None, tq, MLA_W), lambda b, i: (b, i, 0)),
        out_shape=jax.ShapeDtypeStruct((B, L, MLA_W), BF16),
        scratch_shapes=[pltpu.VMEM((2, tq, n_keys), F32)],
        compiler_params=_cparams(("parallel", "parallel")),
        name="attention_lat" if has_ctx else "attention_ctx",
    )(*args)


def _hy_filter_kernel(z_ref, w1_ref, b1_ref, w2_ref, b2_ref, w3_ref, win_ref, x_ref):
    h1 = jnp.sin(_dot3(z_ref[...], w1_ref[...]) + b1_ref[...])
    h2 = jnp.sin(_dot3(h1, w2_ref[...]) + b2_ref[...])
    h = _dot3(h2, w3_ref[...])
    win = win_ref[...]
    hf = h[:, :HY_CH] * win
    hb = h[:, HY_CH:] * win
    row = lax.broadcasted_iota(jnp.int32, hb.shape, 0)
    hb = jnp.where(row == 0, 0.0, hb)
    nrm = jnp.sum(jnp.abs(hf), axis=0, keepdims=True) + jnp.sum(jnp.abs(hb), axis=0, keepdims=True) + EPS
    inv = 1.0 / nrm
    x_ref[:, :HY_CH] = (hf + hb) * inv
    x_ref[:, HY_CH:] = (hf - hb) * inv


def _hy_filter(L, p):
    z, win = _hyena_feat_np(L)
    full = lambda shape: pl.BlockSpec(shape, lambda o: (0,) * len(shape))
    return pl.pallas_call(
        _hy_filter_kernel,
        grid=(HY_ORDER,),
        in_specs=[full((L, HY_EMB_PAD)), full((HY_EMB_PAD, HY_FFN)), full((1, HY_FFN)),
                  full((HY_FFN, HY_FFN)), full((1, HY_FFN)),
                  pl.BlockSpec((HY_FFN, 2 * HY_CH), lambda o: (0, o)),
                  full((L, HY_CH))],
        out_specs=pl.BlockSpec((L, 2 * HY_CH), lambda o: (0, o)),
        out_shape=jax.ShapeDtypeStruct((L, HY_ORDER * 2 * HY_CH), F32),
        compiler_params=_cparams(("parallel",)),
        name="hy_filter",
    )(jnp.asarray(z), p["hy_w1"], p["hy_b1"], p["hy_w2"], p["hy_b2"], p["hy_w3"], jnp.asarray(win))


def _hy_gspec_kernel(f_ref, x_ref, g_ref, *, hb):
    xh, xl = _split(x_ref[...])
    f = f_ref[...]
    r = jnp.dot(f, xh, preferred_element_type=F32) + jnp.dot(f, xl, preferred_element_type=F32)
    ga = r[:hb, :HY_CH]
    gb = r[hb:, HY_CH:]
    nyq = r[hb:hb + 1, :HY_CH]
    k = pl.program_id(1) * hb + lax.broadcasted_iota(jnp.int32, (hb, HY_CH), 0)
    g_ref[0] = ga
    g_ref[1] = jnp.where(k == 0, 0.0, gb)
    g_ref[2] = jnp.where(k == 0, jnp.broadcast_to(nyq, ga.shape), ga)


def _hy_gspec(fwd, x, L, hb):
    return pl.pallas_call(
        functools.partial(_hy_gspec_kernel, hb=hb),
        grid=(HY_ORDER, L // hb),
        in_specs=[pl.BlockSpec((2 * hb, L), lambda o, t: (t, 0)),
                  pl.BlockSpec((L, 2 * HY_CH), lambda o, t: (0, o))],
        out_specs=pl.BlockSpec((3, hb, HY_CH), lambda o, t: (0, t, o)),
        out_shape=jax.ShapeDtypeStruct((3, L, HY_ORDER * HY_CH), F32),
        compiler_params=_cparams(("parallel", "parallel")),
        name="hy_gspec",
    )(fwd, x)


def _hy_conv_kernel(u_ref, w_ref, b_ref, v_ref, vb_ref, x1_ref, x2_ref):
    u = u_ref[...]
    L = u.shape[0]
    row = lax.broadcasted_iota(jnp.int32, u.shape, 0)
    prev = jnp.where(row == 0, 0.0, pltpu.roll(u, 1, axis=0))
    nxt = jnp.where(row == L - 1, 0.0, pltpu.roll(u, L - 1, axis=0))
    w = w_ref[...]
    uc = prev * w[0:1] + u * w[1:2] + nxt * w[2:3] + b_ref[...]
    v_ref[...] = uc[:, :HY_CH]
    vb_ref[...] = uc[:, :HY_CH].astype(vb_ref.dtype)
    x1_ref[...] = uc[:, HY_CH:2 * HY_CH]
    x2_ref[...] = uc[:, 2 * HY_CH:]


def _hy_conv(hu, B, w, b):
    L = hu.shape[0]
    col = pl.BlockSpec((L, HY_CH), lambda bb: (0, bb))
    sds = lambda dt: jax.ShapeDtypeStruct((L, B * HY_CH), dt)
    return pl.pallas_call(
        _hy_conv_kernel,
        grid=(B,),
        in_specs=[pl.BlockSpec((L, 3 * HY_CH), lambda bb: (0, bb)),
                  pl.BlockSpec((3, 3 * HY_CH), lambda bb: (0, 0)),
                  pl.BlockSpec((1, 3 * HY_CH), lambda bb: (0, 0))],
        out_specs=[col, col, col, col],
        out_shape=[sds(F32), sds(BF16), sds(F32), sds(F32)],
        compiler_params=_cparams(("parallel",)),
        name="hy_conv",
    )(hu, w, b)


def _hy_fwd_kernel(f_ref, u_ref, g_ref, y_ref, *, hb):
    acc = jnp.dot(f_ref[...], u_ref[...], preferred_element_type=F32)
    ga, gb, gd = g_ref[0], g_ref[1], g_ref[2]
    for c in range(u_ref.shape[1] // HY_CH):
        sl = slice(c * HY_CH, (c + 1) * HY_CH)
        ur = acc[:hb, sl]
        us = acc[hb:, sl]
        y_ref[:hb, sl] = (ur * ga - us * gb).astype(y_ref.dtype)
        y_ref[hb:, sl] = (ur * gb + us * gd).astype(y_ref.dtype)


def _hy_fwd(fwd, u, g, o, hb, tn):
    M, L = fwd.shape
    N = u.shape[1]
    tn = min(tn, N)
    return pl.pallas_call(
        functools.partial(_hy_fwd_kernel, hb=hb),
        grid=(N // tn, L // hb),
        in_specs=[pl.BlockSpec((2 * hb, L), lambda j, t: (t, 0)),
                  pl.BlockSpec((L, tn), lambda j, t: (0, j)),
                  pl.BlockSpec((3, hb, HY_CH), lambda j, t: (0, t, o))],
        out_specs=pl.BlockSpec((2 * hb, tn), lambda j, t: (t, j)),
        out_shape=jax.ShapeDtypeStruct((M, N), BF16),
        compiler_params=_cparams(("parallel", "parallel")),
        name="hy_fwd",
    )(fwd, u, g)


def _hy_inv_kernel(f_ref, y_ref, gate_ref, z_ref, d_ref, *out_refs):
    y = jnp.dot(f_ref[...], y_ref[...], preferred_element_type=F32)
    d = d_ref[...]
    for c in range(y.shape[1] // HY_CH):
        sl = slice(c * HY_CH, (c + 1) * HY_CH)
        z = gate_ref[:, sl] * (y[:, sl] + d * z_ref[:, sl])
        for r in out_refs:
            r[:, sl] = z.astype(r.dtype)


def _hy_inv(inv, y, gate, zprev, hy_bias, o, tm, tn, out_dtypes):
    L, K = inv.shape
    N = y.shape[1]
    tm, tn = min(tm, L), min(tn, N)
    tile = pl.BlockSpec((tm, tn), lambda j, i: (i, j))
    return pl.pallas_call(
        _hy_inv_kernel,
        grid=(N // tn, L // tm),
        in_specs=[pl.BlockSpec((tm, K), lambda j, i: (i, 0)),
                  pl.BlockSpec((K, tn), lambda j, i: (0, j)),
                  tile, tile,
                  pl.BlockSpec((1, HY_CH), lambda j, i: (0, 0))],
        out_specs=[tile] * len(out_dtypes),
        out_shape=[jax.ShapeDtypeStruct((L, N), dt) for dt in out_dtypes],
        compiler_params=_cparams(("parallel", "parallel")),
        name="hy_inv",
    )(inv, y, gate, zprev, hy_bias[o:o + 1])


def _tile_rows(ref, lead, rows):
    return jnp.concatenate([ref[lead, r * SUBL:(r + 1) * SUBL, :] for r in rows], axis=0)


def _fft_fwd_kernel(*refs, Q, R, S, filt):
    if filt:
        u_ref, k1_ref, e_ref, y_ref, a_scr = refs
    else:
        u_ref, k1_ref, e_ref, ga_ref, gb_ref, ny_ref, y_ref, a_scr = refs
    D = S // 2
    nb = S // SUBL
    tn = u_ref.shape[-1]
    dot = lambda p, q: jnp.dot(p, q, preferred_element_type=F32)
    k1 = k1_ref[...]
    for bh in range(nb):
        x = u_ref[:, bh].reshape(Q * SUBL, tn)
        if filt:
            xh, xl = _split(x)
            a_scr[bh] = dot(k1, xh) + dot(k1, xl)
        else:
            a_scr[bh] = dot(k1, x.astype(BF16))
    for c in range(R):
        rhs = jnp.concatenate([_tile_rows(a_scr, bh, (part * R + c,)) for part in (0, 1) for bh in range(nb)],
                              axis=0)
        if filt:
            rh, rl = _split(rhs)
            y_ref[c] = dot(e_ref[c], rh) + dot(e_ref[c], rl)
            continue
        y = dot(e_ref[c], rhs.astype(BF16))
        ur, us = y[:D], y[D:]
        ga, gb = ga_ref[c], gb_ref[c]
        gd = ga
        if c == 0:
            row0 = lax.broadcasted_iota(jnp.int32, ga.shape, 0) == 0
            gb = jnp.where(row0, 0.0, gb)
            gd = jnp.where(row0, jnp.broadcast_to(ny_ref[0, 0:1, :], ga.shape), ga)
        y_ref[c, :D, :] = (ur * ga - us * gb).astype(y_ref.dtype)
        y_ref[c, D:, :] = (ur * gb + us * gd).astype(y_ref.dtype)


def _fft_fwd(u, L, spec, o):
    S = FFT_S
    R, Q, D = 2 * L // S, L // S, S // 2
    N = u.shape[1]
    tn = FFT_TN
    nb = S // SUBL
    filt = spec is None
    f1_np, e_np = _hy_fwd_tabs_np(L, S)
    const = lambda shape: pl.BlockSpec(shape, lambda j: (0,) * len(shape), pipeline_mode=pl.Buffered(1))
    in_specs = [pl.BlockSpec((Q, nb, SUBL, tn), lambda j: (0, 0, 0, j)),
                const((2 * R * SUBL, Q * SUBL)), const((R, S, 2 * S))]
    args = [u.reshape(Q, nb, SUBL, N), _bf16_const(np.kron(f1_np, np.eye(SUBL))), _bf16_const(e_np)]
    if not filt:
        in_specs += [pl.BlockSpec((R, D, tn), lambda j: (0, 0, 2 * o), pipeline_mode=pl.Buffered(1)),
                     pl.BlockSpec((R, D, tn), lambda j: (0, 1, 2 * o + 1), pipeline_mode=pl.Buffered(1)),
                     pl.BlockSpec((1, 8, tn), lambda j: (0, D // 8, 2 * o), pipeline_mode=pl.Buffered(1))]
        args += [spec, spec, spec]
    return pl.pallas_call(
        functools.partial(_fft_fwd_kernel, Q=Q, R=R, S=S, filt=filt),
        grid=(N // tn,), in_specs=in_specs,
        out_specs=pl.BlockSpec((R, S, tn), lambda j: (0, 0, j)),
        out_shape=jax.ShapeDtypeStruct((R, S, N), F32 if filt else BF16),
        scratch_shapes=[pltpu.VMEM((nb, 2 * R * SUBL, tn), F32)],
        compiler_params=_cparams(("parallel",)),
        name="fft_filter" if filt else "fft_fwd",
    )(*args)


def _fft_inv_kernel(y_ref, e_ref, k2_ref, gate_ref, z_ref, d_ref, o_ref, b_scr, *, Q, R, S):
    dot = lambda p, q: jnp.dot(p, q, preferred_element_type=F32)
    tn = y_ref.shape[-1]
    for c in range(R):
        b_scr[c] = dot(e_ref[c], y_ref[c])
    k2 = k2_ref[...]
    d = d_ref[...]
    for fh in range(S // SUBL):
        rhs = jnp.concatenate([_tile_rows(b_scr, c, (part * (S // SUBL) + fh,))
                               for part in (0, 1) for c in range(R)], axis=0).astype(BF16)
        conv = dot(k2, rhs).reshape(Q, SUBL, tn)
        o_ref[:, fh] = (gate_ref[:, fh] * (conv + d * z_ref[:, fh])).astype(o_ref.dtype)


def _fft_inv(y, L, gate, zprev, bias, out_dtype):
    S = FFT_S
    R, Q = 2 * L // S, L // S
    N = y.shape[2]
    tn = FFT_TN
    nb = S // SUBL
    e_np, f2_np = _hy_inv_tabs_np(L, S)
    const = lambda shape: pl.BlockSpec(shape, lambda j: (0,) * len(shape), pipeline_mode=pl.Buffered(1))
    tile = pl.BlockSpec((Q, nb, SUBL, tn), lambda j: (0, 0, 0, j))
    view = lambda a: a.reshape(Q, nb, SUBL, N)
    out = pl.pallas_call(
        functools.partial(_fft_inv_kernel, Q=Q, R=R, S=S),
        grid=(N // tn,),
        in_specs=[pl.BlockSpec((R, S, tn), lambda j: (0, 0, j)), const((R, 2 * S, S)),
                  const((Q * SUBL, 2 * R * SUBL)), tile, tile, const((1, tn))],
        out_specs=tile,
        out_shape=jax.ShapeDtypeStruct((Q, nb, SUBL, N), out_dtype),
        scratch_shapes=[pltpu.VMEM((R, 2 * S, tn), F32)],
        compiler_params=_cparams(("parallel",)),
        name="fft_inv",
    )(y, _bf16_const(e_np), _bf16_const(np.kron(f2_np, np.eye(SUBL))), view(gate), view(zprev), bias)
    return out.reshape(L, N)


def _fft_four_kernel(x_ref, k1_ref, e2_ref, o_ref, a_scr, *, R, S, scale):
    dot = lambda p, q: jnp.dot(p, q, preferred_element_type=F32)
    tn = x_ref.shape[-1]
    nb = S // SUBL
    k1 = k1_ref[...]
    for bh in range(nb):
        x = jnp.concatenate([x_ref[part, :, bh].reshape(R * SUBL, tn) for part in (0, 1)], axis=0)
        a_scr[bh] = dot(k1, x.astype(BF16))
    for ch in range(R // SUBL):
        rhs = jnp.concatenate([_tile_rows(a_scr, bh, (part * R + ch * SUBL + cl,))
                               for cl in range(SUBL) for part in (0, 1) for bh in range(nb)],
                              axis=0).astype(BF16)
        y = dot(e2_ref[ch], rhs)
        o_ref[:, ch] = (y.reshape(S, SUBL, tn) * scale).astype(o_ref.dtype)


def _fft_fourier(fx, L):
    S = FFT_S
    R = L // S
    N = fx.shape[2]
    tn = FFT_TN
    nb = S // SUBL
    f1_np, e_np = _four_tabs_np(L, S)
    e2 = np.zeros((R // SUBL, S, SUBL, SUBL, 2 * S))
    for cl in range(SUBL):
        e2[:, :, cl, cl, :] = e_np[cl::SUBL]
    e2 = e2.reshape(R // SUBL, S * SUBL, SUBL * 2 * S)
    const = lambda shape: pl.BlockSpec(shape, lambda j: (0,) * len(shape), pipeline_mode=pl.Buffered(1))
    out = pl.pallas_call(
        functools.partial(_fft_four_kernel, R=R, S=S, scale=(FNET_GC * L) ** -0.5),
        grid=(N // tn,),
        in_specs=[pl.BlockSpec((2, R, nb, SUBL, tn), lambda j: (0, 0, 0, 0, j)),
                  const((2 * R * SUBL, 2 * R * SUBL)), const(e2.shape)],
        out_specs=pl.BlockSpec((S, R // SUBL, SUBL, tn), lambda j: (0, 0, 0, j)),
        out_shape=jax.ShapeDtypeStruct((S, R // SUBL, SUBL, N), BF16),
        scratch_shapes=[pltpu.VMEM((nb, 2 * R * SUBL, tn), F32)],
        compiler_params=_cparams(("parallel",)),
        name="fft_fourier",
    )(fx.reshape(2, R, nb, SUBL, N), _bf16_const(np.kron(f1_np, np.eye(SUBL))), _bf16_const(e2))
    return out.reshape(L, N)


def _use_fft(L):
    return L >= FFT_MIN_L


def _hyena(hu, B, p):
    L = hu.shape[0]
    taps = _hy_filter(L, p)
    v, vb, x1, x2 = _hy_conv(hu, B, p["hy_short_w"], p["hy_short_b"])
    if _use_fft(L):
        spec = _fft_fwd(taps, L, None, None)
        z1 = _fft_inv(_fft_fwd(v, L, spec, 0), L, x1, v, p["hy_bias"][0:1], F32)
        return _fft_inv(_fft_fwd(z1, L, spec, 1), L, x2, z1, p["hy_bias"][1:2], BF16)
    t = _tiles(L)
    hb, tn = t["hb"], t["tcol"]
    fw_np, iv_np = _hyena_dft_np(L, hb)
    fwd, inv = _bf16_const(fw_np), _bf16_const(iv_np)
    g = _hy_gspec(fwd, taps, L, hb)
    y0 = _hy_fwd(fwd, vb, g, 0, hb, tn)
    z1, z1b = _hy_inv(inv, y0, x1, v, p["hy_bias"], 0, hb, tn, (F32, BF16))
    y1 = _hy_fwd(fwd, z1b, g, 1, hb, tn)
    (z2,) = _hy_inv(inv, y1, x2, z1, p["hy_bias"], 1, hb, tn, (BF16,))
    return z2


def _mix_ffn_kernel(ret_ref, four_ref, att_ref, hy_ref, wo_ref, x_ref, mod_ref, g_ref,
                    wg_ref, wu_ref, wd_ref, o_ref, hm_ref):
    mod = mod_ref[...]
    g1 = mod[:, 2 * D_MODEL:3 * D_MODEL]
    sh2 = mod[:, 3 * D_MODEL:4 * D_MODEL]
    sc2 = mod[:, 4 * D_MODEL:5 * D_MODEL]
    g2 = mod[:, 5 * D_MODEL:6 * D_MODEL]
    g = g_ref[...]

    def mix(rows):
        d = lambda a, i: jnp.dot(a[rows, :], wo_ref[i * 256:(i + 1) * 256, :], preferred_element_type=F32)
        m = d(ret_ref, 0) + d(four_ref, 1) + d(att_ref, 2) + d(hy_ref, 3)
        x1 = x_ref[rows, :] + g1 * _rms(m, g[1:2])
        o_ref[rows, :] = x1
        return (_rms(x1, g[2:3]) * (1.0 + sc2) + sh2).astype(BF16)

    def hidden(rows, h2):
        for c in range(D_FF // FFN_CHUNK):
            cs = slice(c * FFN_CHUNK, (c + 1) * FFN_CHUNK)
            a = jnp.dot(h2, wg_ref[:, cs], preferred_element_type=F32)
            u = jnp.dot(h2, wu_ref[:, cs], preferred_element_type=F32)
            hm_ref[rows, cs] = (_silu(a) * u).astype(BF16)

    def down(rows):
        f = jnp.dot(hm_ref[rows, :], wd_ref[...], preferred_element_type=F32)
        o_ref[rows, :] = o_ref[rows, :] + g2 * _rms(f, g[3:4])

    half = o_ref.shape[0] // 2
    ra, rb = slice(0, half), slice(half, 2 * half)
    hidden(ra, mix(ra))
    h2b = mix(rb)
    down(ra)
    hidden(rb, h2b)
    down(rb)


def _mix_ffn(ret, four, att, hy, x, mod, l, p, latent):
    B, L, _ = x.shape
    tm = _tiles(L)["tm_ffn"]
    row = (lambda b: b) if latent else (lambda b: CTX_ROW)
    bm = lambda w: pl.BlockSpec((None, tm, w), lambda b, i: (b, i, 0))
    tmj = pl.BlockSpec((tm, 256), lambda b, i: (i, b))
    resident = lambda shape: pl.BlockSpec(shape, lambda b, i: (0, 0), pipeline_mode=pl.Buffered(1))
    return pl.pallas_call(
        _mix_ffn_kernel,
        grid=(B, L // tm),
        in_specs=[bm(RET_W), tmj, bm(MLA_W), tmj,
                  resident((4 * 256, D_MODEL)),
                  bm(D_MODEL),
                  pl.BlockSpec((None, None, 1, 6 * D_MODEL), lambda b, i: (l, row(b), 0, 0)),
                  pl.BlockSpec((None, 4, D_MODEL), lambda b, i: (l, 0, 0)),
                  resident((D_MODEL, D_FF)), resident((D_MODEL, D_FF)), resident((D_FF, D_MODEL))],
        out_specs=bm(D_MODEL),
        out_shape=jax.ShapeDtypeStruct((B, L, D_MODEL), F32),
        scratch_shapes=[pltpu.VMEM((tm, D_FF), BF16)],
        compiler_params=_cparams(("parallel", "parallel")),
        name="mix_ffn",
    )(ret, four, att, hy, p["w_out"], x, mod, p["norm_g"], p["w_gate"], p["w_up"], p["w_down"])


def _layer(x, mod, l, p, latent, ctx_kv, s0):
    B, L, _ = x.shape
    t = _tiles(L)
    qkvg, fx, q, k, v, hu, ckv, krp = _in_proj(x, mod, l, p, latent)
    ret, s_new = _retention(qkvg, p["dec_rows"], s0, l, emit_state=not latent)
    if _use_fft(L):
        four = _fft_fourier(fx, L)
    else:
        four = _mm(_bf16_const(_pos_dft_np(L)), fx.reshape(2 * L, B * FNET_W),
                   (FNET_GC * L) ** -0.5, BF16, t["hb"], t["tcol"], "fourier_pos")
    kc, vc = ctx_kv if latent else (None, None)
    att = _attention(q, k, v, kc, vc)
    hy = _hyena(hu, B, p)
    return _mix_ffn(ret, four, att, hy, x, mod, l, p, latent), ckv, krp, s_new


def _prep_layer(l, w):
    zeros = lambda r, c: jnp.zeros((r, c), F32)
    w_in = w["w_in"][l]
    n0 = OFF_KR
    w_in_p = jnp.concatenate(
        [w_in[:, :n0], zeros(D_MODEL, MLA_ROPE_OFF), w_in[:, n0:n0 + MLA_ROPE],
         zeros(D_MODEL, MLA_HP - MLA_ROPE_OFF - MLA_ROPE), w_in[:, n0 + MLA_ROPE:]], axis=1)
    uq = w["mla_w_uq"][l].reshape(MLA_Q_LORA, MLA_HEADS, MLA_NOPE + MLA_ROPE)
    uq = jnp.pad(uq, ((0, 0), (0, 0), (0, MLA_HP - MLA_NOPE - MLA_ROPE))).reshape(MLA_Q_LORA, -1)
    ukv = w["mla_w_ukv"][l].reshape(MLA_KV_LORA, MLA_HEADS, MLA_NOPE + MLA_V)
    uk = jnp.pad(ukv[:, :, :MLA_NOPE], ((0, 0), (0, 0), (0, MLA_HP - MLA_NOPE))).reshape(MLA_KV_LORA, -1)
    uv = jnp.pad(ukv[:, :, MLA_NOPE:], ((0, 0), (0, 0), (0, MLA_HP - MLA_V))).reshape(MLA_KV_LORA, -1)
    return dict(
        g0=w["norm_g"][l, 0:1], norm_g=w["norm_g"],
        w_in=w_in_p.astype(BF16), w_out=w["w_out"][l].astype(BF16),
        q_norm=w["mla_q_norm"][l][None], kv_norm=w["mla_kv_norm"][l][None],
        w_uq=uq.astype(BF16), w_ukv=jnp.concatenate([uk, uv], axis=1).astype(BF16),
        dec_rows=w["dec_rows"],
        hy_short_w=w["hy_short_w"][l], hy_short_b=w["hy_short_b"][l][None],
        hy_w1=jnp.pad(w["hy_w1"][l], ((0, HY_EMB_PAD - HY_EMB), (0, 0))), hy_b1=w["hy_b1"][l][None],
        hy_w2=w["hy_w2"][l], hy_b2=w["hy_b2"][l][None], hy_w3=w["hy_w3"][l],
        hy_bias=w["hy_bias"][l],
        w_gate=w["w_gate"][l].astype(BF16), w_up=w["w_up"][l].astype(BF16),
        w_down=w["w_down"][l].astype(BF16),
    )


def kernel(x_prompt, x_sample, cache_ckv, cache_krope, state_ret, c, c_ctx, w_ada, b_ada, norm_g, w_in, w_out,
           ret_decay, mla_q_norm, mla_kv_norm, mla_w_uq, mla_w_ukv, hy_short_w, hy_short_b, hy_w1, hy_b1,
           hy_w2, hy_b2, hy_w3, hy_bias, w_gate, w_up, w_down):
    nb = c.shape[0]
    cvec = jnp.concatenate([c, c_ctx[None], jnp.zeros((MOD_ROWS - nb - 1, D_MODEL), F32)], axis=0)
    mod = _ada(cvec, w_ada, b_ada).reshape(DEPTH, MOD_ROWS, 1, 6 * D_MODEL)
    dec_rows = jnp.pad(jnp.repeat(ret_decay, RET_DV, axis=-1), ((0, 0), (0, 6), (0, 0)))
    krope_pad = jnp.pad(cache_krope, ((0, 0), (0, 0), (0, 0),
                                      (MLA_ROPE_OFF, MLA_HP - MLA_ROPE_OFF - MLA_ROPE)))
    w = dict(norm_g=norm_g, w_in=w_in, w_out=w_out, dec_rows=dec_rows, mla_q_norm=mla_q_norm,
             mla_kv_norm=mla_kv_norm, mla_w_uq=mla_w_uq, mla_w_ukv=mla_w_ukv, hy_short_w=hy_short_w,
             hy_short_b=hy_short_b, hy_w1=hy_w1, hy_b1=hy_b1, hy_w2=hy_w2, hy_b2=hy_b2, hy_w3=hy_w3,
             hy_bias=hy_bias, w_gate=w_gate, w_up=w_up, w_down=w_down)
    xp, xs = x_prompt, x_sample
    new_ckv, new_kr, new_s = [], [], []
    for l in range(DEPTH):
        p = _prep_layer(l, w)
        xp, ckv, krp, s_ret = _layer(xp, mod, l, p, False, None, None)
        new_ckv.append(ckv)
        new_kr.append(krp[:, :, MLA_ROPE_OFF:MLA_ROPE_OFF + MLA_ROPE])
        new_s.append(s_ret)
        ctx_kv = _ctx_kv(cache_ckv, krope_pad, l, p["w_ukv"])
        xs, _, _, _ = _layer(xs, mod, l, p, True, ctx_kv, state_ret)
    return (xp, xs, jnp.stack(new_ckv, axis=1), jnp.stack(new_kr, axis=1), jnp.stack(new_s, axis=1))
```

```python
import functools
import math

import numpy as np
import jax
import jax.numpy as jnp
from jax import lax
from jax.experimental import pallas as pl
from jax.experimental.pallas import tpu as pltpu

F32 = jnp.float32
BF16 = jnp.bfloat16

D_MODEL = 1024
DEPTH = 2
GRID_W = 64
EPS = 1e-6
ROPE_BASE = 10000.0
RET_HEADS = 4
RET_DK = 64
RET_DV = 64
RET_W = RET_HEADS * RET_DV
FNET_GROUPS = 4
FNET_GC = 64
FNET_W = FNET_GROUPS * FNET_GC
MLA_HEADS = 4
MLA_Q_LORA = 256
MLA_KV_LORA = 128
MLA_NOPE = 64
MLA_ROPE = 32
MLA_V = 64
MLA_W = MLA_HEADS * MLA_V
MLA_HP = 128
MLA_ROPE_OFF = MLA_NOPE
ATTN_CK = 256
SUBL = 8
FFT_TN = 256
FFT_S = 64
FFT_MIN_L = 1024
HY_CH = 256
HY_ORDER = 2
HY_BANDS = 16
HY_EMB = 1 + 2 * HY_BANDS
HY_EMB_PAD = 128
HY_FFN = 64
HY_FAST_DECAY = 0.3
HY_SLOW_DECAY = 1.5
HY_TARGET = 1e-2
D_FF = ((8 * D_MODEL + 3 * 256 - 1) // (3 * 256)) * 256
FFN_CHUNK = 256
MOD_ROWS = 16
CTX_ROW = 8

OFF_RET = 0
OFF_FU = 4 * RET_W
OFF_CQ = OFF_FU + FNET_W
OFF_CKV = OFF_CQ + MLA_Q_LORA
OFF_KR = OFF_CKV + MLA_KV_LORA
OFF_HU = OFF_KR + MLA_HP
IN_WP = OFF_HU + 3 * HY_CH

VMEM_LIMIT = 52 * 1024 * 1024


def _cparams(sem):
    return pltpu.CompilerParams(dimension_semantics=sem, vmem_limit_bytes=VMEM_LIMIT)


def _tiles(L):
    return dict(
        tm=min(L, 512),
        tm_ffn=min(L, 512),
        chunk=min(L, 256),
        tq=min(L, 256),
        hb=min(L, 512),
        tcol=1024,
    )


def _bf16_const(a):
    return jnp.asarray(np.asarray(a, np.float32)).astype(BF16)


@functools.lru_cache(maxsize=None)
def _rope_np(L, width, seg_off, seg_w, rope_dim):
    row = np.repeat(np.arange(L // GRID_W), GRID_W).astype(np.float64)
    col = np.tile(np.arange(GRID_W), L // GRID_W).astype(np.float64)
    quarter = rope_dim // 4
    inv = ROPE_BASE ** (-np.arange(quarter, dtype=np.float64) / quarter)
    cos = np.ones((L, width), np.float64)
    sin = np.zeros((L, width), np.float64)
    for lane in range(width):
        r = lane % seg_w - seg_off
        if r < 0 or r >= rope_dim:
            continue
        pos = row if r < rope_dim // 2 else col
        ang = pos * inv[r % quarter]
        first = (r % (rope_dim // 2)) < quarter
        cos[:, lane] = np.cos(ang)
        sin[:, lane] = -np.sin(ang) if first else np.sin(ang)
    return cos.astype(np.float32), sin.astype(np.float32)


@functools.lru_cache(maxsize=None)
def _chan_dft_np():
    k = np.arange(FNET_GC)
    ang = 2.0 * np.pi * np.outer(k, k) / FNET_GC
    c = np.kron(np.eye(FNET_GROUPS), np.cos(ang))
    s = np.kron(np.eye(FNET_GROUPS), np.sin(ang))
    return np.concatenate([c, s], axis=1)


@functools.lru_cache(maxsize=None)
def _pos_dft_np(L):
    k = np.arange(L, dtype=np.int64)
    ang = 2.0 * np.pi * (np.outer(k, k) % L) / L
    return np.concatenate([np.cos(ang), -np.sin(ang)], axis=1)


@functools.lru_cache(maxsize=None)
def _hyena_dft_np(L, hb):
    k = np.arange(L, dtype=np.int64)
    ang = np.pi * (np.outer(k, k) % (2 * L)) / L
    cf = np.cos(ang)
    sf = np.sin(ang)
    alt = (-1.0) ** np.arange(L)
    sf[0, :] = alt
    w = np.where(k == 0, 1.0, 2.0)[None, :] / (2.0 * L)
    ainv = cf.T * w
    binv = sf.T / L
    binv[:, 0] = alt / (2.0 * L)
    fw, iv = [], []
    for t in range(L // hb):
        sl = slice(t * hb, (t + 1) * hb)
        fw += [cf[sl], sf[sl]]
        iv += [ainv[:, sl], binv[:, sl]]
    return np.concatenate(fw, axis=0), np.concatenate(iv, axis=1)


@functools.lru_cache(maxsize=None)
def _hyena_feat_np(L):
    pos = np.arange(L, dtype=np.float64)
    t = pos / L
    bands = np.arange(1, HY_BANDS + 1, dtype=np.float64)
    ang = (2.0 * math.pi / L) * pos[:, None] * bands[None, :]
    z = np.zeros((L, HY_EMB_PAD), np.float64)
    z[:, 0] = t
    z[:, 1:1 + HY_BANDS] = np.sin(ang)
    z[:, 1 + HY_BANDS:HY_EMB] = np.cos(ang)
    deltas = np.abs(np.linspace(math.log(HY_TARGET) / HY_SLOW_DECAY,
                                math.log(HY_TARGET) / HY_FAST_DECAY, HY_CH))
    window = np.exp(-t[:, None] * deltas[None, :])
    return z.astype(np.float32), window.astype(np.float32)


@functools.lru_cache(maxsize=None)
def _hy_fwd_tabs_np(L, S):
    R, Q, D = 2 * L // S, L // S, S // 2
    c, a = np.arange(R)[:, None], np.arange(Q)[None, :]
    ang = 2 * np.pi * ((c * a) % R) / R
    f1 = np.concatenate([np.cos(ang), -np.sin(ang)], axis=0)
    cc, d, b = np.arange(R)[:, None, None], np.arange(D)[None, :, None], np.arange(S)[None, None, :]
    th = np.pi * (((cc + R * d) * b) % (2 * L)) / L
    ur = np.concatenate([np.cos(th), np.sin(th)], axis=2)
    us = np.concatenate([np.sin(th), -np.cos(th)], axis=2)
    us[0, 0, :S] = (-1.0) ** np.arange(S)
    us[0, 0, S:] = 0.0
    return f1, np.concatenate([ur, us], axis=1)


@functools.lru_cache(maxsize=None)
def _hy_inv_tabs_np(L, S):
    R, Q, D = 2 * L // S, L // S, S // 2
    cc, f, d = np.arange(R)[:, None, None], np.arange(S)[None, :, None], np.arange(D)[None, None, :]
    k = cc + R * d
    be = np.pi * ((k * f) % (2 * L)) / L
    w = np.where(k == 0, 1.0, 2.0) / (2.0 * L) + 0.0 * be
    bc = np.concatenate([w * np.cos(be), w * np.sin(be)], axis=2)
    bs = np.concatenate([-w * np.sin(be), w * np.cos(be)], axis=2)
    bc[0, :, D] = (-1.0) ** np.arange(S) / (2.0 * L)
    bs[0, :, D] = 0.0
    e, c = np.arange(Q)[:, None], np.arange(R)[None, :]
    al = 2 * np.pi * ((e * c) % R) / R
    return np.concatenate([bc, bs], axis=1), np.concatenate([np.cos(al), np.sin(al)], axis=1)


@functools.lru_cache(maxsize=None)
def _four_tabs_np(L, S):
    R = L // S
    c, a = np.arange(R)[:, None], np.arange(R)[None, :]
    ang = 2 * np.pi * ((c * a) % R) / R
    f1 = np.concatenate([np.concatenate([np.cos(ang), -np.sin(ang)], axis=1),
                         np.concatenate([-np.sin(ang), -np.cos(ang)], axis=1)], axis=0)
    cc, d, b = np.arange(R)[:, None, None], np.arange(S)[None, :, None], np.arange(S)[None, None, :]
    th = 2 * np.pi * (((cc + R * d) * b) % L) / L
    return f1, np.concatenate([np.cos(th), np.sin(th)], axis=2)


def _rms(x, g):
    return x * lax.rsqrt(jnp.mean(x * x, axis=-1, keepdims=True) + EPS) * g


def _silu(x):
    return x * jax.nn.sigmoid(x)


def _bdot(a, b):
    return jnp.dot(a.astype(BF16), b.astype(BF16), preferred_element_type=F32)


def _split(a):
    hi = a.astype(BF16)
    lo = (a - hi.astype(F32)).astype(BF16)
    return hi, lo


def _dot3(a, b):
    ah, al = _split(a)
    bh, bl = _split(b)
    d = lambda x, y: jnp.dot(x, y, preferred_element_type=F32)
    return d(ah, bh) + (d(al, bh) + d(ah, bl))


def _store_kv(kv, kr_pad, k_ref, v_ref):
    wk = MLA_HEADS * MLA_HP
    k_ref[...] = (kv[:, :wk] + jnp.concatenate([kr_pad] * MLA_HEADS, axis=1)).astype(k_ref.dtype)
    lane = lax.broadcasted_iota(jnp.int32, (1, wk), 1)
    ones = jnp.where(lane % MLA_HP >= MLA_V, 1.0, 0.0)
    v_ref[...] = (kv[:, wk:] + ones).astype(v_ref.dtype)


def _rope(x, cos, sin_signed, quarter):
    w = x.shape[-1]
    lane = lax.broadcasted_iota(jnp.int32, x.shape, 1)
    first = (lane % (2 * quarter)) < quarter
    up = pltpu.roll(x, w - quarter, axis=1)
    dn = pltpu.roll(x, quarter, axis=1)
    return x * cos + jnp.where(first, up, dn) * sin_signed


def _ada_kernel(c_ref, w_ref, b_ref, o_ref):
    s = _silu(c_ref[...])
    o_ref[...] = _bdot(s, w_ref[...]) + b_ref[...]


def _ada(cvec, w_ada, b_ada):
    tn = 1024
    n6 = w_ada.shape[-1]
    return pl.pallas_call(
        _ada_kernel,
        grid=(DEPTH, n6 // tn),
        in_specs=[
            pl.BlockSpec((MOD_ROWS, D_MODEL), lambda l, j: (0, 0)),
            pl.BlockSpec((None, D_MODEL, tn), lambda l, j: (l, 0, j)),
            pl.BlockSpec((None, 1, tn), lambda l, j: (l, 0, j)),
        ],
        out_specs=pl.BlockSpec((None, MOD_ROWS, tn), lambda l, j: (l, 0, j)),
        out_shape=jax.ShapeDtypeStruct((DEPTH, MOD_ROWS, n6), F32),
        compiler_params=_cparams(("parallel", "parallel")),
        name="ada",
    )(cvec, w_ada, b_ada.reshape(DEPTH, 1, n6))


def _in_proj_kernel(*refs, latent):
    (x_ref, mod_ref, g0_ref, w_ref, qn_ref, kvn_ref, wuq_ref, wukv_ref, bd_ref) = refs[:9]
    pos = 9
    if latent:
        cr_ref, sr_ref, cm_ref, sm_ref = refs[pos:pos + 4]
        pos += 4
        qkvg_ref, fx_ref, q_ref, k_ref, v_ref, hu_ref = refs[pos:pos + 6]
        lanes = lambda ref, reps: jnp.concatenate([ref[...]] * reps, axis=1)
    else:
        qkvg_ref, fx_ref, q_ref, k_ref, v_ref, hu_ref, ckv_ref, kr_ref = refs[pos:pos + 8]

    x = x_ref[...]
    mod = mod_ref[...]
    sh1 = mod[:, 0:D_MODEL]
    sc1 = mod[:, D_MODEL:2 * D_MODEL]
    h = _rms(x, g0_ref[...]) * (1.0 + sc1) + sh1
    proj = jnp.dot(h.astype(BF16), w_ref[...], preferred_element_type=F32)

    rq = proj[:, OFF_RET:OFF_RET + RET_W]
    rk = proj[:, OFF_RET + RET_W:OFF_RET + 2 * RET_W] * (RET_DK ** -0.5)
    if latent:
        cos_r, sin_r = lanes(cr_ref, RET_W // 128), lanes(sr_ref, RET_W // 128)
        rq = _rope(rq, cos_r, sin_r, RET_DK // 4)
        rk = _rope(rk, cos_r, sin_r, RET_DK // 4)
    qkvg_ref[:, 0:RET_W] = rq
    qkvg_ref[:, RET_W:2 * RET_W] = rk
    qkvg_ref[:, 2 * RET_W:4 * RET_W] = proj[:, OFF_RET + 2 * RET_W:OFF_RET + 4 * RET_W]

    fcs = _bdot(proj[:, OFF_FU:OFF_FU + FNET_W], bd_ref[...])
    fx_ref[0] = fcs[:, :FNET_W].astype(fx_ref.dtype)
    fx_ref[1] = fcs[:, FNET_W:].astype(fx_ref.dtype)

    cqn = _rms(proj[:, OFF_CQ:OFF_CQ + MLA_Q_LORA], qn_ref[...])
    q = _bdot(cqn, wuq_ref[...])
    ckvn = _rms(proj[:, OFF_CKV:OFF_CKV + MLA_KV_LORA], kvn_ref[...])
    kv = _bdot(ckvn, wukv_ref[...])
    krp = proj[:, OFF_KR:OFF_KR + MLA_HP]
    if latent:
        q = _rope(q, lanes(cm_ref, MLA_HEADS), lanes(sm_ref, MLA_HEADS), MLA_ROPE // 4)
        krp = _rope(krp, cm_ref[...], sm_ref[...], MLA_ROPE // 4)
    else:
        ckv_ref[...] = ckvn
        kr_ref[...] = krp
    q_ref[...] = (q * ((MLA_NOPE + MLA_ROPE) ** -0.5 * math.log2(math.e))).astype(q_ref.dtype)
    _store_kv(kv, krp, k_ref, v_ref)

    hu_ref[...] = proj[:, OFF_HU:OFF_HU + 3 * HY_CH]


def _in_proj(x, mod, l, p, latent):
    B, L, _ = x.shape
    t = _tiles(L)
    tm = t["tm"]
    nt = L // tm
    row = (lambda b: b) if latent else (lambda b: CTX_ROW)
    full = lambda shape: pl.BlockSpec(shape, lambda i, b: (0,) * len(shape), pipeline_mode=pl.Buffered(1))
    in_specs = [
        pl.BlockSpec((None, tm, D_MODEL), lambda i, b: (b, i, 0)),
        pl.BlockSpec((None, None, 1, 6 * D_MODEL), lambda i, b: (l, row(b), 0, 0)),
        full((1, D_MODEL)),
        full((D_MODEL, IN_WP)),
        full((1, MLA_Q_LORA)),
        full((1, MLA_KV_LORA)),
        full((MLA_Q_LORA, MLA_HEADS * MLA_HP)),
        full((MLA_KV_LORA, 2 * MLA_HEADS * MLA_HP)),
        full((FNET_W, 2 * FNET_W)),
    ]
    args = [x, mod, p["g0"], p["w_in"], p["q_norm"], p["kv_norm"], p["w_uq"], p["w_ukv"],
            _bf16_const(_chan_dft_np())]
    if latent:
        tabs = (_rope_np(L, 128, 0, RET_DK, RET_DK)
                + _rope_np(L, MLA_HP, MLA_ROPE_OFF, MLA_HP, MLA_ROPE))
        for tab in tabs:
            in_specs.append(pl.BlockSpec((tm, tab.shape[1]), lambda i, b: (i, 0)))
            args.append(jnp.asarray(tab))
    bm = lambda w: pl.BlockSpec((None, tm, w), lambda i, b: (b, i, 0))
    sds = jax.ShapeDtypeStruct
    out_specs = [
        bm(4 * RET_W),
        pl.BlockSpec((2, tm, FNET_W), lambda i, b: (0, i, b)),
        bm(MLA_HEADS * MLA_HP), bm(MLA_HEADS * MLA_HP), bm(MLA_HEADS * MLA_HP),
        pl.BlockSpec((tm, 3 * HY_CH), lambda i, b: (i, b)),
    ]
    out_shape = [
        sds((B, L, 4 * RET_W), F32),
        sds((2, L, B * FNET_W), F32 if _use_fft(L) else BF16),
        sds((B, L, MLA_HEADS * MLA_HP), BF16),
        sds((B, L, MLA_HEADS * MLA_HP), BF16),
        sds((B, L, MLA_HEADS * MLA_HP), BF16),
        sds((L, B * 3 * HY_CH), F32),
    ]
    if not latent:
        out_specs += [bm(MLA_KV_LORA), bm(MLA_HP)]
        out_shape += [sds((B, L, MLA_KV_LORA), F32), sds((B, L, MLA_HP), F32)]
    outs = pl.pallas_call(
        functools.partial(_in_proj_kernel, latent=latent),
        grid=(nt, B), in_specs=in_specs, out_specs=out_specs, out_shape=out_shape,
        compiler_params=_cparams(("parallel", "parallel")),
        name="in_proj_lat" if latent else "in_proj_ctx",
    )(*args)
    return tuple(outs) + ((None, None) if latent else ())


def _ctx_kv_kernel(ckv_ref, kr_ref, w_ref, k_ref, v_ref):
    _store_kv(_bdot(ckv_ref[...], w_ref[...]), kr_ref[...], k_ref, v_ref)


def _ctx_kv(cache_ckv, krope_pad, l, w_ukv):
    B, _, P, _ = cache_ckv.shape
    return pl.pallas_call(
        _ctx_kv_kernel,
        grid=(B,),
        in_specs=[
            pl.BlockSpec((None, None, P, MLA_KV_LORA), lambda b: (b, l, 0, 0)),
            pl.BlockSpec((None, None, P, MLA_HP), lambda b: (b, l, 0, 0)),
            pl.BlockSpec(w_ukv.shape, lambda b: (0, 0)),
        ],
        out_specs=[
            pl.BlockSpec((None, P, MLA_HEADS * MLA_HP), lambda b: (b, 0, 0)),
            pl.BlockSpec((None, P, MLA_HEADS * MLA_HP), lambda b: (b, 0, 0)),
        ],
        out_shape=[
            jax.ShapeDtypeStruct((B, P, MLA_HEADS * MLA_HP), BF16),
            jax.ShapeDtypeStruct((B, P, MLA_HEADS * MLA_HP), BF16),
        ],
        compiler_params=_cparams(("parallel",)),
        name="ctx_kv",
    )(cache_ckv, krope_pad, w_ukv)


def _ret_kernel(*refs, L, C, has_s0, emit_state):
    q_ref, k_ref, v_ref, g_ref, dec_ref = refs[:5]
    pos = 5
    if has_s0:
        s0_ref = refs[pos]
        pos += 1
    o_ref = refs[pos]
    pos += 1
    if emit_state:
        sn_ref = refs[pos]
        pos += 1
    sf_ref, sb_ref, dm_ref, tab_ref = refs[pos:pos + 4]
    n = L // C
    H, W = RET_HEADS, RET_W

    d = dec_ref[0:2, :]
    lg = jnp.minimum(d, 0.0) - jnp.log1p(jnp.exp(-jnp.abs(d)))
    lf, lb = lg[0:1, :], lg[1:2, :]
    @pl.when(pl.program_id(0) == 0)
    def _():
        ii = lax.broadcasted_iota(jnp.int32, (C, C), 0)
        jj = lax.broadcasted_iota(jnp.int32, (C, C), 1)
        diff = (ii - jj).astype(F32)
        for h in range(H):
            lfh = lf[:, h * RET_DV:h * RET_DV + 1]
            lbh = lb[:, h * RET_DV:h * RET_DV + 1]
            dm_ref[h] = jnp.where(diff > 0, jnp.exp(diff * lfh),
                                  jnp.where(diff < 0, jnp.exp(-diff * lbh), 2.0))
        ci = lax.broadcasted_iota(jnp.int32, (C, W), 0).astype(F32)
        tab_ref[0] = jnp.exp((ci + 1.0) * lf)
        tab_ref[1] = jnp.exp((C - ci) * lb)
        tab_ref[2] = jnp.exp((C - 1.0 - ci) * lf)
        tab_ref[3] = jnp.exp(ci * lb)

    g_f = jnp.exp(C * lf)
    g_b = jnp.exp(C * lb)
    hr = lax.broadcasted_iota(jnp.int32, (W, W), 0) // RET_DK
    hc = lax.broadcasted_iota(jnp.int32, (W, W), 1) // RET_DV
    same_head = hr == hc
    avg = jnp.where(same_head, 1.0 / RET_DV, 0.0).astype(BF16)

    sf_ref[0] = jnp.zeros((W, W), F32)
    sb_ref[n] = jnp.zeros((W, W), F32)
    if has_s0:
        for h in range(H):
            blk = slice(h * RET_DK, (h + 1) * RET_DK)
            sf_ref[0, blk, blk] = s0_ref[0, h]
            sb_ref[n, blk, blk] = s0_ref[1, h]

    tdot = lambda a, b: lax.dot_general(a.astype(BF16), b.astype(BF16), (((0,), (0,)), ((), ())),
                                        preferred_element_type=F32)

    def states(j, carry):
        jb = n - 1 - j
        rf = pl.multiple_of(j * C, C)
        rb = pl.multiple_of(jb * C, C)
        upd = tdot(k_ref[pl.ds(rf, C), :] * tab_ref[2], v_ref[pl.ds(rf, C), :])
        sf_ref[j + 1] = g_f * sf_ref[j] + jnp.where(same_head, upd, 0.0)
        upd = tdot(k_ref[pl.ds(rb, C), :] * tab_ref[3], v_ref[pl.ds(rb, C), :])
        sb_ref[jb] = g_b * sb_ref[jb + 1] + jnp.where(same_head, upd, 0.0)
        return carry

    lax.fori_loop(0, n, states, 0)

    lane_head = lax.broadcasted_iota(jnp.int32, (C, W), 1) // RET_DV

    def group_mean(x):
        hi, lo = _split(x)
        return (jnp.dot(hi, avg, preferred_element_type=F32)
                + jnp.dot(lo, avg, preferred_element_type=F32))

    def chunk(j, carry):
        r0 = pl.multiple_of(j * C, C)
        q = q_ref[pl.ds(r0, C), :]
        qb = q.astype(BF16)
        kb = k_ref[pl.ds(r0, C), :].astype(BF16)
        vb = v_ref[pl.ds(r0, C), :].astype(BF16)
        o = tab_ref[0] * jnp.dot(qb, sf_ref[j].astype(BF16), preferred_element_type=F32)
        o = o + tab_ref[1] * jnp.dot(qb, sb_ref[j + 1].astype(BF16), preferred_element_type=F32)
        inner = jnp.zeros((C, W), F32)
        for h in range(H):
            in_head = lane_head == h
            qh = jnp.where(in_head, q, 0.0).astype(BF16)
            s = lax.dot_general(qh, kb, (((1,), (1,)), ((), ())), preferred_element_type=F32)
            oh = jnp.dot((s * dm_ref[h]).astype(BF16), vb, preferred_element_type=F32)
            inner = jnp.where(in_head, oh, inner)
        o = o + inner
        dlt = o - group_mean(o)
        var = group_mean(dlt * dlt)
        gate = g_ref[pl.ds(r0, C), :]
        o_ref[pl.ds(r0, C), :] = (_silu(gate) * (dlt * lax.rsqrt(var + EPS))).astype(o_ref.dtype)
        return carry

    lax.fori_loop(0, n, chunk, 0, unroll=4 if n % 4 == 0 else 1)

    if emit_state:
        for h in range(H):
            blk = slice(h * RET_DK, (h + 1) * RET_DK)
            sn_ref[0, h] = sf_ref[n, blk, blk]
            sn_ref[1, h] = sb_ref[0, blk, blk]


def _retention(qkvg, dec_rows, s0, l, emit_state):
    B, L, _ = qkvg.shape
    C = _tiles(L)["chunk"]
    n = L // C
    has_s0 = s0 is not None
    def colspec(j):
        return pl.BlockSpec((None, L, RET_W), lambda b: (b, 0, j))

    in_specs = [colspec(j) for j in range(4)]
    in_specs.append(pl.BlockSpec((None, 8, RET_W), lambda b: (l, 0, 0)))
    args = [qkvg, qkvg, qkvg, qkvg, dec_rows]
    st_block = (None, None, 2, RET_HEADS, RET_DK, RET_DV)
    if has_s0:
        in_specs.append(pl.BlockSpec(st_block, lambda b: (b, l, 0, 0, 0, 0)))
        args.append(s0)
    out_specs = [pl.BlockSpec((None, L, RET_W), lambda b: (b, 0, 0))]
    out_shape = [jax.ShapeDtypeStruct((B, L, RET_W), BF16)]
    if emit_state:
        out_specs.append(pl.BlockSpec((None, 2, RET_HEADS, RET_DK, RET_DV), lambda b: (b, 0, 0, 0, 0)))
        out_shape.append(jax.ShapeDtypeStruct((B, 2, RET_HEADS, RET_DK, RET_DV), F32))
    outs = pl.pallas_call(
        functools.partial(_ret_kernel, L=L, C=C, has_s0=has_s0, emit_state=emit_state),
        grid=(B,), in_specs=in_specs, out_specs=out_specs, out_shape=out_shape,
        scratch_shapes=[
            pltpu.VMEM((n + 1, RET_W, RET_W), F32),
            pltpu.VMEM((n + 1, RET_W, RET_W), F32),
            pltpu.VMEM((RET_HEADS, C, C), F32),
            pltpu.VMEM((4, C, RET_W), F32),
        ],
        compiler_params=_cparams(("arbitrary",)),
        name="retention_lat" if has_s0 else "retention_ctx",
    )(*args)
    return outs if emit_state else (outs[0], None)


def _mm_kernel(a_ref, b_ref, o_ref, *, scale):
    o_ref[...] = (jnp.dot(a_ref[...], b_ref[...], preferred_element_type=F32) * scale).astype(o_ref.dtype)


def _mm(a, b, scale, out_dtype, tm, tn, name):
    M, K = a.shape
    _, N = b.shape
    tm, tn = min(tm, M), min(tn, N)
    return pl.pallas_call(
        functools.partial(_mm_kernel, scale=scale),
        grid=(N // tn, M // tm),
        in_specs=[pl.BlockSpec((tm, K), lambda j, i: (i, 0)),
                  pl.BlockSpec((K, tn), lambda j, i: (0, j))],
        out_specs=pl.BlockSpec((tm, tn), lambda j, i: (i, j)),
        out_shape=jax.ShapeDtypeStruct((M, N), out_dtype),
        compiler_params=_cparams(("parallel", "parallel")),
        name=name,
    )(a, b)


def _attn_kernel(*refs, has_ctx):
    if has_ctx:
        q_ref, k_ref, v_ref, kc_ref, vc_ref, o_ref, s_ref = refs
        srcs = ((k_ref, v_ref), (kc_ref, vc_ref))
    else:
        q_ref, k_ref, v_ref, o_ref, s_ref = refs
        srcs = ((k_ref, v_ref),)
    nt = lambda a, b: lax.dot_general(a, b, (((1,), (1,)), ((), ())), preferred_element_type=F32)
    chunks = [(kr, vr, c0) for kr, vr in srcs for c0 in range(0, kr.shape[0], ATTN_CK)]
    def scores(h):
        hq = slice(h * MLA_HP, (h + 1) * MLA_HP)
        qh = q_ref[:, hq]
        mx = None
        for n, (kr, _, c0) in enumerate(chunks):
            s = nt(qh, kr[c0:c0 + ATTN_CK, hq])
            s_ref[h % 2, :, n * ATTN_CK:(n + 1) * ATTN_CK] = s
            for g in range(ATTN_CK // 128):
                part = s[:, g * 128:(g + 1) * 128]
                mx = part if mx is None else jnp.maximum(mx, part)
        return jnp.max(mx, axis=-1, keepdims=True)

    def weighted_values(h, m):
        hq = slice(h * MLA_HP, (h + 1) * MLA_HP)
        acc = None
        for n, (_, vr, c0) in enumerate(chunks):
            p = jnp.exp2(s_ref[h % 2, :, n * ATTN_CK:(n + 1) * ATTN_CK] - m).astype(BF16)
            d = jnp.dot(p, vr[c0:c0 + ATTN_CK, hq], preferred_element_type=F32)
            acc = d if acc is None else acc + d
        o_ref[:, h * MLA_V:(h + 1) * MLA_V] = (acc[:, :MLA_V] / acc[:, MLA_V:2 * MLA_V]).astype(o_ref.dtype)

    m = scores(0)
    for h in range(MLA_HEADS):
        m_next = scores(h + 1) if h + 1 < MLA_HEADS else None
        weighted_values(h, m)
        m = m_next


def _attention(q, k, v, kc, vc):
    B, L, _ = q.shape
    tq = _tiles(L)["tq"]
    has_ctx = kc is not None
    whole = lambda a: pl.BlockSpec((None,) + a.shape[1:], lambda b, i: (b, 0, 0))
    in_specs = [pl.BlockSpec((None, tq, q.shape[2]), lambda b, i: (b, i, 0)), whole(k), whole(v)]
    args = [q, k, v]
    n_keys = L
    if has_ctx:
        in_specs += [whole(kc), whole(vc)]
        args += [kc, vc]
        n_keys += kc.shape[1]
    return pl.pallas_call(
        functools.partial(_attn_kernel, has_ctx=has_ctx),
        grid=(B, L // tq), in_specs=in_specs,
        out_specs=pl.BlockSpec((None, tq, MLA_W), lambda b, i: (b, i, 0)),
        out_shape=jax.ShapeDtypeStruct((B, L, MLA_W), BF16),
        scratch_shapes=[pltpu.VMEM((2, tq, n_keys), F32)],
        compiler_params=_cparams(("parallel", "parallel")),
        name="attention_lat" if has_ctx else "attention_ctx",
    )(*args)


def _hy_filter_kernel(z_ref, w1_ref, b1_ref, w2_ref, b2_ref, w3_ref, win_ref, x_ref):
    h1 = jnp.sin(_dot3(z_ref[...], w1_ref[...]) + b1_ref[...])
    h2 = jnp.sin(_dot3(h1, w2_ref[...]) + b2_ref[...])
    w3 = w3_ref[...]
    h = jnp.concatenate([_dot3(h2[:, :HY_FFN], w3), _dot3(h2[:, HY_FFN:], w3)], axis=0)
    win = win_ref[...]
    row = lax.broadcasted_iota(jnp.int32, win.shape, 0)
    for o in range(HY_ORDER):
        base = o * 2 * HY_CH
        hf = h[:, base:base + HY_CH] * win
        hb = h[:, base + HY_CH:base + 2 * HY_CH] * win
        hb = jnp.where(row == 0, 0.0, hb)
        nrm = jnp.sum(jnp.abs(hf), axis=0, keepdims=True) + jnp.sum(jnp.abs(hb), axis=0, keepdims=True) + EPS
        inv = 1.0 / nrm
        x_ref[:, base:base + HY_CH] = (hf + hb) * inv
        x_ref[:, base + HY_CH:base + 2 * HY_CH] = (hf - hb) * inv


def _hy_filter(L, p):
    z, win = _hyena_feat_np(L)
    z2 = np.concatenate([z[:L // 2], z[L // 2:]], axis=1)
    two = lambda w: jnp.kron(jnp.eye(2, dtype=F32), w)
    full = lambda shape: pl.BlockSpec(shape, lambda i: (0,) * len(shape))
    return pl.pallas_call(
        _hy_filter_kernel,
        grid=(1,),
        in_specs=[full((L // 2, 2 * HY_EMB_PAD)), full((2 * HY_EMB_PAD, 2 * HY_FFN)), full((1, 2 * HY_FFN)),
                  full((2 * HY_FFN, 2 * HY_FFN)), full((1, 2 * HY_FFN)),
                  full((HY_FFN, HY_ORDER * 2 * HY_CH)), full((L, HY_CH))],
        out_specs=full((L, HY_ORDER * 2 * HY_CH)),
        out_shape=jax.ShapeDtypeStruct((L, HY_ORDER * 2 * HY_CH), F32),
        compiler_params=_cparams(("arbitrary",)),
        name="hy_filter",
    )(jnp.asarray(z2), two(p["hy_w1"]), jnp.tile(p["hy_b1"], (1, 2)), two(p["hy_w2"]),
      jnp.tile(p["hy_b2"], (1, 2)), p["hy_w3"], jnp.asarray(win))


def _hy_gspec_kernel(f_ref, x_ref, g_ref, *, hb):
    xh, xl = _split(x_ref[...])
    f = f_ref[...]
    r = jnp.dot(f, xh, preferred_element_type=F32) + jnp.dot(f, xl, preferred_element_type=F32)
    ga = r[:hb, :HY_CH]
    gb = r[hb:, HY_CH:]
    nyq = r[hb:hb + 1, :HY_CH]
    k = pl.program_id(1) * hb + lax.broadcasted_iota(jnp.int32, (hb, HY_CH), 0)
    g_ref[0] = ga
    g_ref[1] = jnp.where(k == 0, 0.0, gb)
    g_ref[2] = jnp.where(k == 0, jnp.broadcast_to(nyq, ga.shape), ga)


def _hy_gspec(fwd, x, L, hb):
    return pl.pallas_call(
        functools.partial(_hy_gspec_kernel, hb=hb),
        grid=(HY_ORDER, L // hb),
        in_specs=[pl.BlockSpec((2 * hb, L), lambda o, t: (t, 0)),
                  pl.BlockSpec((L, 2 * HY_CH), lambda o, t: (0, o))],
        out_specs=pl.BlockSpec((3, hb, HY_CH), lambda o, t: (0, t, o)),
        out_shape=jax.ShapeDtypeStruct((3, L, HY_ORDER * HY_CH), F32),
        compiler_params=_cparams(("parallel", "parallel")),
        name="hy_gspec",
    )(fwd, x)


def _hy_conv_kernel(u_ref, w_ref, b_ref, v_ref, x1_ref, x2_ref, vb_ref):
    u = u_ref[...]
    L = u.shape[0]
    row = lax.broadcasted_iota(jnp.int32, u.shape, 0)
    prev = jnp.where(row == 0, 0.0, pltpu.roll(u, 1, axis=0))
    nxt = jnp.where(row == L - 1, 0.0, pltpu.roll(u, L - 1, axis=0))
    w = w_ref[...]
    uc = prev * w[0:1] + u * w[1:2] + nxt * w[2:3] + b_ref[...]
    v_ref[...] = uc[:, :HY_CH]
    x1_ref[...] = uc[:, HY_CH:2 * HY_CH]
    x2_ref[...] = uc[:, 2 * HY_CH:]
    vb_ref[...] = uc[:, :HY_CH].astype(vb_ref.dtype)


def _hy_conv(hu, B, w, b):
    L = hu.shape[0]
    col = pl.BlockSpec((L, HY_CH), lambda bb: (0, bb))
    sds = lambda dt: jax.ShapeDtypeStruct((L, B * HY_CH), dt)
    dts = [F32, F32, F32, BF16]
    return pl.pallas_call(
        _hy_conv_kernel,
        grid=(B,),
        in_specs=[pl.BlockSpec((L, 3 * HY_CH), lambda bb: (0, bb)),
                  pl.BlockSpec((3, 3 * HY_CH), lambda bb: (0, 0)),
                  pl.BlockSpec((1, 3 * HY_CH), lambda bb: (0, 0))],
        out_specs=[col] * len(dts),
        out_shape=[sds(dt) for dt in dts],
        compiler_params=_cparams(("parallel",)),
        name="hy_conv",
    )(hu, w, b)


def _hy_fwd_kernel(f_ref, u_ref, g_ref, y_ref, *, hb):
    acc = jnp.dot(f_ref[...], u_ref[...], preferred_element_type=F32)
    ga, gb, gd = g_ref[0], g_ref[1], g_ref[2]
    for c in range(u_ref.shape[1] // HY_CH):
        sl = slice(c * HY_CH, (c + 1) * HY_CH)
        ur = acc[:hb, sl]
        us = acc[hb:, sl]
        y_ref[:hb, sl] = (ur * ga - us * gb).astype(y_ref.dtype)
        y_ref[hb:, sl] = (ur * gb + us * gd).astype(y_ref.dtype)


def _hy_fwd(fwd, u, g, o, hb, tn):
    M, L = fwd.shape
    N = u.shape[1]
    tn = min(tn, N)
    return pl.pallas_call(
        functools.partial(_hy_fwd_kernel, hb=hb),
        grid=(N // tn, L // hb),
        in_specs=[pl.BlockSpec((2 * hb, L), lambda j, t: (t, 0)),
                  pl.BlockSpec((L, tn), lambda j, t: (0, j)),
                  pl.BlockSpec((3, hb, HY_CH), lambda j, t: (0, t, o))],
        out_specs=pl.BlockSpec((2 * hb, tn), lambda j, t: (t, j)),
        out_shape=jax.ShapeDtypeStruct((M, N), BF16),
        compiler_params=_cparams(("parallel", "parallel")),
        name="hy_fwd",
    )(fwd, u, g)


def _hy_inv_kernel(f_ref, y_ref, gate_ref, z_ref, d_ref, *out_refs):
    y = jnp.dot(f_ref[...], y_ref[...], preferred_element_type=F32)
    d = d_ref[...]
    for c in range(y.shape[1] // HY_CH):
        sl = slice(c * HY_CH, (c + 1) * HY_CH)
        z = gate_ref[:, sl] * (y[:, sl] + d * z_ref[:, sl])
        for r in out_refs:
            r[:, sl] = z.astype(r.dtype)


def _hy_inv(inv, y, gate, zprev, hy_bias, o, tm, tn, out_dtypes):
    L, K = inv.shape
    N = y.shape[1]
    tm, tn = min(tm, L), min(tn, N)
    tile = pl.BlockSpec((tm, tn), lambda j, i: (i, j))
    return pl.pallas_call(
        _hy_inv_kernel,
        grid=(N // tn, L // tm),
        in_specs=[pl.BlockSpec((tm, K), lambda j, i: (i, 0)),
                  pl.BlockSpec((K, tn), lambda j, i: (0, j)),
                  tile, tile,
                  pl.BlockSpec((1, HY_CH), lambda j, i: (0, 0))],
        out_specs=[tile] * len(out_dtypes),
        out_shape=[jax.ShapeDtypeStruct((L, N), dt) for dt in out_dtypes],
        compiler_params=_cparams(("parallel", "parallel")),
        name="hy_inv",
    )(inv, y, gate, zprev, hy_bias[o:o + 1])


def _tile_rows(ref, lead, rows):
    return jnp.concatenate([ref[lead, r * SUBL:(r + 1) * SUBL, :] for r in rows], axis=0)


def _fft_fwd_kernel(*refs, Q, R, S, filt):
    if filt:
        u_ref, k1_ref, e_ref, y_ref, a_scr = refs
    else:
        u_ref, k1_ref, e_ref, ga_ref, gb_ref, ny_ref, y_ref, a_scr = refs
    D = S // 2
    nb = S // SUBL
    tn = u_ref.shape[-1]
    dot = lambda p, q: jnp.dot(p, q, preferred_element_type=F32)
    k1 = k1_ref[...]
    for bh in range(nb):
        x = u_ref[:, bh].reshape(Q * SUBL, tn)
        if filt:
            xh, xl = _split(x)
            a_scr[bh] = dot(k1, xh) + dot(k1, xl)
        else:
            a_scr[bh] = dot(k1, x.astype(BF16))
    for c in range(R):
        rhs = jnp.concatenate([_tile_rows(a_scr, bh, (part * R + c,)) for part in (0, 1) for bh in range(nb)],
                              axis=0)
        if filt:
            rh, rl = _split(rhs)
            y_ref[c] = dot(e_ref[c], rh) + dot(e_ref[c], rl)
            continue
        y = dot(e_ref[c], rhs.astype(BF16))
        ur, us = y[:D], y[D:]
        ga, gb = ga_ref[c], gb_ref[c]
        gd = ga
        if c == 0:
            row0 = lax.broadcasted_iota(jnp.int32, ga.shape, 0) == 0
            gb = jnp.where(row0, 0.0, gb)
            gd = jnp.where(row0, jnp.broadcast_to(ny_ref[0, 0:1, :], ga.shape), ga)
        y_ref[c, :D, :] = (ur * ga - us * gb).astype(y_ref.dtype)
        y_ref[c, D:, :] = (ur * gb + us * gd).astype(y_ref.dtype)


def _fft_filter(taps, L):
    S = FFT_S
    R, Q = 2 * L // S, L // S
    N = taps.shape[1]
    tn = FFT_TN
    nb = S // SUBL
    f1_np, e_np = _hy_fwd_tabs_np(L, S)
    const = lambda shape: pl.BlockSpec(shape, lambda j: (0,) * len(shape), pipeline_mode=pl.Buffered(1))
    return pl.pallas_call(
        functools.partial(_fft_fwd_kernel, Q=Q, R=R, S=S, filt=True),
        grid=(N // tn,),
        in_specs=[pl.BlockSpec((Q, nb, SUBL, tn), lambda j: (0, 0, 0, j)),
                  const((2 * R * SUBL, Q * SUBL)), const((R, S, 2 * S))],
        out_specs=pl.BlockSpec((R, S, tn), lambda j: (0, 0, j)),
        out_shape=jax.ShapeDtypeStruct((R, S, N), F32),
        scratch_shapes=[pltpu.VMEM((nb, 2 * R * SUBL, tn), F32)],
        compiler_params=_cparams(("parallel",)),
        name="fft_filter",
    )(taps.reshape(Q, nb, SUBL, N), _bf16_const(np.kron(f1_np, np.eye(SUBL))), _bf16_const(e_np))


def _fft_inv_kernel(y_ref, e_ref, k2_ref, gate_ref, z_ref, d_ref, o_ref, scr, *, Q, R, S):
    dot = lambda p, q: jnp.dot(p, q, preferred_element_type=F32)
    tn = y_ref.shape[-1]
    per = scr.shape[1] // (2 * S)
    for c in range(R):
        scr[c // per, (c % per) * 2 * S:(c % per + 1) * 2 * S, :] = dot(e_ref[c], y_ref[c])
    k2 = k2_ref[...]
    d = d_ref[...]
    for fh in range(S // SUBL):
        rhs = jnp.concatenate([_tile_rows(scr, c // per, ((c % per) * 2 * S // SUBL + part * (S // SUBL) + fh,))
                               for part in (0, 1) for c in range(R)], axis=0).astype(BF16)
        conv = dot(k2, rhs).reshape(Q, SUBL, tn)
        o_ref[:, fh] = (gate_ref[:, fh] * (conv + d * z_ref[:, fh])).astype(o_ref.dtype)


def _hy_fused_kernel(hu_ref, w_ref, b_ref, k1_ref, e_ref, ei_ref, k2_ref, ga0, gb0, ny0, ga1, gb1, ny1,
                     bias_ref, o_ref, v4, x14, x24, z14, y_scr, scr, *, Q, R, S):
    L, tn = hu_ref.shape[0], HY_CH
    w, b = w_ref[...], b_ref[...]
    row = lax.broadcasted_iota(jnp.int32, (L, tn), 0)
    for g, dst in enumerate((v4, x14, x24)):
        cs = slice(g * tn, (g + 1) * tn)
        u = hu_ref[:, cs]
        prev = jnp.where(row == 0, 0.0, pltpu.roll(u, 1, axis=0))
        nxt = jnp.where(row == L - 1, 0.0, pltpu.roll(u, L - 1, axis=0))
        uc = prev * w[0:1, cs] + u * w[1:2, cs] + nxt * w[2:3, cs] + b[:, cs]
        dst[...] = uc.reshape(dst.shape)
    fwd = functools.partial(_fft_fwd_kernel, Q=Q, R=R, S=S, filt=False)
    inv = functools.partial(_fft_inv_kernel, Q=Q, R=R, S=S)
    fwd(v4, k1_ref, e_ref, ga0, gb0, ny0, y_scr, scr)
    inv(y_scr, ei_ref, k2_ref, x14, v4, bias_ref.at[0:1], z14, scr)
    fwd(z14, k1_ref, e_ref, ga1, gb1, ny1, y_scr, scr)
    inv(y_scr, ei_ref, k2_ref, x24, z14, bias_ref.at[1:2], o_ref, scr)


def _hy_fused(hu, B, spec, p):
    L = hu.shape[0]
    S = FFT_S
    R, Q, D = 2 * L // S, L // S, S // 2
    nb = S // SUBL
    tn = HY_CH
    f1_np, e_np = _hy_fwd_tabs_np(L, S)
    ei_np, f2_np = _hy_inv_tabs_np(L, S)
    const = lambda shape: pl.BlockSpec(shape, lambda j: (0,) * len(shape), pipeline_mode=pl.Buffered(1))
    one = lambda blk, idx: pl.BlockSpec(blk, lambda j: idx, pipeline_mode=pl.Buffered(1))
    g_specs = [s for o in range(HY_ORDER) for s in (one((R, D, tn), (0, 0, 2 * o)),
                                                    one((R, D, tn), (0, 1, 2 * o + 1)),
                                                    one((1, 8, tn), (0, D // 8, 2 * o)))]
    work = pltpu.VMEM((Q, nb, SUBL, tn), F32)
    out = pl.pallas_call(
        functools.partial(_hy_fused_kernel, Q=Q, R=R, S=S),
        grid=(B,),
        in_specs=[pl.BlockSpec((L, 3 * HY_CH), lambda j: (0, j)), const((3, 3 * HY_CH)), const((1, 3 * HY_CH)),
                  const((2 * R * SUBL, Q * SUBL)), const((R, S, 2 * S)), const((R, 2 * S, S)),
                  const((Q * SUBL, 2 * R * SUBL))] + g_specs + [const((HY_ORDER, HY_CH))],
        out_specs=pl.BlockSpec((Q, nb, SUBL, tn), lambda j: (0, 0, 0, j)),
        out_shape=jax.ShapeDtypeStruct((Q, nb, SUBL, B * HY_CH), BF16),
        scratch_shapes=[work, work, work, work, pltpu.VMEM((R, S, tn), BF16),
                        pltpu.VMEM((nb, 2 * R * SUBL, tn), F32)],
        compiler_params=_cparams(("parallel",)),
        name="hy_fused",
    )(hu, p["hy_short_w"], p["hy_short_b"], _bf16_const(np.kron(f1_np, np.eye(SUBL))), _bf16_const(e_np),
      _bf16_const(ei_np), _bf16_const(np.kron(f2_np, np.eye(SUBL))), *([spec] * 6), p["hy_bias"])
    return out.reshape(L, B * HY_CH)


def _fft_four_kernel(x_ref, k1_ref, e2_ref, o_ref, a_scr, *, R, S, scale):
    dot = lambda p, q: jnp.dot(p, q, preferred_element_type=F32)
    tn = x_ref.shape[-1]
    nb = S // SUBL
    k1 = k1_ref[...]
    for bh in range(nb):
        x = jnp.concatenate([x_ref[part, :, bh].reshape(R * SUBL, tn) for part in (0, 1)], axis=0)
        a_scr[bh] = dot(k1, x.astype(BF16))
    for ch in range(R // SUBL):
        rhs = jnp.concatenate([_tile_rows(a_scr, bh, (part * R + ch * SUBL + cl,))
                               for cl in range(SUBL) for part in (0, 1) for bh in range(nb)],
                              axis=0).astype(BF16)
        y = dot(e2_ref[ch], rhs)
        o_ref[:, ch] = (y.reshape(S, SUBL, tn) * scale).astype(o_ref.dtype)


def _fft_fourier(fx, L):
    S = FFT_S
    R = L // S
    N = fx.shape[2]
    tn = FFT_TN
    nb = S // SUBL
    f1_np, e_np = _four_tabs_np(L, S)
    e2 = np.zeros((R // SUBL, S, SUBL, SUBL, 2 * S))
    for cl in range(SUBL):
        e2[:, :, cl, cl, :] = e_np[cl::SUBL]
    e2 = e2.reshape(R // SUBL, S * SUBL, SUBL * 2 * S)
    const = lambda shape: pl.BlockSpec(shape, lambda j: (0,) * len(shape), pipeline_mode=pl.Buffered(1))
    out = pl.pallas_call(
        functools.partial(_fft_four_kernel, R=R, S=S, scale=(FNET_GC * L) ** -0.5),
        grid=(N // tn,),
        in_specs=[pl.BlockSpec((2, R, nb, SUBL, tn), lambda j: (0, 0, 0, 0, j)),
                  const((2 * R * SUBL, 2 * R * SUBL)), const(e2.shape)],
        out_specs=pl.BlockSpec((S, R // SUBL, SUBL, tn), lambda j: (0, 0, 0, j)),
        out_shape=jax.ShapeDtypeStruct((S, R // SUBL, SUBL, N), BF16),
        scratch_shapes=[pltpu.VMEM((nb, 2 * R * SUBL, tn), F32)],
        compiler_params=_cparams(("parallel",)),
        name="fft_fourier",
    )(fx.reshape(2, R, nb, SUBL, N), _bf16_const(np.kron(f1_np, np.eye(SUBL))), _bf16_const(e2))
    return out.reshape(L, N)


def _use_fft(L):
    return L >= FFT_MIN_L


def _hyena(hu, B, p):
    L = hu.shape[0]
    taps = _hy_filter(L, p)
    if _use_fft(L):
        return _hy_fused(hu, B, _fft_filter(taps, L), p)
    v, x1, x2, vb = _hy_conv(hu, B, p["hy_short_w"], p["hy_short_b"])
    t = _tiles(L)
    hb, tn = t["hb"], t["tcol"]
    fw_np, iv_np = _hyena_dft_np(L, hb)
    fwd, inv = _bf16_const(fw_np), _bf16_const(iv_np)
    g = _hy_gspec(fwd, taps, L, hb)
    y0 = _hy_fwd(fwd, vb, g, 0, hb, tn)
    z1, z1b = _hy_inv(inv, y0, x1, v, p["hy_bias"], 0, hb, tn, (F32, BF16))
    y1 = _hy_fwd(fwd, z1b, g, 1, hb, tn)
    (z2,) = _hy_inv(inv, y1, x2, z1, p["hy_bias"], 1, hb, tn, (BF16,))
    return z2


def _mix_ffn_kernel(ret_ref, four_ref, att_ref, hy_ref, wo_ref, x_ref, mod_ref, g_ref,
                    wg_ref, wu_ref, wd_ref, o_ref, hm_ref):
    mod = mod_ref[...]
    g1 = mod[:, 2 * D_MODEL:3 * D_MODEL]
    sh2 = mod[:, 3 * D_MODEL:4 * D_MODEL]
    sc2 = mod[:, 4 * D_MODEL:5 * D_MODEL]
    g2 = mod[:, 5 * D_MODEL:6 * D_MODEL]
    g = g_ref[...]

    def mix(rows):
        d = lambda a, i: jnp.dot(a[rows, :], wo_ref[i * 256:(i + 1) * 256, :], preferred_element_type=F32)
        m = d(ret_ref, 0) + d(four_ref, 1) + d(att_ref, 2) + d(hy_ref, 3)
        x1 = x_ref[rows, :] + g1 * _rms(m, g[1:2])
        o_ref[rows, :] = x1
        return (_rms(x1, g[2:3]) * (1.0 + sc2) + sh2).astype(BF16)

    def hidden(rows, h2):
        for c in range(D_FF // FFN_CHUNK):
            cs = slice(c * FFN_CHUNK, (c + 1) * FFN_CHUNK)
            a = jnp.dot(h2, wg_ref[:, cs], preferred_element_type=F32)
            u = jnp.dot(h2, wu_ref[:, cs], preferred_element_type=F32)
            hm_ref[rows, cs] = (_silu(a) * u).astype(BF16)

    def down(rows):
        f = jnp.dot(hm_ref[rows, :], wd_ref[...], preferred_element_type=F32)
        o_ref[rows, :] = o_ref[rows, :] + g2 * _rms(f, g[3:4])

    half = o_ref.shape[0] // 2
    ra, rb = slice(0, half), slice(half, 2 * half)
    hidden(ra, mix(ra))
    h2b = mix(rb)
    down(ra)
    hidden(rb, h2b)
    down(rb)


def _mix_ffn(ret, four, att, hy, x, mod, l, p, latent):
    B, L, _ = x.shape
    tm = _tiles(L)["tm_ffn"]
    row = (lambda b: b) if latent else (lambda b: CTX_ROW)
    bm = lambda w: pl.BlockSpec((None, tm, w), lambda b, i: (b, i, 0))
    tmj = pl.BlockSpec((tm, 256), lambda b, i: (i, b))
    resident = lambda shape: pl.BlockSpec(shape, lambda b, i: (0, 0), pipeline_mode=pl.Buffered(1))
    return pl.pallas_call(
        _mix_ffn_kernel,
        grid=(B, L // tm),
        in_specs=[bm(RET_W), tmj, bm(MLA_W), tmj,
                  resident((4 * 256, D_MODEL)),
                  bm(D_MODEL),
                  pl.BlockSpec((None, None, 1, 6 * D_MODEL), lambda b, i: (l, row(b), 0, 0)),
                  pl.BlockSpec((None, 4, D_MODEL), lambda b, i: (l, 0, 0)),
                  resident((D_MODEL, D_FF)), resident((D_MODEL, D_FF)), resident((D_FF, D_MODEL))],
        out_specs=bm(D_MODEL),
        out_shape=jax.ShapeDtypeStruct((B, L, D_MODEL), F32),
        scratch_shapes=[pltpu.VMEM((tm, D_FF), BF16)],
        compiler_params=_cparams(("parallel", "parallel")),
        name="mix_ffn",
    )(ret, four, att, hy, p["w_out"], x, mod, p["norm_g"], p["w_gate"], p["w_up"], p["w_down"])


def _layer(x, mod, l, p, latent, ctx_kv, s0):
    B, L, _ = x.shape
    t = _tiles(L)
    qkvg, fx, q, k, v, hu, ckv, krp = _in_proj(x, mod, l, p, latent)
    ret, s_new = _retention(qkvg, p["dec_rows"], s0, l, emit_state=not latent)
    if _use_fft(L):
        four = _fft_fourier(fx, L)
    else:
        four = _mm(_bf16_const(_pos_dft_np(L)), fx.reshape(2 * L, B * FNET_W),
                   (FNET_GC * L) ** -0.5, BF16, t["hb"], t["tcol"], "fourier_pos")
    kc, vc = ctx_kv if latent else (None, None)
    att = _attention(q, k, v, kc, vc)
    hy = _hyena(hu, B, p)
    return _mix_ffn(ret, four, att, hy, x, mod, l, p, latent), ckv, krp, s_new


def _prep_layer(l, w):
    zeros = lambda r, c: jnp.zeros((r, c), F32)
    w_in = w["w_in"][l]
    n0 = OFF_KR
    w_in_p = jnp.concatenate(
        [w_in[:, :n0], zeros(D_MODEL, MLA_ROPE_OFF), w_in[:, n0:n0 + MLA_ROPE],
         zeros(D_MODEL, MLA_HP - MLA_ROPE_OFF - MLA_ROPE), w_in[:, n0 + MLA_ROPE:]], axis=1)
    uq = w["mla_w_uq"][l].reshape(MLA_Q_LORA, MLA_HEADS, MLA_NOPE + MLA_ROPE)
    uq = jnp.pad(uq, ((0, 0), (0, 0), (0, MLA_HP - MLA_NOPE - MLA_ROPE))).reshape(MLA_Q_LORA, -1)
    ukv = w["mla_w_ukv"][l].reshape(MLA_KV_LORA, MLA_HEADS, MLA_NOPE + MLA_V)
    uk = jnp.pad(ukv[:, :, :MLA_NOPE], ((0, 0), (0, 0), (0, MLA_HP - MLA_NOPE))).reshape(MLA_KV_LORA, -1)
    uv = jnp.pad(ukv[:, :, MLA_NOPE:], ((0, 0), (0, 0), (0, MLA_HP - MLA_V))).reshape(MLA_KV_LORA, -1)
    return dict(
        g0=w["norm_g"][l, 0:1], norm_g=w["norm_g"],
        w_in=w_in_p.astype(BF16), w_out=w["w_out"][l].astype(BF16),
        q_norm=w["mla_q_norm"][l][None], kv_norm=w["mla_kv_norm"][l][None],
        w_uq=uq.astype(BF16), w_ukv=jnp.concatenate([uk, uv], axis=1).astype(BF16),
        dec_rows=w["dec_rows"],
        hy_short_w=w["hy_short_w"][l], hy_short_b=w["hy_short_b"][l][None],
        hy_w1=jnp.pad(w["hy_w1"][l], ((0, HY_EMB_PAD - HY_EMB), (0, 0))), hy_b1=w["hy_b1"][l][None],
        hy_w2=w["hy_w2"][l], hy_b2=w["hy_b2"][l][None], hy_w3=w["hy_w3"][l],
        hy_bias=w["hy_bias"][l],
        w_gate=w["w_gate"][l].astype(BF16), w_up=w["w_up"][l].astype(BF16),
        w_down=w["w_down"][l].astype(BF16),
    )


def kernel(x_prompt, x_sample, cache_ckv, cache_krope, state_ret, c, c_ctx, w_ada, b_ada, norm_g, w_in, w_out,
           ret_decay, mla_q_norm, mla_kv_norm, mla_w_uq, mla_w_ukv, hy_short_w, hy_short_b, hy_w1, hy_b1,
           hy_w2, hy_b2, hy_w3, hy_bias, w_gate, w_up, w_down):
    nb = c.shape[0]
    cvec = jnp.concatenate([c, c_ctx[None], jnp.zeros((MOD_ROWS - nb - 1, D_MODEL), F32)], axis=0)
    mod = _ada(cvec, w_ada, b_ada).reshape(DEPTH, MOD_ROWS, 1, 6 * D_MODEL)
    dec_rows = jnp.pad(jnp.repeat(ret_decay, RET_DV, axis=-1), ((0, 0), (0, 6), (0, 0)))
    krope_pad = jnp.pad(cache_krope, ((0, 0), (0, 0), (0, 0),
                                      (MLA_ROPE_OFF, MLA_HP - MLA_ROPE_OFF - MLA_ROPE)))
    w = dict(norm_g=norm_g, w_in=w_in, w_out=w_out, dec_rows=dec_rows, mla_q_norm=mla_q_norm,
             mla_kv_norm=mla_kv_norm, mla_w_uq=mla_w_uq, mla_w_ukv=mla_w_ukv, hy_short_w=hy_short_w,
             hy_short_b=hy_short_b, hy_w1=hy_w1, hy_b1=hy_b1, hy_w2=hy_w2, hy_b2=hy_b2, hy_w3=hy_w3,
             hy_bias=hy_bias, w_gate=w_gate, w_up=w_up, w_down=w_down)
    xp, xs = x_prompt, x_sample
    new_ckv, new_kr, new_s = [], [], []
    for l in range(DEPTH):
        p = _prep_layer(l, w)
        xp, ckv, krp, s_ret = _layer(xp, mod, l, p, False, None, None)
        new_ckv.append(ckv)
        new_kr.append(krp[:, :, MLA_ROPE_OFF:MLA_ROPE_OFF + MLA_ROPE])
        new_s.append(s_ret)
        ctx_kv = _ctx_kv(cache_ckv, krope_pad, l, p["w_ukv"])
        xs, _, _, _ = _layer(xs, mod, l, p, True, ctx_kv, state_ret)
    return (xp, xs, jnp.stack(new_ckv, axis=1), jnp.stack(new_kr, axis=1), jnp.stack(new_s, axis=1))
```

```python
import functools
import math

import numpy as np
import jax
import jax.numpy as jnp
from jax import lax
from jax.experimental import pallas as pl
from jax.experimental.pallas import tpu as pltpu

F32 = jnp.float32
BF16 = jnp.bfloat16

D_MODEL = 1024
DEPTH = 2
GRID_W = 64
EPS = 1e-6
ROPE_BASE = 10000.0
RET_HEADS = 4
RET_DK = 64
RET_DV = 64
RET_W = RET_HEADS * RET_DV
FNET_GROUPS = 4
FNET_GC = 64
FNET_W = FNET_GROUPS * FNET_GC
MLA_HEADS = 4
MLA_Q_LORA = 256
MLA_KV_LORA = 128
MLA_NOPE = 64
MLA_ROPE = 32
MLA_V = 64
MLA_W = MLA_HEADS * MLA_V
MLA_HP = 128
MLA_ROPE_OFF = MLA_NOPE
ATTN_CK = 256
SUBL = 8
FFT_TN = 256
FFT_S = 64
HY_CH = 256
HY_ORDER = 2
HY_BANDS = 16
HY_EMB = 1 + 2 * HY_BANDS
HY_EMB_PAD = 128
HY_FFN = 64
HY_FAST_DECAY = 0.3
HY_SLOW_DECAY = 1.5
HY_TARGET = 1e-2
D_FF = ((8 * D_MODEL + 3 * 256 - 1) // (3 * 256)) * 256
FFN_CHUNK = 256
MOD_ROWS = 16
CTX_ROW = 8

OFF_RET = 0
OFF_FU = 4 * RET_W
OFF_CQ = OFF_FU + FNET_W
OFF_CKV = OFF_CQ + MLA_Q_LORA
OFF_KR = OFF_CKV + MLA_KV_LORA
OFF_HU = OFF_KR + MLA_HP
IN_WP = OFF_HU + 3 * HY_CH

VMEM_LIMIT = 52 * 1024 * 1024


def _cparams(sem):
    return pltpu.CompilerParams(dimension_semantics=sem, vmem_limit_bytes=VMEM_LIMIT)


def _tiles(L):
    return dict(
        tm=min(L, 512),
        tm_ffn=min(L, 512),
        chunk=min(L, 256),
        tq=min(L, 512),
        hb=min(L, 512),
        tcol=1024,
    )


def _bf16_const(a):
    return jnp.asarray(np.asarray(a, np.float32)).astype(BF16)


@functools.lru_cache(maxsize=None)
def _rope_np(L, width, seg_off, seg_w, rope_dim):
    row = np.repeat(np.arange(L // GRID_W), GRID_W).astype(np.float64)
    col = np.tile(np.arange(GRID_W), L // GRID_W).astype(np.float64)
    quarter = rope_dim // 4
    inv = ROPE_BASE ** (-np.arange(quarter, dtype=np.float64) / quarter)
    cos = np.ones((L, width), np.float64)
    sin = np.zeros((L, width), np.float64)
    for lane in range(width):
        r = lane % seg_w - seg_off
        if r < 0 or r >= rope_dim:
            continue
        pos = row if r < rope_dim // 2 else col
        ang = pos * inv[r % quarter]
        first = (r % (rope_dim // 2)) < quarter
        cos[:, lane] = np.cos(ang)
        sin[:, lane] = -np.sin(ang) if first else np.sin(ang)
    return cos.astype(np.float32), sin.astype(np.float32)


@functools.lru_cache(maxsize=None)
def _chan_dft_np():
    k = np.arange(FNET_GC)
    ang = 2.0 * np.pi * np.outer(k, k) / FNET_GC
    c = np.kron(np.eye(FNET_GROUPS), np.cos(ang))
    s = np.kron(np.eye(FNET_GROUPS), np.sin(ang))
    return np.concatenate([c, s], axis=1)


@functools.lru_cache(maxsize=None)
def _pos_dft_np(L):
    k = np.arange(L, dtype=np.int64)
    ang = 2.0 * np.pi * (np.outer(k, k) % L) / L
    return np.concatenate([np.cos(ang), -np.sin(ang)], axis=1)


@functools.lru_cache(maxsize=None)
def _hyena_feat_np(L):
    pos = np.arange(L, dtype=np.float64)
    t = pos / L
    bands = np.arange(1, HY_BANDS + 1, dtype=np.float64)
    ang = (2.0 * math.pi / L) * pos[:, None] * bands[None, :]
    z = np.zeros((L, HY_EMB_PAD), np.float64)
    z[:, 0] = t
    z[:, 1:1 + HY_BANDS] = np.sin(ang)
    z[:, 1 + HY_BANDS:HY_EMB] = np.cos(ang)
    deltas = np.abs(np.linspace(math.log(HY_TARGET) / HY_SLOW_DECAY,
                                math.log(HY_TARGET) / HY_FAST_DECAY, HY_CH))
    window = np.exp(-t[:, None] * deltas[None, :])
    return z.astype(np.float32), window.astype(np.float32)


@functools.lru_cache(maxsize=None)
def _hy_fwd_tabs_np(L, S):
    R, Q, D = 2 * L // S, L // S, S // 2
    c, a = np.arange(R)[:, None], np.arange(Q)[None, :]
    ang = 2 * np.pi * ((c * a) % R) / R
    f1 = np.concatenate([np.cos(ang), -np.sin(ang)], axis=0)
    cc, d, b = np.arange(R)[:, None, None], np.arange(D)[None, :, None], np.arange(S)[None, None, :]
    th = np.pi * (((cc + R * d) * b) % (2 * L)) / L
    ur = np.concatenate([np.cos(th), np.sin(th)], axis=2)
    us = np.concatenate([np.sin(th), -np.cos(th)], axis=2)
    us[0, 0, :S] = (-1.0) ** np.arange(S)
    us[0, 0, S:] = 0.0
    return f1, np.concatenate([ur, us], axis=1)


@functools.lru_cache(maxsize=None)
def _hy_inv_tabs_np(L, S):
    R, Q, D = 2 * L // S, L // S, S // 2
    cc, f, d = np.arange(R)[:, None, None], np.arange(S)[None, :, None], np.arange(D)[None, None, :]
    k = cc + R * d
    be = np.pi * ((k * f) % (2 * L)) / L
    w = np.where(k == 0, 1.0, 2.0) / (2.0 * L) + 0.0 * be
    bc = np.concatenate([w * np.cos(be), w * np.sin(be)], axis=2)
    bs = np.concatenate([-w * np.sin(be), w * np.cos(be)], axis=2)
    bc[0, :, D] = (-1.0) ** np.arange(S) / (2.0 * L)
    bs[0, :, D] = 0.0
    e, c = np.arange(Q)[:, None], np.arange(R)[None, :]
    al = 2 * np.pi * ((e * c) % R) / R
    return np.concatenate([bc, bs], axis=1), np.concatenate([np.cos(al), np.sin(al)], axis=1)


@functools.lru_cache(maxsize=None)
def _four_tabs_np(L, S):
    R = L // S
    c, a = np.arange(R)[:, None], np.arange(R)[None, :]
    ang = 2 * np.pi * ((c * a) % R) / R
    f1 = np.concatenate([np.concatenate([np.cos(ang), -np.sin(ang)], axis=1),
                         np.concatenate([-np.sin(ang), -np.cos(ang)], axis=1)], axis=0)
    cc, d, b = np.arange(R)[:, None, None], np.arange(S)[None, :, None], np.arange(S)[None, None, :]
    th = 2 * np.pi * (((cc + R * d) * b) % L) / L
    return f1, np.concatenate([np.cos(th), np.sin(th)], axis=2)


def _rms(x, g):
    return x * lax.rsqrt(jnp.mean(x * x, axis=-1, keepdims=True) + EPS) * g


def _silu(x):
    return x * jax.nn.sigmoid(x)


def _bdot(a, b):
    return jnp.dot(a.astype(BF16), b.astype(BF16), preferred_element_type=F32)


def _split(a):
    hi = a.astype(BF16)
    lo = (a - hi.astype(F32)).astype(BF16)
    return hi, lo


def _dot3(a, b):
    ah, al = _split(a)
    bh, bl = _split(b)
    d = lambda x, y: jnp.dot(x, y, preferred_element_type=F32)
    return d(ah, bh) + (d(al, bh) + d(ah, bl))


def _store_kv(kv, kr_pad, k_ref, v_ref):
    wk = MLA_HEADS * MLA_HP
    k_ref[...] = (kv[:, :wk] + jnp.concatenate([kr_pad] * MLA_HEADS, axis=1)).astype(k_ref.dtype)
    lane = lax.broadcasted_iota(jnp.int32, (1, wk), 1)
    ones = jnp.where(lane % MLA_HP >= MLA_V, 1.0, 0.0)
    v_ref[...] = (kv[:, wk:] + ones).astype(v_ref.dtype)


def _rope(x, cos, sin_signed, quarter):
    w = x.shape[-1]
    lane = lax.broadcasted_iota(jnp.int32, x.shape, 1)
    first = (lane % (2 * quarter)) < quarter
    up = pltpu.roll(x, w - quarter, axis=1)
    dn = pltpu.roll(x, quarter, axis=1)
    return x * cos + jnp.where(first, up, dn) * sin_signed


def _ada_kernel(c_ref, w_ref, b_ref, o_ref):
    s = _silu(c_ref[...])
    o_ref[...] = _bdot(s, w_ref[...]) + b_ref[...]


def _ada(cvec, w_ada, b_ada):
    tn = 1024
    n6 = w_ada.shape[-1]
    return pl.pallas_call(
        _ada_kernel,
        grid=(DEPTH, n6 // tn),
        in_specs=[
            pl.BlockSpec((MOD_ROWS, D_MODEL), lambda l, j: (0, 0)),
            pl.BlockSpec((None, D_MODEL, tn), lambda l, j: (l, 0, j)),
            pl.BlockSpec((None, 1, tn), lambda l, j: (l, 0, j)),
        ],
        out_specs=pl.BlockSpec((None, MOD_ROWS, tn), lambda l, j: (l, 0, j)),
        out_shape=jax.ShapeDtypeStruct((DEPTH, MOD_ROWS, n6), F32),
        compiler_params=_cparams(("parallel", "parallel")),
        name="ada",
    )(cvec, w_ada, b_ada.reshape(DEPTH, 1, n6))


def _in_proj_kernel(*refs, latent):
    (x_ref, mod_ref, g0_ref, w_ref, qn_ref, kvn_ref, wuq_ref, wukv_ref, bd_ref) = refs[:9]
    pos = 9
    if latent:
        cr_ref, sr_ref, cm_ref, sm_ref = refs[pos:pos + 4]
        pos += 4
        qkvg_ref, fx_ref, q_ref, k_ref, v_ref, hu_ref = refs[pos:pos + 6]
        lanes = lambda ref, reps: jnp.concatenate([ref[...]] * reps, axis=1)
    else:
        qkvg_ref, fx_ref, q_ref, k_ref, v_ref, hu_ref, ckv_ref, kr_ref = refs[pos:pos + 8]

    x = x_ref[...]
    mod = mod_ref[...]
    sh1 = mod[:, 0:D_MODEL]
    sc1 = mod[:, D_MODEL:2 * D_MODEL]
    h = _rms(x, g0_ref[0:1, :]) * (1.0 + sc1) + sh1
    proj = jnp.dot(h.astype(BF16), w_ref[...], preferred_element_type=F32)

    rq = proj[:, OFF_RET:OFF_RET + RET_W]
    rk = proj[:, OFF_RET + RET_W:OFF_RET + 2 * RET_W] * (RET_DK ** -0.5)
    if latent:
        cos_r, sin_r = lanes(cr_ref, RET_W // 128), lanes(sr_ref, RET_W // 128)
        rq = _rope(rq, cos_r, sin_r, RET_DK // 4)
        rk = _rope(rk, cos_r, sin_r, RET_DK // 4)
    qkvg_ref[:, 0:RET_W] = rq
    qkvg_ref[:, RET_W:2 * RET_W] = rk
    qkvg_ref[:, 2 * RET_W:4 * RET_W] = proj[:, OFF_RET + 2 * RET_W:OFF_RET + 4 * RET_W]

    fcs = _bdot(proj[:, OFF_FU:OFF_FU + FNET_W], bd_ref[...])
    fx_ref[0] = fcs[:, :FNET_W].astype(fx_ref.dtype)
    fx_ref[1] = fcs[:, FNET_W:].astype(fx_ref.dtype)

    cqn = _rms(proj[:, OFF_CQ:OFF_CQ + MLA_Q_LORA], qn_ref[...])
    q = _bdot(cqn, wuq_ref[...])
    ckvn = _rms(proj[:, OFF_CKV:OFF_CKV + MLA_KV_LORA], kvn_ref[...])
    kv = _bdot(ckvn, wukv_ref[...])
    krp = proj[:, OFF_KR:OFF_KR + MLA_HP]
    if latent:
        q = _rope(q, lanes(cm_ref, MLA_HEADS), lanes(sm_ref, MLA_HEADS), MLA_ROPE // 4)
        krp = _rope(krp, cm_ref[...], sm_ref[...], MLA_ROPE // 4)
    else:
        ckv_ref[...] = ckvn
        kr_ref[...] = krp
    q_ref[...] = (q * ((MLA_NOPE + MLA_ROPE) ** -0.5 * math.log2(math.e))).astype(q_ref.dtype)
    _store_kv(kv, krp, k_ref, v_ref)

    hu_ref[...] = proj[:, OFF_HU:OFF_HU + 3 * HY_CH]


def _in_proj(x, mod, l, p, latent):
    B, L, _ = x.shape
    t = _tiles(L)
    tm = t["tm"]
    nt = L // tm
    row = (lambda b: b) if latent else (lambda b: CTX_ROW)
    full = lambda shape: pl.BlockSpec((None,) + shape, lambda i, b: (l,) + (0,) * len(shape),
                                      pipeline_mode=pl.Buffered(1))
    in_specs = [
        pl.BlockSpec((None, tm, D_MODEL), lambda i, b: (b, i, 0)),
        pl.BlockSpec((None, None, 1, 6 * D_MODEL), lambda i, b: (l, row(b), 0, 0)),
        full((4, D_MODEL)),
        full((D_MODEL, IN_WP)),
        full((1, MLA_Q_LORA)),
        full((1, MLA_KV_LORA)),
        full((MLA_Q_LORA, MLA_HEADS * MLA_HP)),
        full((MLA_KV_LORA, 2 * MLA_HEADS * MLA_HP)),
        pl.BlockSpec((FNET_W, 2 * FNET_W), lambda i, b: (0, 0), pipeline_mode=pl.Buffered(1)),
    ]
    args = [x, mod, p["norm_g"], p["w_in"], p["q_norm"], p["kv_norm"], p["w_uq"], p["w_ukv"],
            _bf16_const(_chan_dft_np())]
    if latent:
        tabs = (_rope_np(L, 128, 0, RET_DK, RET_DK)
                + _rope_np(L, MLA_HP, MLA_ROPE_OFF, MLA_HP, MLA_ROPE))
        for tab in tabs:
            in_specs.append(pl.BlockSpec((tm, tab.shape[1]), lambda i, b: (i, 0)))
            args.append(jnp.asarray(tab))
    bm = lambda w: pl.BlockSpec((None, tm, w), lambda i, b: (b, i, 0))
    sds = jax.ShapeDtypeStruct
    out_specs = [
        bm(4 * RET_W),
        pl.BlockSpec((2, tm, FNET_W), lambda i, b: (0, i, b)),
        bm(MLA_HEADS * MLA_HP), bm(MLA_HEADS * MLA_HP), bm(MLA_HEADS * MLA_HP),
        pl.BlockSpec((tm, 3 * HY_CH), lambda i, b: (i, b)),
    ]
    out_shape = [
        sds((B, L, 4 * RET_W), F32),
        sds((2, L, B * FNET_W), F32 if _use_fft(L) else BF16),
        sds((B, L, MLA_HEADS * MLA_HP), BF16),
        sds((B, L, MLA_HEADS * MLA_HP), BF16),
        sds((B, L, MLA_HEADS * MLA_HP), BF16),
        sds((L, B * 3 * HY_CH), F32),
    ]
    if not latent:
        out_specs += [bm(MLA_KV_LORA), bm(MLA_HP)]
        out_shape += [sds((B, L, MLA_KV_LORA), F32), sds((B, L, MLA_HP), F32)]
    outs = pl.pallas_call(
        functools.partial(_in_proj_kernel, latent=latent),
        grid=(nt, B), in_specs=in_specs, out_specs=out_specs, out_shape=out_shape,
        compiler_params=_cparams(("parallel", "parallel")),
        name="in_proj_lat" if latent else "in_proj_ctx",
    )(*args)
    return tuple(outs) + ((None, None) if latent else ())


def _ctx_kv_kernel(ckv_ref, kr_ref, w_ref, k_ref, v_ref):
    _store_kv(_bdot(ckv_ref[...], w_ref[...]), kr_ref[...], k_ref, v_ref)


def _ctx_kv(cache_ckv, krope_pad, l, w_ukv):
    B, _, P, _ = cache_ckv.shape
    return pl.pallas_call(
        _ctx_kv_kernel,
        grid=(B,),
        in_specs=[
            pl.BlockSpec((None, None, P, MLA_KV_LORA), lambda b: (b, l, 0, 0)),
            pl.BlockSpec((None, None, P, MLA_HP), lambda b: (b, l, 0, 0)),
            pl.BlockSpec((None,) + w_ukv.shape[1:], lambda b: (l, 0, 0)),
        ],
        out_specs=[
            pl.BlockSpec((None, P, MLA_HEADS * MLA_HP), lambda b: (b, 0, 0)),
            pl.BlockSpec((None, P, MLA_HEADS * MLA_HP), lambda b: (b, 0, 0)),
        ],
        out_shape=[
            jax.ShapeDtypeStruct((B, P, MLA_HEADS * MLA_HP), BF16),
            jax.ShapeDtypeStruct((B, P, MLA_HEADS * MLA_HP), BF16),
        ],
        compiler_params=_cparams(("parallel",)),
        name="ctx_kv",
    )(cache_ckv, krope_pad, w_ukv)


def _ret_kernel(*refs, L, C, has_s0, emit_state):
    q_ref, k_ref, v_ref, g_ref, dec_ref = refs[:5]
    pos = 5
    if has_s0:
        s0_ref = refs[pos]
        pos += 1
    o_ref = refs[pos]
    pos += 1
    if emit_state:
        sn_ref = refs[pos]
        pos += 1
    sf_ref, sb_ref, dm_ref, tab_ref = refs[pos:pos + 4]
    n = L // C
    H, W = RET_HEADS, RET_W

    d = dec_ref[0:2, :]
    lg = jnp.minimum(d, 0.0) - jnp.log1p(jnp.exp(-jnp.abs(d)))
    lf, lb = lg[0:1, :], lg[1:2, :]
    @pl.when(pl.program_id(0) == 0)
    def _():
        ii = lax.broadcasted_iota(jnp.int32, (C, C), 0)
        jj = lax.broadcasted_iota(jnp.int32, (C, C), 1)
        diff = (ii - jj).astype(F32)
        for h in range(H):
            lfh = lf[:, h * RET_DV:h * RET_DV + 1]
            lbh = lb[:, h * RET_DV:h * RET_DV + 1]
            dm_ref[h] = jnp.where(diff > 0, jnp.exp(diff * lfh),
                                  jnp.where(diff < 0, jnp.exp(-diff * lbh), 2.0))
        ci = lax.broadcasted_iota(jnp.int32, (C, W), 0).astype(F32)
        tab_ref[0] = jnp.exp((ci + 1.0) * lf)
        tab_ref[1] = jnp.exp((C - ci) * lb)
        tab_ref[2] = jnp.exp((C - 1.0 - ci) * lf)
        tab_ref[3] = jnp.exp(ci * lb)

    g_f = jnp.exp(C * lf)
    g_b = jnp.exp(C * lb)
    hr = lax.broadcasted_iota(jnp.int32, (W, W), 0) // RET_DK
    hc = lax.broadcasted_iota(jnp.int32, (W, W), 1) // RET_DV
    same_head = hr == hc
    avg = jnp.where(same_head, 1.0 / RET_DV, 0.0).astype(BF16)

    sf_ref[0] = jnp.zeros((W, W), F32)
    sb_ref[n] = jnp.zeros((W, W), F32)
    if has_s0:
        for h in range(H):
            blk = slice(h * RET_DK, (h + 1) * RET_DK)
            sf_ref[0, blk, blk] = s0_ref[0, h]
            sb_ref[n, blk, blk] = s0_ref[1, h]

    tdot = lambda a, b: lax.dot_general(a.astype(BF16), b.astype(BF16), (((0,), (0,)), ((), ())),
                                        preferred_element_type=F32)

    def states(j, carry):
        jb = n - 1 - j
        rf = pl.multiple_of(j * C, C)
        rb = pl.multiple_of(jb * C, C)
        upd = tdot(k_ref[pl.ds(rf, C), :] * tab_ref[2], v_ref[pl.ds(rf, C), :])
        sf_ref[j + 1] = g_f * sf_ref[j] + jnp.where(same_head, upd, 0.0)
        upd = tdot(k_ref[pl.ds(rb, C), :] * tab_ref[3], v_ref[pl.ds(rb, C), :])
        sb_ref[jb] = g_b * sb_ref[jb + 1] + jnp.where(same_head, upd, 0.0)
        return carry

    lax.fori_loop(0, n, states, 0)

    lane_head = lax.broadcasted_iota(jnp.int32, (C, W), 1) // RET_DV

    def group_mean(x):
        hi, lo = _split(x)
        return (jnp.dot(hi, avg, preferred_element_type=F32)
                + jnp.dot(lo, avg, preferred_element_type=F32))

    def chunk(j, carry):
        r0 = pl.multiple_of(j * C, C)
        q = q_ref[pl.ds(r0, C), :]
        qb = q.astype(BF16)
        kb = k_ref[pl.ds(r0, C), :].astype(BF16)
        vb = v_ref[pl.ds(r0, C), :].astype(BF16)
        o = tab_ref[0] * jnp.dot(qb, sf_ref[j].astype(BF16), preferred_element_type=F32)
        o = o + tab_ref[1] * jnp.dot(qb, sb_ref[j + 1].astype(BF16), preferred_element_type=F32)
        inner = jnp.zeros((C, W), F32)
        for h in range(H):
            in_head = lane_head == h
            qh = jnp.where(in_head, q, 0.0).astype(BF16)
            s = lax.dot_general(qh, kb, (((1,), (1,)), ((), ())), preferred_element_type=F32)
            oh = jnp.dot((s * dm_ref[h]).astype(BF16), vb, preferred_element_type=F32)
            inner = jnp.where(in_head, oh, inner)
        o = o + inner
        dlt = o - group_mean(o)
        var = group_mean(dlt * dlt)
        gate = g_ref[pl.ds(r0, C), :]
        o_ref[pl.ds(r0, C), :] = (_silu(gate) * (dlt * lax.rsqrt(var + EPS))).astype(o_ref.dtype)
        return carry

    lax.fori_loop(0, n, chunk, 0, unroll=4 if n % 4 == 0 else 1)

    if emit_state:
        for h in range(H):
            blk = slice(h * RET_DK, (h + 1) * RET_DK)
            sn_ref[0, h] = sf_ref[n, blk, blk]
            sn_ref[1, h] = sb_ref[0, blk, blk]


def _retention(qkvg, dec_rows, s0, l, emit_state):
    B, L, _ = qkvg.shape
    C = _tiles(L)["chunk"]
    n = L // C
    has_s0 = s0 is not None
    def colspec(j):
        return pl.BlockSpec((None, L, RET_W), lambda b: (b, 0, j))

    in_specs = [colspec(j) for j in range(4)]
    in_specs.append(pl.BlockSpec((None, 8, RET_W), lambda b: (l, 0, 0)))
    args = [qkvg, qkvg, qkvg, qkvg, dec_rows]
    st_block = (None, None, 2, RET_HEADS, RET_DK, RET_DV)
    if has_s0:
        in_specs.append(pl.BlockSpec(st_block, lambda b: (b, l, 0, 0, 0, 0)))
        args.append(s0)
    out_specs = [pl.BlockSpec((None, L, RET_W), lambda b: (b, 0, 0))]
    out_shape = [jax.ShapeDtypeStruct((B, L, RET_W), BF16)]
    if emit_state:
        out_specs.append(pl.BlockSpec((None, 2, RET_HEADS, RET_DK, RET_DV), lambda b: (b, 0, 0, 0, 0)))
        out_shape.append(jax.ShapeDtypeStruct((B, 2, RET_HEADS, RET_DK, RET_DV), F32))
    outs = pl.pallas_call(
        functools.partial(_ret_kernel, L=L, C=C, has_s0=has_s0, emit_state=emit_state),
        grid=(B,), in_specs=in_specs, out_specs=out_specs, out_shape=out_shape,
        scratch_shapes=[
            pltpu.VMEM((n + 1, RET_W, RET_W), F32),
            pltpu.VMEM((n + 1, RET_W, RET_W), F32),
            pltpu.VMEM((RET_HEADS, C, C), F32),
            pltpu.VMEM((4, C, RET_W), F32),
        ],
        compiler_params=_cparams(("arbitrary",)),
        name="retention_lat" if has_s0 else "retention_ctx",
    )(*args)
    return outs if emit_state else (outs[0], None)


def _mm_kernel(a_ref, b_ref, o_ref, *, scale):
    o_ref[...] = (jnp.dot(a_ref[...], b_ref[...], preferred_element_type=F32) * scale).astype(o_ref.dtype)


def _mm(a, b, scale, out_dtype, tm, tn, name):
    M, K = a.shape
    _, N = b.shape
    tm, tn = min(tm, M), min(tn, N)
    return pl.pallas_call(
        functools.partial(_mm_kernel, scale=scale),
        grid=(N // tn, M // tm),
        in_specs=[pl.BlockSpec((tm, K), lambda j, i: (i, 0)),
                  pl.BlockSpec((K, tn), lambda j, i: (0, j))],
        out_specs=pl.BlockSpec((tm, tn), lambda j, i: (i, j)),
        out_shape=jax.ShapeDtypeStruct((M, N), out_dtype),
        compiler_params=_cparams(("parallel", "parallel")),
        name=name,
    )(a, b)


def _attn_kernel(*refs, has_ctx):
    if has_ctx:
        q_ref, k_ref, v_ref, kc_ref, vc_ref, o_ref, s_ref = refs
        srcs = ((k_ref, v_ref), (kc_ref, vc_ref))
    else:
        q_ref, k_ref, v_ref, o_ref, s_ref = refs
        srcs = ((k_ref, v_ref),)
    nt = lambda a, b: lax.dot_general(a, b, (((1,), (1,)), ((), ())), preferred_element_type=F32)
    chunks = [(kr, vr, c0) for kr, vr in srcs for c0 in range(0, kr.shape[0], ATTN_CK)]
    def scores(h):
        hq = slice(h * MLA_HP, (h + 1) * MLA_HP)
        qh = q_ref[:, hq]
        mx = None
        for n, (kr, _, c0) in enumerate(chunks):
            s = nt(qh, kr[c0:c0 + ATTN_CK, hq])
            s_ref[h % 2, :, n * ATTN_CK:(n + 1) * ATTN_CK] = s
            for g in range(ATTN_CK // 128):
                part = s[:, g * 128:(g + 1) * 128]
                mx = part if mx is None else jnp.maximum(mx, part)
        return jnp.max(mx, axis=-1, keepdims=True)

    def weighted_values(h, m):
        hq = slice(h * MLA_HP, (h + 1) * MLA_HP)
        acc = None
        for n, (_, vr, c0) in enumerate(chunks):
            p = jnp.exp2(s_ref[h % 2, :, n * ATTN_CK:(n + 1) * ATTN_CK] - m).astype(BF16)
            d = jnp.dot(p, vr[c0:c0 + ATTN_CK, hq], preferred_element_type=F32)
            acc = d if acc is None else acc + d
        o_ref[:, h * MLA_V:(h + 1) * MLA_V] = (acc[:, :MLA_V] / acc[:, MLA_V:2 * MLA_V]).astype(o_ref.dtype)

    m = scores(0)
    for h in range(MLA_HEADS):
        m_next = scores(h + 1) if h + 1 < MLA_HEADS else None
        weighted_values(h, m)
        m = m_next


def _attention(q, k, v, kc, vc):
    B, L, _ = q.shape
    tq = _tiles(L)["tq"]
    has_ctx = kc is not None
    whole = lambda a: pl.BlockSpec((None,) + a.shape[1:], lambda b, i: (b, 0, 0))
    in_specs = [pl.BlockSpec((None, tq, q.shape[2]), lambda b, i: (b, i, 0)), whole(k), whole(v)]
    args = [q, k, v]
    n_keys = L
    if has_ctx:
        in_specs += [whole(kc), whole(vc)]
        args += [kc, vc]
        n_keys += kc.shape[1]
    return pl.pallas_call(
        functools.partial(_attn_kernel, has_ctx=has_ctx),
        grid=(B, L // tq), in_specs=in_specs,
        out_specs=pl.BlockSpec((None, tq, MLA_W), lambda b, i: (b, i, 0)),
        out_shape=jax.ShapeDtypeStruct((B, L, MLA_W), BF16),
        scratch_shapes=[pltpu.VMEM((2, tq, n_keys), F32)],
        compiler_params=_cparams(("parallel", "parallel")),
        name="attention_lat" if has_ctx else "attention_ctx",
    )(*args)


def _hy_filter_kernel(z_ref, w1_ref, b1_ref, w2_ref, b2_ref, w3_ref, win_ref, x_ref):
    h1 = jnp.sin(_dot3(z_ref[...], w1_ref[...]) + b1_ref[...])
    h2 = jnp.sin(_dot3(h1, w2_ref[...]) + b2_ref[...])
    w3 = w3_ref[...]
    h = jnp.concatenate([_dot3(h2[:, :HY_FFN], w3), _dot3(h2[:, HY_FFN:], w3)], axis=0)
    win = win_ref[...]
    row = lax.broadcasted_iota(jnp.int32, win.shape, 0)
    for o in range(HY_ORDER):
        base = o * 2 * HY_CH
        hf = h[:, base:base + HY_CH] * win
        hb = h[:, base + HY_CH:base + 2 * HY_CH] * win
        hb = jnp.where(row == 0, 0.0, hb)
        nrm = jnp.sum(jnp.abs(hf), axis=0, keepdims=True) + jnp.sum(jnp.abs(hb), axis=0, keepdims=True) + EPS
        inv = 1.0 / nrm
        x_ref[:, base:base + HY_CH] = (hf + hb) * inv
        x_ref[:, base + HY_CH:base + 2 * HY_CH] = (hf - hb) * inv


def _hy_filter(L, l, p):
    z, win = _hyena_feat_np(L)
    z2 = np.concatenate([z[:L // 2], z[L // 2:]], axis=1)
    full = lambda shape: pl.BlockSpec(shape, lambda i: (0,) * len(shape))
    layer = lambda shape: pl.BlockSpec((None,) + shape, lambda i: (l,) + (0,) * len(shape))
    return pl.pallas_call(
        _hy_filter_kernel,
        grid=(1,),
        in_specs=[full((L // 2, 2 * HY_EMB_PAD)), layer((2 * HY_EMB_PAD, 2 * HY_FFN)), layer((1, 2 * HY_FFN)),
                  layer((2 * HY_FFN, 2 * HY_FFN)), layer((1, 2 * HY_FFN)),
                  layer((HY_FFN, HY_ORDER * 2 * HY_CH)), full((L, HY_CH))],
        out_specs=full((L, HY_ORDER * 2 * HY_CH)),
        out_shape=jax.ShapeDtypeStruct((L, HY_ORDER * 2 * HY_CH), F32),
        compiler_params=_cparams(("arbitrary",)),
        name="hy_filter",
    )(jnp.asarray(z2), p["hy_w1"], p["hy_b1"], p["hy_w2"], p["hy_b2"], p["hy_w3"], jnp.asarray(win))


def _tile_rows(ref, lead, rows):
    return jnp.concatenate([ref[lead, r * SUBL:(r + 1) * SUBL, :] for r in rows], axis=0)


def _fft_fwd_kernel(*refs, Q, R, S, filt):
    if filt:
        u_ref, k1_ref, e_ref, y_ref, a_scr = refs
    else:
        u_ref, k1_ref, e_ref, ga_ref, gb_ref, ny_ref, y_ref, a_scr = refs
    D = S // 2
    nb = S // SUBL
    tn = u_ref.shape[-1]
    dot = lambda p, q: jnp.dot(p, q, preferred_element_type=F32)
    k1 = k1_ref[...]
    for bh in range(nb):
        x = u_ref[:, bh].reshape(Q * SUBL, tn)
        if filt:
            xh, xl = _split(x)
            a_scr[bh] = dot(k1, xh) + dot(k1, xl)
        else:
            a_scr[bh] = dot(k1, x.astype(BF16))
    for c in range(R):
        rhs = jnp.concatenate([_tile_rows(a_scr, bh, (part * R + c,)) for part in (0, 1) for bh in range(nb)],
                              axis=0)
        if filt:
            rh, rl = _split(rhs)
            y_ref[c] = dot(e_ref[c], rh) + dot(e_ref[c], rl)
            continue
        y = dot(e_ref[c], rhs.astype(BF16))
        ur, us = y[:D], y[D:]
        ga, gb = ga_ref[c], gb_ref[c]
        gd = ga
        if c == 0:
            row0 = lax.broadcasted_iota(jnp.int32, ga.shape, 0) == 0
            gb = jnp.where(row0, 0.0, gb)
            gd = jnp.where(row0, jnp.broadcast_to(ny_ref[0, 0:1, :], ga.shape), ga)
        y_ref[c, :D, :] = (ur * ga - us * gb).astype(y_ref.dtype)
        y_ref[c, D:, :] = (ur * gb + us * gd).astype(y_ref.dtype)


def _fft_filter(taps, L):
    S = FFT_S
    R, Q = 2 * L // S, L // S
    N = taps.shape[1]
    tn = FFT_TN
    nb = S // SUBL
    f1_np, e_np = _hy_fwd_tabs_np(L, S)
    const = lambda shape: pl.BlockSpec(shape, lambda j: (0,) * len(shape), pipeline_mode=pl.Buffered(1))
    return pl.pallas_call(
        functools.partial(_fft_fwd_kernel, Q=Q, R=R, S=S, filt=True),
        grid=(N // tn,),
        in_specs=[pl.BlockSpec((Q, nb, SUBL, tn), lambda j: (0, 0, 0, j)),
                  const((2 * R * SUBL, Q * SUBL)), const((R, S, 2 * S))],
        out_specs=pl.BlockSpec((R, S, tn), lambda j: (0, 0, j)),
        out_shape=jax.ShapeDtypeStruct((R, S, N), F32),
        scratch_shapes=[pltpu.VMEM((nb, 2 * R * SUBL, tn), F32)],
        compiler_params=_cparams(("parallel",)),
        name="fft_filter",
    )(taps.reshape(Q, nb, SUBL, N), _bf16_const(np.kron(f1_np, np.eye(SUBL))), _bf16_const(e_np))


def _fft_inv_kernel(y_ref, e_ref, k2_ref, gate_ref, z_ref, d_ref, o_ref, scr, *, Q, R, S):
    dot = lambda p, q: jnp.dot(p, q, preferred_element_type=F32)
    tn = y_ref.shape[-1]
    per = scr.shape[1] // (2 * S)
    for c in range(R):
        scr[c // per, (c % per) * 2 * S:(c % per + 1) * 2 * S, :] = dot(e_ref[c], y_ref[c])
    k2 = k2_ref[...]
    d = d_ref[...]
    for fh in range(S // SUBL):
        rhs = jnp.concatenate([_tile_rows(scr, c // per, ((c % per) * 2 * S // SUBL + part * (S // SUBL) + fh,))
                               for part in (0, 1) for c in range(R)], axis=0).astype(BF16)
        conv = dot(k2, rhs).reshape(Q, SUBL, tn)
        o_ref[:, fh] = (gate_ref[:, fh] * (conv + d * z_ref[:, fh])).astype(o_ref.dtype)


def _hy_fused_kernel(hu_ref, w_ref, b_ref, k1_ref, e_ref, ei_ref, k2_ref, ga0, gb0, ny0, ga1, gb1, ny1,
                     bias_ref, o_ref, v4, x14, x24, z14, y_scr, scr, *, Q, R, S):
    L, tn = hu_ref.shape[0], HY_CH
    w, b = w_ref[...], b_ref[...]
    row = lax.broadcasted_iota(jnp.int32, (L, tn), 0)
    for g, dst in enumerate((v4, x14, x24)):
        cs = slice(g * tn, (g + 1) * tn)
        u = hu_ref[:, cs]
        prev = jnp.where(row == 0, 0.0, pltpu.roll(u, 1, axis=0))
        nxt = jnp.where(row == L - 1, 0.0, pltpu.roll(u, L - 1, axis=0))
        uc = prev * w[0:1, cs] + u * w[1:2, cs] + nxt * w[2:3, cs] + b[:, cs]
        dst[...] = uc.reshape(dst.shape)
    fwd = functools.partial(_fft_fwd_kernel, Q=Q, R=R, S=S, filt=False)
    inv = functools.partial(_fft_inv_kernel, Q=Q, R=R, S=S)
    fwd(v4, k1_ref, e_ref, ga0, gb0, ny0, y_scr, scr)
    inv(y_scr, ei_ref, k2_ref, x14, v4, bias_ref.at[0:1], z14, scr)
    fwd(z14, k1_ref, e_ref, ga1, gb1, ny1, y_scr, scr)
    inv(y_scr, ei_ref, k2_ref, x24, z14, bias_ref.at[1:2], o_ref, scr)


def _hy_fused(hu, B, spec, l, p):
    layer = lambda shape: pl.BlockSpec((None,) + shape, lambda j: (l,) + (0,) * len(shape),
                                       pipeline_mode=pl.Buffered(1))
    L = hu.shape[0]
    S = FFT_S
    R, Q, D = 2 * L // S, L // S, S // 2
    nb = S // SUBL
    tn = HY_CH
    f1_np, e_np = _hy_fwd_tabs_np(L, S)
    ei_np, f2_np = _hy_inv_tabs_np(L, S)
    const = lambda shape: pl.BlockSpec(shape, lambda j: (0,) * len(shape), pipeline_mode=pl.Buffered(1))
    one = lambda blk, idx: pl.BlockSpec(blk, lambda j: idx, pipeline_mode=pl.Buffered(1))
    g_specs = [s for o in range(HY_ORDER) for s in (one((R, D, tn), (0, 0, 2 * o)),
                                                    one((R, D, tn), (0, 1, 2 * o + 1)),
                                                    one((1, 8, tn), (0, D // 8, 2 * o)))]
    work = pltpu.VMEM((Q, nb, SUBL, tn), F32)
    out = pl.pallas_call(
        functools.partial(_hy_fused_kernel, Q=Q, R=R, S=S),
        grid=(B,),
        in_specs=[pl.BlockSpec((L, 3 * HY_CH), lambda j: (0, j)), layer((3, 3 * HY_CH)), layer((1, 3 * HY_CH)),
                  const((2 * R * SUBL, Q * SUBL)), const((R, S, 2 * S)), const((R, 2 * S, S)),
                  const((Q * SUBL, 2 * R * SUBL))] + g_specs + [layer((HY_ORDER, HY_CH))],
        out_specs=pl.BlockSpec((Q, nb, SUBL, tn), lambda j: (0, 0, 0, j)),
        out_shape=jax.ShapeDtypeStruct((Q, nb, SUBL, B * HY_CH), BF16),
        scratch_shapes=[work, work, work, work, pltpu.VMEM((R, S, tn), BF16),
                        pltpu.VMEM((nb, 2 * R * SUBL, tn), F32)],
        compiler_params=_cparams(("parallel",)),
        name="hy_fused",
    )(hu, p["hy_short_w"], p["hy_short_b"], _bf16_const(np.kron(f1_np, np.eye(SUBL))), _bf16_const(e_np),
      _bf16_const(ei_np), _bf16_const(np.kron(f2_np, np.eye(SUBL))), *([spec] * 6), p["hy_bias"])
    return out.reshape(L, B * HY_CH)


def _fft_four_kernel(x_ref, k1_ref, e2_ref, o_ref, a_scr, *, R, S, scale):
    dot = lambda p, q: jnp.dot(p, q, preferred_element_type=F32)
    tn = x_ref.shape[-1]
    nb = S // SUBL
    k1 = k1_ref[...]
    for bh in range(nb):
        x = jnp.concatenate([x_ref[part, :, bh].reshape(R * SUBL, tn) for part in (0, 1)], axis=0)
        a_scr[bh] = dot(k1, x.astype(BF16))
    for ch in range(R // SUBL):
        rhs = jnp.concatenate([_tile_rows(a_scr, bh, (part * R + ch * SUBL + cl,))
                               for cl in range(SUBL) for part in (0, 1) for bh in range(nb)],
                              axis=0).astype(BF16)
        y = dot(e2_ref[ch], rhs)
        o_ref[:, ch] = (y.reshape(S, SUBL, tn) * scale).astype(o_ref.dtype)


def _fft_fourier(fx, L):
    S = FFT_S
    R = L // S
    N = fx.shape[2]
    tn = FFT_TN
    nb = S // SUBL
    f1_np, e_np = _four_tabs_np(L, S)
    e2 = np.zeros((R // SUBL, S, SUBL, SUBL, 2 * S))
    for cl in range(SUBL):
        e2[:, :, cl, cl, :] = e_np[cl::SUBL]
    e2 = e2.reshape(R // SUBL, S * SUBL, SUBL * 2 * S)
    const = lambda shape: pl.BlockSpec(shape, lambda j: (0,) * len(shape), pipeline_mode=pl.Buffered(1))
    out = pl.pallas_call(
        functools.partial(_fft_four_kernel, R=R, S=S, scale=(FNET_GC * L) ** -0.5),
        grid=(N // tn,),
        in_specs=[pl.BlockSpec((2, R, nb, SUBL, tn), lambda j: (0, 0, 0, 0, j)),
                  const((2 * R * SUBL, 2 * R * SUBL)), const(e2.shape)],
        out_specs=pl.BlockSpec((S, R // SUBL, SUBL, tn), lambda j: (0, 0, 0, j)),
        out_shape=jax.ShapeDtypeStruct((S, R // SUBL, SUBL, N), BF16),
        scratch_shapes=[pltpu.VMEM((nb, 2 * R * SUBL, tn), F32)],
        compiler_params=_cparams(("parallel",)),
        name="fft_fourier",
    )(fx.reshape(2, R, nb, SUBL, N), _bf16_const(np.kron(f1_np, np.eye(SUBL))), _bf16_const(e2))
    return out.reshape(L, N)


def _use_fft(L):
    return L // FFT_S >= SUBL


def _hyena(hu, B, l, p):
    L = hu.shape[0]
    return _hy_fused(hu, B, _fft_filter(_hy_filter(L, l, p), L), l, p)


def _mix_ffn_kernel(ret_ref, four_ref, att_ref, hy_ref, wo_ref, x_ref, mod_ref, g_ref,
                    wg_ref, wu_ref, wd_ref, o_ref, hm_ref):
    mod = mod_ref[...]
    g1 = mod[:, 2 * D_MODEL:3 * D_MODEL]
    sh2 = mod[:, 3 * D_MODEL:4 * D_MODEL]
    sc2 = mod[:, 4 * D_MODEL:5 * D_MODEL]
    g2 = mod[:, 5 * D_MODEL:6 * D_MODEL]
    g = g_ref[...]

    def mix(rows):
        d = lambda a, i: jnp.dot(a[rows, :], wo_ref[i * 256:(i + 1) * 256, :], preferred_element_type=F32)
        m = d(ret_ref, 0) + d(four_ref, 1) + d(att_ref, 2) + d(hy_ref, 3)
        x1 = x_ref[rows, :] + g1 * _rms(m, g[1:2])
        o_ref[rows, :] = x1
        return (_rms(x1, g[2:3]) * (1.0 + sc2) + sh2).astype(BF16)

    def hidden(rows, h2):
        for c in range(D_FF // FFN_CHUNK):
            cs = slice(c * FFN_CHUNK, (c + 1) * FFN_CHUNK)
            a = jnp.dot(h2, wg_ref[:, cs], preferred_element_type=F32)
            u = jnp.dot(h2, wu_ref[:, cs], preferred_element_type=F32)
            hm_ref[rows, cs] = (_silu(a) * u).astype(BF16)

    def down(rows):
        f = jnp.dot(hm_ref[rows, :], wd_ref[...], preferred_element_type=F32)
        o_ref[rows, :] = o_ref[rows, :] + g2 * _rms(f, g[3:4])

    half = o_ref.shape[0] // 2
    ra, rb = slice(0, half), slice(half, 2 * half)
    hidden(ra, mix(ra))
    h2b = mix(rb)
    down(ra)
    hidden(rb, h2b)
    down(rb)


def _mix_ffn(ret, four, att, hy, x, mod, l, p, latent):
    B, L, _ = x.shape
    tm = _tiles(L)["tm_ffn"]
    row = (lambda b: b) if latent else (lambda b: CTX_ROW)
    bm = lambda w: pl.BlockSpec((None, tm, w), lambda b, i: (b, i, 0))
    tmj = pl.BlockSpec((tm, 256), lambda b, i: (i, b))
    resident = lambda shape: pl.BlockSpec((None,) + shape, lambda b, i: (l, 0, 0), pipeline_mode=pl.Buffered(1))
    return pl.pallas_call(
        _mix_ffn_kernel,
        grid=(B, L // tm),
        in_specs=[bm(RET_W), tmj, bm(MLA_W), tmj,
                  resident((4 * 256, D_MODEL)),
                  bm(D_MODEL),
                  pl.BlockSpec((None, None, 1, 6 * D_MODEL), lambda b, i: (l, row(b), 0, 0)),
                  pl.BlockSpec((None, 4, D_MODEL), lambda b, i: (l, 0, 0)),
                  resident((D_MODEL, D_FF)), resident((D_MODEL, D_FF)), resident((D_FF, D_MODEL))],
        out_specs=bm(D_MODEL),
        out_shape=jax.ShapeDtypeStruct((B, L, D_MODEL), F32),
        scratch_shapes=[pltpu.VMEM((tm, D_FF), BF16)],
        compiler_params=_cparams(("parallel", "parallel")),
        name="mix_ffn",
    )(ret, four, att, hy, p["w_out"], x, mod, p["norm_g"], p["w_gate"], p["w_up"], p["w_down"])


def _layer(x, mod, l, p, latent, ctx_kv, s0):
    B, L, _ = x.shape
    t = _tiles(L)
    qkvg, fx, q, k, v, hu, ckv, krp = _in_proj(x, mod, l, p, latent)
    ret, s_new = _retention(qkvg, p["dec_rows"], s0, l, emit_state=not latent)
    if _use_fft(L):
        four = _fft_fourier(fx, L)
    else:
        four = _mm(_bf16_const(_pos_dft_np(L)), fx.reshape(2 * L, B * FNET_W),
                   (FNET_GC * L) ** -0.5, BF16, t["hb"], t["tcol"], "fourier_pos")
    kc, vc = ctx_kv if latent else (None, None)
    att = _attention(q, k, v, kc, vc)
    hy = _hyena(hu, B, l, p)
    return _mix_ffn(ret, four, att, hy, x, mod, l, p, latent), ckv, krp, s_new


def _prep(w):
    last = lambda a, n: jnp.pad(a, [(0, 0)] * (a.ndim - 1) + [(0, n)])
    w_in = w["w_in"].astype(BF16)
    zeros = lambda c: jnp.zeros((DEPTH, D_MODEL, c), BF16)
    n0 = OFF_KR
    w_in_p = jnp.concatenate(
        [w_in[..., :n0], zeros(MLA_ROPE_OFF), w_in[..., n0:n0 + MLA_ROPE],
         zeros(MLA_HP - MLA_ROPE_OFF - MLA_ROPE), w_in[..., n0 + MLA_ROPE:]], axis=-1)
    uq = w["mla_w_uq"].reshape(DEPTH, MLA_Q_LORA, MLA_HEADS, MLA_NOPE + MLA_ROPE)
    uq = last(uq, MLA_HP - MLA_NOPE - MLA_ROPE).reshape(DEPTH, MLA_Q_LORA, -1)
    ukv = w["mla_w_ukv"].reshape(DEPTH, MLA_KV_LORA, MLA_HEADS, MLA_NOPE + MLA_V)
    uk = last(ukv[..., :MLA_NOPE], MLA_HP - MLA_NOPE).reshape(DEPTH, MLA_KV_LORA, -1)
    uv = last(ukv[..., MLA_NOPE:], MLA_HP - MLA_V).reshape(DEPTH, MLA_KV_LORA, -1)
    two = lambda a: jnp.stack([jnp.kron(jnp.eye(2, dtype=F32), a[l]) for l in range(DEPTH)])
    hy_w1 = jnp.pad(w["hy_w1"], ((0, 0), (0, HY_EMB_PAD - HY_EMB), (0, 0)))
    return dict(
        norm_g=w["norm_g"], w_in=w_in_p, w_out=w["w_out"].astype(BF16),
        q_norm=w["mla_q_norm"][:, None], kv_norm=w["mla_kv_norm"][:, None],
        w_uq=uq.astype(BF16), w_ukv=jnp.concatenate([uk, uv], axis=-1).astype(BF16),
        dec_rows=w["dec_rows"],
        hy_short_w=w["hy_short_w"], hy_short_b=w["hy_short_b"][:, None],
        hy_w1=two(hy_w1), hy_b1=jnp.tile(w["hy_b1"][:, None], (1, 1, 2)),
        hy_w2=two(w["hy_w2"]), hy_b2=jnp.tile(w["hy_b2"][:, None], (1, 1, 2)), hy_w3=w["hy_w3"],
        hy_bias=w["hy_bias"],
        w_gate=w["w_gate"].astype(BF16), w_up=w["w_up"].astype(BF16), w_down=w["w_down"].astype(BF16),
    )


def kernel(x_prompt, x_sample, cache_ckv, cache_krope, state_ret, c, c_ctx, w_ada, b_ada, norm_g, w_in, w_out,
           ret_decay, mla_q_norm, mla_kv_norm, mla_w_uq, mla_w_ukv, hy_short_w, hy_short_b, hy_w1, hy_b1,
           hy_w2, hy_b2, hy_w3, hy_bias, w_gate, w_up, w_down):
    nb = c.shape[0]
    cvec = jnp.concatenate([c, c_ctx[None], jnp.zeros((MOD_ROWS - nb - 1, D_MODEL), F32)], axis=0)
    mod = _ada(cvec, w_ada, b_ada).reshape(DEPTH, MOD_ROWS, 1, 6 * D_MODEL)
    dec_rows = jnp.pad(jnp.repeat(ret_decay, RET_DV, axis=-1), ((0, 0), (0, 6), (0, 0)))
    krope_pad = jnp.pad(cache_krope, ((0, 0), (0, 0), (0, 0),
                                      (MLA_ROPE_OFF, MLA_HP - MLA_ROPE_OFF - MLA_ROPE)))
    w = dict(norm_g=norm_g, w_in=w_in, w_out=w_out, dec_rows=dec_rows, mla_q_norm=mla_q_norm,
             mla_kv_norm=mla_kv_norm, mla_w_uq=mla_w_uq, mla_w_ukv=mla_w_ukv, hy_short_w=hy_short_w,
             hy_short_b=hy_short_b, hy_w1=hy_w1, hy_b1=hy_b1, hy_w2=hy_w2, hy_b2=hy_b2, hy_w3=hy_w3,
             hy_bias=hy_bias, w_gate=w_gate, w_up=w_up, w_down=w_down)
    xp, xs = x_prompt, x_sample
    new_ckv, new_kr, new_s = [], [], []
    p = _prep(w)
    for l in range(DEPTH):
        xp, ckv, krp, s_ret = _layer(xp, mod, l, p, False, None, None)
        new_ckv.append(ckv)
        new_kr.append(krp[:, :, MLA_ROPE_OFF:MLA_ROPE_OFF + MLA_ROPE])
        new_s.append(s_ret)
        ctx_kv = _ctx_kv(cache_ckv, krope_pad, l, p["w_ukv"])
        xs, _, _, _ = _layer(xs, mod, l, p, True, ctx_kv, state_ret)
    return (xp, xs, jnp.stack(new_ckv, axis=1), jnp.stack(new_kr, axis=1), jnp.stack(new_s, axis=1))
```

```python
import functools
import math

import numpy as np
import jax
import jax.numpy as jnp
from jax import lax
from jax.experimental import pallas as pl
from jax.experimental.pallas import tpu as pltpu

F32 = jnp.float32
BF16 = jnp.bfloat16

D_MODEL = 1024
DEPTH = 2
GRID_W = 64
EPS = 1e-6
ROPE_BASE = 10000.0
RET_HEADS = 4
RET_DK = 64
RET_DV = 64
RET_W = RET_HEADS * RET_DV
FNET_GROUPS = 4
FNET_GC = 64
FNET_W = FNET_GROUPS * FNET_GC
MLA_HEADS = 4
MLA_Q_LORA = 256
MLA_KV_LORA = 128
MLA_NOPE = 64
MLA_ROPE = 32
MLA_V = 64
MLA_W = MLA_HEADS * MLA_V
MLA_HP = 128
MLA_ROPE_OFF = MLA_NOPE
ATTN_CK = 256
SUBL = 8
FFT_TN = 256
FFT_S = 64
HY_CH = 256
HY_ORDER = 2
HY_BANDS = 16
HY_EMB = 1 + 2 * HY_BANDS
HY_EMB_PAD = 128
HY_FFN = 64
HY_FAST_DECAY = 0.3
HY_SLOW_DECAY = 1.5
HY_TARGET = 1e-2
D_FF = ((8 * D_MODEL + 3 * 256 - 1) // (3 * 256)) * 256
FFN_CHUNK = 256
MOD_ROWS = 16
CTX_ROW = 8

OFF_RET = 0
OFF_FU = 4 * RET_W
OFF_CQ = OFF_FU + FNET_W
OFF_CKV = OFF_CQ + MLA_Q_LORA
OFF_KR = OFF_CKV + MLA_KV_LORA
OFF_HU = OFF_KR + MLA_HP
IN_WP = OFF_HU + 3 * HY_CH

VMEM_LIMIT = 52 * 1024 * 1024


def _cparams(sem):
    return pltpu.CompilerParams(dimension_semantics=sem, vmem_limit_bytes=VMEM_LIMIT)


def _tiles(L):
    return dict(
        tm=min(L, 512),
        tm_ffn=min(L, 512),
        chunk=min(L, 256),
        tq=min(L, 512),
        hb=min(L, 512),
        tcol=1024,
    )


def _bf16_const(a):
    return jnp.asarray(np.asarray(a, np.float32)).astype(BF16)


@functools.lru_cache(maxsize=None)
def _rope_np(L, width, seg_off, seg_w, rope_dim):
    row = np.repeat(np.arange(L // GRID_W), GRID_W).astype(np.float64)
    col = np.tile(np.arange(GRID_W), L // GRID_W).astype(np.float64)
    quarter = rope_dim // 4
    inv = ROPE_BASE ** (-np.arange(quarter, dtype=np.float64) / quarter)
    cos = np.ones((L, width), np.float64)
    sin = np.zeros((L, width), np.float64)
    for lane in range(width):
        r = lane % seg_w - seg_off
        if r < 0 or r >= rope_dim:
            continue
        pos = row if r < rope_dim // 2 else col
        ang = pos * inv[r % quarter]
        first = (r % (rope_dim // 2)) < quarter
        cos[:, lane] = np.cos(ang)
        sin[:, lane] = -np.sin(ang) if first else np.sin(ang)
    return cos.astype(np.float32), sin.astype(np.float32)


@functools.lru_cache(maxsize=None)
def _chan_dft_np():
    k = np.arange(FNET_GC)
    ang = 2.0 * np.pi * np.outer(k, k) / FNET_GC
    c = np.kron(np.eye(FNET_GROUPS), np.cos(ang))
    s = np.kron(np.eye(FNET_GROUPS), np.sin(ang))
    return np.concatenate([c, s], axis=1)


@functools.lru_cache(maxsize=None)
def _pos_dft_np(L):
    k = np.arange(L, dtype=np.int64)
    ang = 2.0 * np.pi * (np.outer(k, k) % L) / L
    return np.concatenate([np.cos(ang), -np.sin(ang)], axis=1)


@functools.lru_cache(maxsize=None)
def _hyena_feat_np(L):
    pos = np.arange(L, dtype=np.float64)
    t = pos / L
    bands = np.arange(1, HY_BANDS + 1, dtype=np.float64)
    ang = (2.0 * math.pi / L) * pos[:, None] * bands[None, :]
    z = np.zeros((L, HY_EMB_PAD), np.float64)
    z[:, 0] = t
    z[:, 1:1 + HY_BANDS] = np.sin(ang)
    z[:, 1 + HY_BANDS:HY_EMB] = np.cos(ang)
    deltas = np.abs(np.linspace(math.log(HY_TARGET) / HY_SLOW_DECAY,
                                math.log(HY_TARGET) / HY_FAST_DECAY, HY_CH))
    window = np.exp(-t[:, None] * deltas[None, :])
    return z.astype(np.float32), window.astype(np.float32)


@functools.lru_cache(maxsize=None)
def _hy_fwd_tabs_np(L, S):
    R, Q, D = 2 * L // S, L // S, S // 2
    c, a = np.arange(R)[:, None], np.arange(Q)[None, :]
    ang = 2 * np.pi * ((c * a) % R) / R
    f1 = np.concatenate([np.cos(ang), -np.sin(ang)], axis=0)
    cc, d, b = np.arange(R)[:, None, None], np.arange(D)[None, :, None], np.arange(S)[None, None, :]
    th = np.pi * (((cc + R * d) * b) % (2 * L)) / L
    ur = np.concatenate([np.cos(th), np.sin(th)], axis=2)
    us = np.concatenate([np.sin(th), -np.cos(th)], axis=2)
    us[0, 0, :S] = (-1.0) ** np.arange(S)
    us[0, 0, S:] = 0.0
    return f1, np.concatenate([ur, us], axis=1)


@functools.lru_cache(maxsize=None)
def _hy_inv_tabs_np(L, S):
    R, Q, D = 2 * L // S, L // S, S // 2
    cc, f, d = np.arange(R)[:, None, None], np.arange(S)[None, :, None], np.arange(D)[None, None, :]
    k = cc + R * d
    be = np.pi * ((k * f) % (2 * L)) / L
    w = np.where(k == 0, 1.0, 2.0) / (2.0 * L) + 0.0 * be
    bc = np.concatenate([w * np.cos(be), w * np.sin(be)], axis=2)
    bs = np.concatenate([-w * np.sin(be), w * np.cos(be)], axis=2)
    bc[0, :, D] = (-1.0) ** np.arange(S) / (2.0 * L)
    bs[0, :, D] = 0.0
    e, c = np.arange(Q)[:, None], np.arange(R)[None, :]
    al = 2 * np.pi * ((e * c) % R) / R
    return np.concatenate([bc, bs], axis=1), np.concatenate([np.cos(al), np.sin(al)], axis=1)


@functools.lru_cache(maxsize=None)
def _four_tabs_np(L, S):
    R = L // S
    c, a = np.arange(R)[:, None], np.arange(R)[None, :]
    ang = 2 * np.pi * ((c * a) % R) / R
    f1 = np.concatenate([np.concatenate([np.cos(ang), -np.sin(ang)], axis=1),
                         np.concatenate([-np.sin(ang), -np.cos(ang)], axis=1)], axis=0)
    cc, d, b = np.arange(R)[:, None, None], np.arange(S)[None, :, None], np.arange(S)[None, None, :]
    th = 2 * np.pi * (((cc + R * d) * b) % L) / L
    return f1, np.concatenate([np.cos(th), np.sin(th)], axis=2)


def _rms(x, g):
    return x * lax.rsqrt(jnp.mean(x * x, axis=-1, keepdims=True) + EPS) * g


def _silu(x):
    return x * jax.nn.sigmoid(x)


def _bdot(a, b):
    return jnp.dot(a.astype(BF16), b.astype(BF16), preferred_element_type=F32)


def _split(a):
    hi = a.astype(BF16)
    lo = (a - hi.astype(F32)).astype(BF16)
    return hi, lo


def _dot3(a, b):
    ah, al = _split(a)
    bh, bl = _split(b)
    d = lambda x, y: jnp.dot(x, y, preferred_element_type=F32)
    return d(ah, bh) + (d(al, bh) + d(ah, bl))


def _store_kv(kv, kr_pad, k_ref, v_ref):
    wk = MLA_HEADS * MLA_HP
    k_ref[...] = (kv[:, :wk] + jnp.concatenate([kr_pad] * MLA_HEADS, axis=1)).astype(k_ref.dtype)
    lane = lax.broadcasted_iota(jnp.int32, (1, wk), 1)
    ones = jnp.where(lane % MLA_HP >= MLA_V, 1.0, 0.0)
    v_ref[...] = (kv[:, wk:] + ones).astype(v_ref.dtype)


def _rope(x, cos, sin_signed, quarter):
    w = x.shape[-1]
    lane = lax.broadcasted_iota(jnp.int32, x.shape, 1)
    first = (lane % (2 * quarter)) < quarter
    up = pltpu.roll(x, w - quarter, axis=1)
    dn = pltpu.roll(x, quarter, axis=1)
    return x * cos + jnp.where(first, up, dn) * sin_signed


def _ada_kernel(c_ref, w_ref, b_ref, o_ref):
    s = _silu(c_ref[...])
    o_ref[...] = _bdot(s, w_ref[...]) + b_ref[...]


def _ada(cvec, w_ada, b_ada):
    tn = 1024
    n6 = w_ada.shape[-1]
    return pl.pallas_call(
        _ada_kernel,
        grid=(DEPTH, n6 // tn),
        in_specs=[
            pl.BlockSpec((MOD_ROWS, D_MODEL), lambda l, j: (0, 0)),
            pl.BlockSpec((None, D_MODEL, tn), lambda l, j: (l, 0, j)),
            pl.BlockSpec((None, 1, tn), lambda l, j: (l, 0, j)),
        ],
        out_specs=pl.BlockSpec((None, MOD_ROWS, tn), lambda l, j: (l, 0, j)),
        out_shape=jax.ShapeDtypeStruct((DEPTH, MOD_ROWS, n6), F32),
        compiler_params=_cparams(("parallel", "parallel")),
        name="ada",
    )(cvec, w_ada, b_ada.reshape(DEPTH, 1, n6))


def _in_proj_kernel(*refs, latent):
    (x_ref, mod_ref, g0_ref, w_ref, qn_ref, kvn_ref, wuq_ref, wukv_ref, bd_ref) = refs[:9]
    pos = 9
    if latent:
        cr_ref, sr_ref, cm_ref, sm_ref = refs[pos:pos + 4]
        pos += 4
        qkvg_ref, fx_ref, q_ref, k_ref, v_ref, hu_ref = refs[pos:pos + 6]
        lanes = lambda ref, reps: jnp.concatenate([ref[...]] * reps, axis=1)
    else:
        qkvg_ref, fx_ref, q_ref, k_ref, v_ref, hu_ref, ckv_ref, kr_ref = refs[pos:pos + 8]

    x = x_ref[...]
    mod = mod_ref[...]
    sh1 = mod[:, 0:D_MODEL]
    sc1 = mod[:, D_MODEL:2 * D_MODEL]
    h = _rms(x, g0_ref[0:1, :]) * (1.0 + sc1) + sh1
    proj = jnp.dot(h.astype(BF16), w_ref[...], preferred_element_type=F32)

    rq = proj[:, OFF_RET:OFF_RET + RET_W]
    rk = proj[:, OFF_RET + RET_W:OFF_RET + 2 * RET_W] * (RET_DK ** -0.5)
    if latent:
        cos_r, sin_r = lanes(cr_ref, RET_W // 128), lanes(sr_ref, RET_W // 128)
        rq = _rope(rq, cos_r, sin_r, RET_DK // 4)
        rk = _rope(rk, cos_r, sin_r, RET_DK // 4)
    qkvg_ref[:, 0:RET_W] = rq
    qkvg_ref[:, RET_W:2 * RET_W] = rk
    qkvg_ref[:, 2 * RET_W:4 * RET_W] = proj[:, OFF_RET + 2 * RET_W:OFF_RET + 4 * RET_W]

    fcs = _bdot(proj[:, OFF_FU:OFF_FU + FNET_W], bd_ref[...])
    fx_ref[0] = fcs[:, :FNET_W].astype(fx_ref.dtype)
    fx_ref[1] = fcs[:, FNET_W:].astype(fx_ref.dtype)

    cqn = _rms(proj[:, OFF_CQ:OFF_CQ + MLA_Q_LORA], qn_ref[...])
    q = _bdot(cqn, wuq_ref[...])
    ckvn = _rms(proj[:, OFF_CKV:OFF_CKV + MLA_KV_LORA], kvn_ref[...])
    kv = _bdot(ckvn, wukv_ref[...])
    krp = proj[:, OFF_KR:OFF_KR + MLA_HP]
    if latent:
        q = _rope(q, lanes(cm_ref, MLA_HEADS), lanes(sm_ref, MLA_HEADS), MLA_ROPE // 4)
        krp = _rope(krp, cm_ref[...], sm_ref[...], MLA_ROPE // 4)
    else:
        ckv_ref[...] = ckvn
        kr_ref[...] = krp
    q_ref[...] = (q * ((MLA_NOPE + MLA_ROPE) ** -0.5 * math.log2(math.e))).astype(q_ref.dtype)
    _store_kv(kv, krp, k_ref, v_ref)

    hu_ref[...] = proj[:, OFF_HU:OFF_HU + 3 * HY_CH]


def _in_proj(x, mod, l, p, latent):
    B, L, _ = x.shape
    t = _tiles(L)
    tm = t["tm"]
    nt = L // tm
    row = (lambda b: b) if latent else (lambda b: CTX_ROW)
    full = lambda shape: pl.BlockSpec((None,) + shape, lambda i, b: (l,) + (0,) * len(shape),
                                      pipeline_mode=pl.Buffered(1))
    in_specs = [
        pl.BlockSpec((None, tm, D_MODEL), lambda i, b: (b, i, 0)),
        pl.BlockSpec((None, None, 1, 6 * D_MODEL), lambda i, b: (l, row(b), 0, 0)),
        full((4, D_MODEL)),
        full((D_MODEL, IN_WP)),
        full((1, MLA_Q_LORA)),
        full((1, MLA_KV_LORA)),
        full((MLA_Q_LORA, MLA_HEADS * MLA_HP)),
        full((MLA_KV_LORA, 2 * MLA_HEADS * MLA_HP)),
        pl.BlockSpec((FNET_W, 2 * FNET_W), lambda i, b: (0, 0), pipeline_mode=pl.Buffered(1)),
    ]
    args = [x, mod, p["norm_g"], p["w_in"], p["q_norm"], p["kv_norm"], p["w_uq"], p["w_ukv"],
            _bf16_const(_chan_dft_np())]
    if latent:
        tabs = (_rope_np(L, 128, 0, RET_DK, RET_DK)
                + _rope_np(L, MLA_HP, MLA_ROPE_OFF, MLA_HP, MLA_ROPE))
        for tab in tabs:
            in_specs.append(pl.BlockSpec((tm, tab.shape[1]), lambda i, b: (i, 0)))
            args.append(jnp.asarray(tab))
    bm = lambda w: pl.BlockSpec((None, tm, w), lambda i, b: (b, i, 0))
    sds = jax.ShapeDtypeStruct
    out_specs = [
        bm(4 * RET_W),
        pl.BlockSpec((2, tm, FNET_W), lambda i, b: (0, i, b)),
        bm(MLA_HEADS * MLA_HP), bm(MLA_HEADS * MLA_HP), bm(MLA_HEADS * MLA_HP),
        pl.BlockSpec((tm, 3 * HY_CH), lambda i, b: (i, b)),
    ]
    out_shape = [
        sds((B, L, 4 * RET_W), F32),
        sds((2, L, B * FNET_W), F32 if _use_fft(L) else BF16),
        sds((B, L, MLA_HEADS * MLA_HP), BF16),
        sds((B, L, MLA_HEADS * MLA_HP), BF16),
        sds((B, L, MLA_HEADS * MLA_HP), BF16),
        sds((L, B * 3 * HY_CH), F32),
    ]
    if not latent:
        out_specs += [bm(MLA_KV_LORA), bm(MLA_HP)]
        out_shape += [sds((B, L, MLA_KV_LORA), F32), sds((B, L, MLA_HP), F32)]
    outs = pl.pallas_call(
        functools.partial(_in_proj_kernel, latent=latent),
        grid=(nt, B), in_specs=in_specs, out_specs=out_specs, out_shape=out_shape,
        compiler_params=_cparams(("parallel", "parallel")),
        name="in_proj_lat" if latent else "in_proj_ctx",
    )(*args)
    return tuple(outs) + ((None, None) if latent else ())


def _ctx_kv_kernel(ckv_ref, kr_ref, w_ref, k_ref, v_ref):
    _store_kv(_bdot(ckv_ref[...], w_ref[...]), kr_ref[...], k_ref, v_ref)


def _ctx_kv(cache_ckv, krope_pad, l, w_ukv):
    B, _, P, _ = cache_ckv.shape
    return pl.pallas_call(
        _ctx_kv_kernel,
        grid=(B,),
        in_specs=[
            pl.BlockSpec((None, None, P, MLA_KV_LORA), lambda b: (b, l, 0, 0)),
            pl.BlockSpec((None, None, P, MLA_HP), lambda b: (b, l, 0, 0)),
            pl.BlockSpec((None,) + w_ukv.shape[1:], lambda b: (l, 0, 0)),
        ],
        out_specs=[
            pl.BlockSpec((None, P, MLA_HEADS * MLA_HP), lambda b: (b, 0, 0)),
            pl.BlockSpec((None, P, MLA_HEADS * MLA_HP), lambda b: (b, 0, 0)),
        ],
        out_shape=[
            jax.ShapeDtypeStruct((B, P, MLA_HEADS * MLA_HP), BF16),
            jax.ShapeDtypeStruct((B, P, MLA_HEADS * MLA_HP), BF16),
        ],
        compiler_params=_cparams(("parallel",)),
        name="ctx_kv",
    )(cache_ckv, krope_pad, w_ukv)


def _ret_kernel(*refs, L, C, has_s0, emit_state):
    q_ref, k_ref, v_ref, g_ref, dec_ref = refs[:5]
    pos = 5
    if has_s0:
        s0_ref = refs[pos]
        pos += 1
    o_ref = refs[pos]
    pos += 1
    if emit_state:
        sn_ref = refs[pos]
        pos += 1
    sf_ref, sb_ref, dm_ref, tab_ref = refs[pos:pos + 4]
    n = L // C
    H, W = RET_HEADS, RET_W

    d = dec_ref[0:2, :]
    lg = jnp.minimum(d, 0.0) - jnp.log1p(jnp.exp(-jnp.abs(d)))
    lf, lb = lg[0:1, :], lg[1:2, :]
    @pl.when(pl.program_id(0) == 0)
    def _():
        ii = lax.broadcasted_iota(jnp.int32, (C, C), 0)
        jj = lax.broadcasted_iota(jnp.int32, (C, C), 1)
        diff = (ii - jj).astype(F32)
        for h in range(H):
            lfh = lf[:, h * RET_DV:h * RET_DV + 1]
            lbh = lb[:, h * RET_DV:h * RET_DV + 1]
            dm_ref[h] = jnp.where(diff > 0, jnp.exp(diff * lfh),
                                  jnp.where(diff < 0, jnp.exp(-diff * lbh), 2.0))
        ci = lax.broadcasted_iota(jnp.int32, (C, W), 0).astype(F32)
        tab_ref[0] = jnp.exp((ci + 1.0) * lf)
        tab_ref[1] = jnp.exp((C - ci) * lb)
        tab_ref[2] = jnp.exp((C - 1.0 - ci) * lf)
        tab_ref[3] = jnp.exp(ci * lb)

    g_f = jnp.exp(C * lf)
    g_b = jnp.exp(C * lb)
    hr = lax.broadcasted_iota(jnp.int32, (W, W), 0) // RET_DK
    hc = lax.broadcasted_iota(jnp.int32, (W, W), 1) // RET_DV
    same_head = hr == hc
    avg = jnp.where(same_head, 1.0 / RET_DV, 0.0).astype(BF16)

    sf_ref[0] = jnp.zeros((W, W), F32)
    sb_ref[n] = jnp.zeros((W, W), F32)
    if has_s0:
        for h in range(H):
            blk = slice(h * RET_DK, (h + 1) * RET_DK)
            sf_ref[0, blk, blk] = s0_ref[0, h]
            sb_ref[n, blk, blk] = s0_ref[1, h]

    tdot = lambda a, b: lax.dot_general(a.astype(BF16), b.astype(BF16), (((0,), (0,)), ((), ())),
                                        preferred_element_type=F32)

    def states(j, carry):
        jb = n - 1 - j
        rf = pl.multiple_of(j * C, C)
        rb = pl.multiple_of(jb * C, C)
        upd = tdot(k_ref[pl.ds(rf, C), :] * tab_ref[2], v_ref[pl.ds(rf, C), :])
        sf_ref[j + 1] = g_f * sf_ref[j] + jnp.where(same_head, upd, 0.0)
        upd = tdot(k_ref[pl.ds(rb, C), :] * tab_ref[3], v_ref[pl.ds(rb, C), :])
        sb_ref[jb] = g_b * sb_ref[jb + 1] + jnp.where(same_head, upd, 0.0)
        return carry

    lax.fori_loop(0, n, states, 0, unroll=4 if n % 4 == 0 else 1)

    lane_head = lax.broadcasted_iota(jnp.int32, (C, W), 1) // RET_DV

    def group_mean(x):
        hi, lo = _split(x)
        return (jnp.dot(hi, avg, preferred_element_type=F32)
                + jnp.dot(lo, avg, preferred_element_type=F32))

    def chunk(j, carry):
        r0 = pl.multiple_of(j * C, C)
        q = q_ref[pl.ds(r0, C), :]
        qb = q.astype(BF16)
        kb = k_ref[pl.ds(r0, C), :].astype(BF16)
        vb = v_ref[pl.ds(r0, C), :].astype(BF16)
        o = tab_ref[0] * jnp.dot(qb, sf_ref[j].astype(BF16), preferred_element_type=F32)
        o = o + tab_ref[1] * jnp.dot(qb, sb_ref[j + 1].astype(BF16), preferred_element_type=F32)
        inner = jnp.zeros((C, W), F32)
        for h in range(H):
            in_head = lane_head == h
            qh = jnp.where(in_head, q, 0.0).astype(BF16)
            s = lax.dot_general(qh, kb, (((1,), (1,)), ((), ())), preferred_element_type=F32)
            oh = jnp.dot((s * dm_ref[h]).astype(BF16), vb, preferred_element_type=F32)
            inner = jnp.where(in_head, oh, inner)
        o = o + inner
        dlt = o - group_mean(o)
        var = group_mean(dlt * dlt)
        gate = g_ref[pl.ds(r0, C), :]
        o_ref[pl.ds(r0, C), :] = (_silu(gate) * (dlt * lax.rsqrt(var + EPS))).astype(o_ref.dtype)
        return carry

    lax.fori_loop(0, n, chunk, 0, unroll=4 if n % 4 == 0 else 1)

    if emit_state:
        for h in range(H):
            blk = slice(h * RET_DK, (h + 1) * RET_DK)
            sn_ref[0, h] = sf_ref[n, blk, blk]
            sn_ref[1, h] = sb_ref[0, blk, blk]


def _retention(qkvg, dec_rows, s0, l, emit_state):
    B, L, _ = qkvg.shape
    C = _tiles(L)["chunk"]
    n = L // C
    has_s0 = s0 is not None
    def colspec(j):
        return pl.BlockSpec((None, L, RET_W), lambda b: (b, 0, j))

    in_specs = [colspec(j) for j in range(4)]
    in_specs.append(pl.BlockSpec((None, 8, RET_W), lambda b: (l, 0, 0)))
    args = [qkvg, qkvg, qkvg, qkvg, dec_rows]
    st_block = (None, None, 2, RET_HEADS, RET_DK, RET_DV)
    if has_s0:
        in_specs.append(pl.BlockSpec(st_block, lambda b: (b, l, 0, 0, 0, 0)))
        args.append(s0)
    out_specs = [pl.BlockSpec((None, L, RET_W), lambda b: (b, 0, 0))]
    out_shape = [jax.ShapeDtypeStruct((B, L, RET_W), BF16)]
    if emit_state:
        out_specs.append(pl.BlockSpec((None, 2, RET_HEADS, RET_DK, RET_DV), lambda b: (b, 0, 0, 0, 0)))
        out_shape.append(jax.ShapeDtypeStruct((B, 2, RET_HEADS, RET_DK, RET_DV), F32))
    outs = pl.pallas_call(
        functools.partial(_ret_kernel, L=L, C=C, has_s0=has_s0, emit_state=emit_state),
        grid=(B,), in_specs=in_specs, out_specs=out_specs, out_shape=out_shape,
        scratch_shapes=[
            pltpu.VMEM((n + 1, RET_W, RET_W), F32),
            pltpu.VMEM((n + 1, RET_W, RET_W), F32),
            pltpu.VMEM((RET_HEADS, C, C), F32),
            pltpu.VMEM((4, C, RET_W), F32),
        ],
        compiler_params=_cparams(("arbitrary",)),
        name="retention_lat" if has_s0 else "retention_ctx",
    )(*args)
    return outs if emit_state else (outs[0], None)


def _mm_kernel(a_ref, b_ref, o_ref, *, scale):
    o_ref[...] = (jnp.dot(a_ref[...], b_ref[...], preferred_element_type=F32) * scale).astype(o_ref.dtype)


def _mm(a, b, scale, out_dtype, tm, tn, name):
    M, K = a.shape
    _, N = b.shape
    tm, tn = min(tm, M), min(tn, N)
    return pl.pallas_call(
        functools.partial(_mm_kernel, scale=scale),
        grid=(N // tn, M // tm),
        in_specs=[pl.BlockSpec((tm, K), lambda j, i: (i, 0)),
                  pl.BlockSpec((K, tn), lambda j, i: (0, j))],
        out_specs=pl.BlockSpec((tm, tn), lambda j, i: (i, j)),
        out_shape=jax.ShapeDtypeStruct((M, N), out_dtype),
        compiler_params=_cparams(("parallel", "parallel")),
        name=name,
    )(a, b)


def _attn_kernel(*refs, has_ctx):
    if has_ctx:
        q_ref, k_ref, v_ref, kc_ref, vc_ref, o_ref, s_ref = refs
        srcs = ((k_ref, v_ref), (kc_ref, vc_ref))
    else:
        q_ref, k_ref, v_ref, o_ref, s_ref = refs
        srcs = ((k_ref, v_ref),)
    nt = lambda a, b: lax.dot_general(a, b, (((1,), (1,)), ((), ())), preferred_element_type=F32)
    chunks = [(kr, vr, c0) for kr, vr in srcs for c0 in range(0, kr.shape[0], ATTN_CK)]
    def scores(h):
        hq = slice(h * MLA_HP, (h + 1) * MLA_HP)
        qh = q_ref[:, hq]
        mx = None
        for n, (kr, _, c0) in enumerate(chunks):
            s = nt(qh, kr[c0:c0 + ATTN_CK, hq])
            s_ref[h % 2, :, n * ATTN_CK:(n + 1) * ATTN_CK] = s
            for g in range(ATTN_CK // 128):
                part = s[:, g * 128:(g + 1) * 128]
                mx = part if mx is None else jnp.maximum(mx, part)
        return jnp.max(mx, axis=-1, keepdims=True)

    def weighted_values(h, m):
        hq = slice(h * MLA_HP, (h + 1) * MLA_HP)
        acc = None
        for n, (_, vr, c0) in enumerate(chunks):
            p = jnp.exp2(s_ref[h % 2, :, n * ATTN_CK:(n + 1) * ATTN_CK] - m).astype(BF16)
            d = jnp.dot(p, vr[c0:c0 + ATTN_CK, hq], preferred_element_type=F32)
            acc = d if acc is None else acc + d
        o_ref[:, h * MLA_V:(h + 1) * MLA_V] = (acc[:, :MLA_V] / acc[:, MLA_V:2 * MLA_V]).astype(o_ref.dtype)

    m = scores(0)
    for h in range(MLA_HEADS):
        m_next = scores(h + 1) if h + 1 < MLA_HEADS else None
        weighted_values(h, m)
        m = m_next


def _attention(q, k, v, kc, vc):
    B, L, _ = q.shape
    tq = _tiles(L)["tq"]
    has_ctx = kc is not None
    whole = lambda a: pl.BlockSpec((None,) + a.shape[1:], lambda b, i: (b, 0, 0))
    in_specs = [pl.BlockSpec((None, tq, q.shape[2]), lambda b, i: (b, i, 0)), whole(k), whole(v)]
    args = [q, k, v]
    n_keys = L
    if has_ctx:
        in_specs += [whole(kc), whole(vc)]
        args += [kc, vc]
        n_keys += kc.shape[1]
    return pl.pallas_call(
        functools.partial(_attn_kernel, has_ctx=has_ctx),
        grid=(B, L // tq), in_specs=in_specs,
        out_specs=pl.BlockSpec((None, tq, MLA_W), lambda b, i: (b, i, 0)),
        out_shape=jax.ShapeDtypeStruct((B, L, MLA_W), BF16),
        scratch_shapes=[pltpu.VMEM((2, tq, n_keys), F32)],
        compiler_params=_cparams(("parallel", "parallel")),
        name="attention_lat" if has_ctx else "attention_ctx",
    )(*args)


def _hy_filter_kernel(z_ref, w1_ref, b1_ref, w2_ref, b2_ref, w3_ref, win_ref, x_ref):
    h1 = jnp.sin(_dot3(z_ref[...], w1_ref[...]) + b1_ref[...])
    h2 = jnp.sin(_dot3(h1, w2_ref[...]) + b2_ref[...])
    w3 = w3_ref[...]
    h = jnp.concatenate([_dot3(h2[:, :HY_FFN], w3), _dot3(h2[:, HY_FFN:], w3)], axis=0)
    win = win_ref[...]
    row = lax.broadcasted_iota(jnp.int32, win.shape, 0)
    for o in range(HY_ORDER):
        base = o * 2 * HY_CH
        hf = h[:, base:base + HY_CH] * win
        hb = h[:, base + HY_CH:base + 2 * HY_CH] * win
        hb = jnp.where(row == 0, 0.0, hb)
        nrm = jnp.sum(jnp.abs(hf), axis=0, keepdims=True) + jnp.sum(jnp.abs(hb), axis=0, keepdims=True) + EPS
        inv = 1.0 / nrm
        x_ref[:, base:base + HY_CH] = (hf + hb) * inv
        x_ref[:, base + HY_CH:base + 2 * HY_CH] = (hf - hb) * inv


def _hy_filter(L, l, p):
    z, win = _hyena_feat_np(L)
    z2 = np.concatenate([z[:L // 2], z[L // 2:]], axis=1)
    full = lambda shape: pl.BlockSpec(shape, lambda i: (0,) * len(shape))
    layer = lambda shape: pl.BlockSpec((None,) + shape, lambda i: (l,) + (0,) * len(shape))
    return pl.pallas_call(
        _hy_filter_kernel,
        grid=(1,),
        in_specs=[full((L // 2, 2 * HY_EMB_PAD)), layer((2 * HY_EMB_PAD, 2 * HY_FFN)), layer((1, 2 * HY_FFN)),
                  layer((2 * HY_FFN, 2 * HY_FFN)), layer((1, 2 * HY_FFN)),
                  layer((HY_FFN, HY_ORDER * 2 * HY_CH)), full((L, HY_CH))],
        out_specs=full((L, HY_ORDER * 2 * HY_CH)),
        out_shape=jax.ShapeDtypeStruct((L, HY_ORDER * 2 * HY_CH), F32),
        compiler_params=_cparams(("arbitrary",)),
        name="hy_filter",
    )(jnp.asarray(z2), p["hy_w1"], p["hy_b1"], p["hy_w2"], p["hy_b2"], p["hy_w3"], jnp.asarray(win))


def _tile_rows(ref, lead, rows):
    return jnp.concatenate([ref[lead, r * SUBL:(r + 1) * SUBL, :] for r in rows], axis=0)


def _fft_fwd_kernel(*refs, Q, R, S, filt):
    if filt:
        u_ref, k1_ref, e_ref, y_ref, a_scr = refs
    else:
        u_ref, k1_ref, e_ref, ga_ref, gb_ref, ny_ref, y_ref, a_scr = refs
    D = S // 2
    nb = S // SUBL
    tn = u_ref.shape[-1]
    dot = lambda p, q: jnp.dot(p, q, preferred_element_type=F32)
    k1 = k1_ref[...]
    for bh in range(nb):
        x = u_ref[:, bh].reshape(Q * SUBL, tn)
        if filt:
            xh, xl = _split(x)
            a_scr[bh] = dot(k1, xh) + dot(k1, xl)
        else:
            a_scr[bh] = dot(k1, x.astype(BF16))
    for c in range(R):
        rhs = jnp.concatenate([_tile_rows(a_scr, bh, (part * R + c,)) for part in (0, 1) for bh in range(nb)],
                              axis=0)
        if filt:
            rh, rl = _split(rhs)
            y_ref[c] = dot(e_ref[c], rh) + dot(e_ref[c], rl)
            continue
        y = dot(e_ref[c], rhs.astype(BF16))
        ur, us = y[:D], y[D:]
        ga, gb = ga_ref[c], gb_ref[c]
        gd = ga
        if c == 0:
            row0 = lax.broadcasted_iota(jnp.int32, ga.shape, 0) == 0
            gb = jnp.where(row0, 0.0, gb)
            gd = jnp.where(row0, jnp.broadcast_to(ny_ref[0, 0:1, :], ga.shape), ga)
        y_ref[c, :D, :] = (ur * ga - us * gb).astype(y_ref.dtype)
        y_ref[c, D:, :] = (ur * gb + us * gd).astype(y_ref.dtype)


def _fft_filter(taps, L):
    S = FFT_S
    R, Q = 2 * L // S, L // S
    N = taps.shape[1]
    tn = FFT_TN
    nb = S // SUBL
    f1_np, e_np = _hy_fwd_tabs_np(L, S)
    const = lambda shape: pl.BlockSpec(shape, lambda j: (0,) * len(shape), pipeline_mode=pl.Buffered(1))
    return pl.pallas_call(
        functools.partial(_fft_fwd_kernel, Q=Q, R=R, S=S, filt=True),
        grid=(N // tn,),
        in_specs=[pl.BlockSpec((Q, nb, SUBL, tn), lambda j: (0, 0, 0, j)),
                  const((2 * R * SUBL, Q * SUBL)), const((R, S, 2 * S))],
        out_specs=pl.BlockSpec((R, S, tn), lambda j: (0, 0, j)),
        out_shape=jax.ShapeDtypeStruct((R, S, N), F32),
        scratch_shapes=[pltpu.VMEM((nb, 2 * R * SUBL, tn), F32)],
        compiler_params=_cparams(("parallel",)),
        name="fft_filter",
    )(taps.reshape(Q, nb, SUBL, N), _bf16_const(np.kron(f1_np, np.eye(SUBL))), _bf16_const(e_np))


def _fft_inv_kernel(y_ref, e_ref, k2_ref, gate_ref, z_ref, d_ref, o_ref, scr, *, Q, R, S):
    dot = lambda p, q: jnp.dot(p, q, preferred_element_type=F32)
    tn = y_ref.shape[-1]
    per = scr.shape[1] // (2 * S)
    for c in range(R):
        scr[c // per, (c % per) * 2 * S:(c % per + 1) * 2 * S, :] = dot(e_ref[c], y_ref[c])
    k2 = k2_ref[...]
    d = d_ref[...]
    for fh in range(S // SUBL):
        rhs = jnp.concatenate([_tile_rows(scr, c // per, ((c % per) * 2 * S // SUBL + part * (S // SUBL) + fh,))
                               for part in (0, 1) for c in range(R)], axis=0).astype(BF16)
        conv = dot(k2, rhs).reshape(Q, SUBL, tn)
        o_ref[:, fh] = (gate_ref[:, fh] * (conv + d * z_ref[:, fh])).astype(o_ref.dtype)


def _hy_fused_kernel(hu_ref, w_ref, b_ref, k1_ref, e_ref, ei_ref, k2_ref, ga0, gb0, ny0, ga1, gb1, ny1,
                     bias_ref, o_ref, v4, x14, x24, z14, y_scr, scr, *, Q, R, S):
    L, tn = hu_ref.shape[0], HY_CH
    w, b = w_ref[...], b_ref[...]
    row = lax.broadcasted_iota(jnp.int32, (L, tn), 0)
    for g, dst in enumerate((v4, x14, x24)):
        cs = slice(g * tn, (g + 1) * tn)
        u = hu_ref[:, cs]
        prev = jnp.where(row == 0, 0.0, pltpu.roll(u, 1, axis=0))
        nxt = jnp.where(row == L - 1, 0.0, pltpu.roll(u, L - 1, axis=0))
        uc = prev * w[0:1, cs] + u * w[1:2, cs] + nxt * w[2:3, cs] + b[:, cs]
        dst[...] = uc.reshape(dst.shape)
    fwd = functools.partial(_fft_fwd_kernel, Q=Q, R=R, S=S, filt=False)
    inv = functools.partial(_fft_inv_kernel, Q=Q, R=R, S=S)
    fwd(v4, k1_ref, e_ref, ga0, gb0, ny0, y_scr, scr)
    inv(y_scr, ei_ref, k2_ref, x14, v4, bias_ref.at[0:1], z14, scr)
    fwd(z14, k1_ref, e_ref, ga1, gb1, ny1, y_scr, scr)
    inv(y_scr, ei_ref, k2_ref, x24, z14, bias_ref.at[1:2], o_ref, scr)


def _hy_fused(hu, B, spec, l, p):
    layer = lambda shape: pl.BlockSpec((None,) + shape, lambda j: (l,) + (0,) * len(shape),
                                       pipeline_mode=pl.Buffered(1))
    L = hu.shape[0]
    S = FFT_S
    R, Q, D = 2 * L // S, L // S, S // 2
    nb = S // SUBL
    tn = HY_CH
    f1_np, e_np = _hy_fwd_tabs_np(L, S)
    ei_np, f2_np = _hy_inv_tabs_np(L, S)
    const = lambda shape: pl.BlockSpec(shape, lambda j: (0,) * len(shape), pipeline_mode=pl.Buffered(1))
    one = lambda blk, idx: pl.BlockSpec(blk, lambda j: idx, pipeline_mode=pl.Buffered(1))
    g_specs = [s for o in range(HY_ORDER) for s in (one((R, D, tn), (0, 0, 2 * o)),
                                                    one((R, D, tn), (0, 1, 2 * o + 1)),
                                                    one((1, 8, tn), (0, D // 8, 2 * o)))]
    work = pltpu.VMEM((Q, nb, SUBL, tn), F32)
    out = pl.pallas_call(
        functools.partial(_hy_fused_kernel, Q=Q, R=R, S=S),
        grid=(B,),
        in_specs=[pl.BlockSpec((L, 3 * HY_CH), lambda j: (0, j)), layer((3, 3 * HY_CH)), layer((1, 3 * HY_CH)),
                  const((2 * R * SUBL, Q * SUBL)), const((R, S, 2 * S)), const((R, 2 * S, S)),
                  const((Q * SUBL, 2 * R * SUBL))] + g_specs + [layer((HY_ORDER, HY_CH))],
        out_specs=pl.BlockSpec((Q, nb, SUBL, tn), lambda j: (0, 0, 0, j)),
        out_shape=jax.ShapeDtypeStruct((Q, nb, SUBL, B * HY_CH), BF16),
        scratch_shapes=[work, work, work, work, pltpu.VMEM((R, S, tn), BF16),
                        pltpu.VMEM((nb, 2 * R * SUBL, tn), F32)],
        compiler_params=_cparams(("parallel",)),
        name="hy_fused",
    )(hu, p["hy_short_w"], p["hy_short_b"], _bf16_const(np.kron(f1_np, np.eye(SUBL))), _bf16_const(e_np),
      _bf16_const(ei_np), _bf16_const(np.kron(f2_np, np.eye(SUBL))), *([spec] * 6), p["hy_bias"])
    return out.reshape(L, B * HY_CH)


def _fft_four_kernel(x_ref, k1_ref, e2_ref, o_ref, a_scr, *, R, S, scale):
    dot = lambda p, q: jnp.dot(p, q, preferred_element_type=F32)
    tn = x_ref.shape[-1]
    nb = S // SUBL
    k1 = k1_ref[...]
    for bh in range(nb):
        x = jnp.concatenate([x_ref[part, :, bh].reshape(R * SUBL, tn) for part in (0, 1)], axis=0)
        a_scr[bh] = dot(k1, x.astype(BF16))
    for ch in range(R // SUBL):
        rhs = jnp.concatenate([_tile_rows(a_scr, bh, (part * R + ch * SUBL + cl,))
                               for cl in range(SUBL) for part in (0, 1) for bh in range(nb)],
                              axis=0).astype(BF16)
        y = dot(e2_ref[ch], rhs)
        o_ref[:, ch] = (y.reshape(S, SUBL, tn) * scale).astype(o_ref.dtype)


def _fft_fourier(fx, L):
    S = FFT_S
    R = L // S
    N = fx.shape[2]
    tn = FFT_TN
    nb = S // SUBL
    f1_np, e_np = _four_tabs_np(L, S)
    e2 = np.zeros((R // SUBL, S, SUBL, SUBL, 2 * S))
    for cl in range(SUBL):
        e2[:, :, cl, cl, :] = e_np[cl::SUBL]
    e2 = e2.reshape(R // SUBL, S * SUBL, SUBL * 2 * S)
    const = lambda shape: pl.BlockSpec(shape, lambda j: (0,) * len(shape), pipeline_mode=pl.Buffered(1))
    out = pl.pallas_call(
        functools.partial(_fft_four_kernel, R=R, S=S, scale=(FNET_GC * L) ** -0.5),
        grid=(N // tn,),
        in_specs=[pl.BlockSpec((2, R, nb, SUBL, tn), lambda j: (0, 0, 0, 0, j)),
                  const((2 * R * SUBL, 2 * R * SUBL)), const(e2.shape)],
        out_specs=pl.BlockSpec((S, R // SUBL, SUBL, tn), lambda j: (0, 0, 0, j)),
        out_shape=jax.ShapeDtypeStruct((S, R // SUBL, SUBL, N), BF16),
        scratch_shapes=[pltpu.VMEM((nb, 2 * R * SUBL, tn), F32)],
        compiler_params=_cparams(("parallel",)),
        name="fft_fourier",
    )(fx.reshape(2, R, nb, SUBL, N), _bf16_const(np.kron(f1_np, np.eye(SUBL))), _bf16_const(e2))
    return out.reshape(L, N)


def _use_fft(L):
    return L // FFT_S >= SUBL


def _hyena(hu, B, l, p):
    L = hu.shape[0]
    return _hy_fused(hu, B, _fft_filter(_hy_filter(L, l, p), L), l, p)


def _mix_ffn_kernel(ret_ref, four_ref, att_ref, hy_ref, wo_ref, x_ref, mod_ref, g_ref,
                    wg_ref, wu_ref, wd_ref, o_ref, hm_ref):
    mod = mod_ref[...]
    g1 = mod[:, 2 * D_MODEL:3 * D_MODEL]
    sh2 = mod[:, 3 * D_MODEL:4 * D_MODEL]
    sc2 = mod[:, 4 * D_MODEL:5 * D_MODEL]
    g2 = mod[:, 5 * D_MODEL:6 * D_MODEL]
    g = g_ref[...]

    def mix(rows):
        d = lambda a, i: jnp.dot(a[rows, :], wo_ref[i * 256:(i + 1) * 256, :], preferred_element_type=F32)
        m = d(ret_ref, 0) + d(four_ref, 1) + d(att_ref, 2) + d(hy_ref, 3)
        x1 = x_ref[rows, :] + g1 * _rms(m, g[1:2])
        o_ref[rows, :] = x1
        return (_rms(x1, g[2:3]) * (1.0 + sc2) + sh2).astype(BF16)

    def hidden(rows, h2):
        for c in range(D_FF // FFN_CHUNK):
            cs = slice(c * FFN_CHUNK, (c + 1) * FFN_CHUNK)
            a = jnp.dot(h2, wg_ref[:, cs], preferred_element_type=F32)
            u = jnp.dot(h2, wu_ref[:, cs], preferred_element_type=F32)
            hm_ref[rows, cs] = (_silu(a) * u).astype(BF16)

    def down(rows):
        f = jnp.dot(hm_ref[rows, :], wd_ref[...], preferred_element_type=F32)
        o_ref[rows, :] = o_ref[rows, :] + g2 * _rms(f, g[3:4])

    half = o_ref.shape[0] // 2
    ra, rb = slice(0, half), slice(half, 2 * half)
    hidden(ra, mix(ra))
    h2b = mix(rb)
    down(ra)
    hidden(rb, h2b)
    down(rb)


def _mix_ffn(ret, four, att, hy, x, mod, l, p, latent):
    B, L, _ = x.shape
    tm = _tiles(L)["tm_ffn"]
    row = (lambda b: b) if latent else (lambda b: CTX_ROW)
    bm = lambda w: pl.BlockSpec((None, tm, w), lambda b, i: (b, i, 0))
    tmj = pl.BlockSpec((tm, 256), lambda b, i: (i, b))
    resident = lambda shape: pl.BlockSpec((None,) + shape, lambda b, i: (l, 0, 0), pipeline_mode=pl.Buffered(1))
    return pl.pallas_call(
        _mix_ffn_kernel,
        grid=(B, L // tm),
        in_specs=[bm(RET_W), tmj, bm(MLA_W), tmj,
                  resident((4 * 256, D_MODEL)),
                  bm(D_MODEL),
                  pl.BlockSpec((None, None, 1, 6 * D_MODEL), lambda b, i: (l, row(b), 0, 0)),
                  pl.BlockSpec((None, 4, D_MODEL), lambda b, i: (l, 0, 0)),
                  resident((D_MODEL, D_FF)), resident((D_MODEL, D_FF)), resident((D_FF, D_MODEL))],
        out_specs=bm(D_MODEL),
        out_shape=jax.ShapeDtypeStruct((B, L, D_MODEL), F32),
        scratch_shapes=[pltpu.VMEM((tm, D_FF), BF16)],
        compiler_params=_cparams(("parallel", "parallel")),
        name="mix_ffn",
    )(ret, four, att, hy, p["w_out"], x, mod, p["norm_g"], p["w_gate"], p["w_up"], p["w_down"])


def _layer(x, mod, l, p, latent, ctx_kv, s0):
    B, L, _ = x.shape
    t = _tiles(L)
    qkvg, fx, q, k, v, hu, ckv, krp = _in_proj(x, mod, l, p, latent)
    ret, s_new = _retention(qkvg, p["dec_rows"], s0, l, emit_state=not latent)
    if _use_fft(L):
        four = _fft_fourier(fx, L)
    else:
        four = _mm(_bf16_const(_pos_dft_np(L)), fx.reshape(2 * L, B * FNET_W),
                   (FNET_GC * L) ** -0.5, BF16, t["hb"], t["tcol"], "fourier_pos")
    kc, vc = ctx_kv if latent else (None, None)
    att = _attention(q, k, v, kc, vc)
    hy = _hyena(hu, B, l, p)
    return _mix_ffn(ret, four, att, hy, x, mod, l, p, latent), ckv, krp, s_new


def _prep(w):
    last = lambda a, n: jnp.pad(a, [(0, 0)] * (a.ndim - 1) + [(0, n)])
    w_in = w["w_in"]
    zeros = lambda c: jnp.zeros((DEPTH, D_MODEL, c), F32)
    n0 = OFF_KR
    w_in_p = jnp.concatenate(
        [w_in[..., :n0], zeros(MLA_ROPE_OFF), w_in[..., n0:n0 + MLA_ROPE],
         zeros(MLA_HP - MLA_ROPE_OFF - MLA_ROPE), w_in[..., n0 + MLA_ROPE:]], axis=-1).astype(BF16)
    uq = w["mla_w_uq"].reshape(DEPTH, MLA_Q_LORA, MLA_HEADS, MLA_NOPE + MLA_ROPE)
    uq = last(uq, MLA_HP - MLA_NOPE - MLA_ROPE).reshape(DEPTH, MLA_Q_LORA, -1)
    ukv = w["mla_w_ukv"].reshape(DEPTH, MLA_KV_LORA, MLA_HEADS, MLA_NOPE + MLA_V)
    uk = last(ukv[..., :MLA_NOPE], MLA_HP - MLA_NOPE).reshape(DEPTH, MLA_KV_LORA, -1)
    uv = last(ukv[..., MLA_NOPE:], MLA_HP - MLA_V).reshape(DEPTH, MLA_KV_LORA, -1)
    two = lambda a: jnp.stack([jnp.kron(jnp.eye(2, dtype=F32), a[l]) for l in range(DEPTH)])
    hy_w1 = jnp.pad(w["hy_w1"], ((0, 0), (0, HY_EMB_PAD - HY_EMB), (0, 0)))
    return dict(
        norm_g=w["norm_g"], w_in=w_in_p, w_out=w["w_out"].astype(BF16),
        q_norm=w["mla_q_norm"][:, None], kv_norm=w["mla_kv_norm"][:, None],
        w_uq=uq.astype(BF16), w_ukv=jnp.concatenate([uk, uv], axis=-1).astype(BF16),
        dec_rows=w["dec_rows"],
        hy_short_w=w["hy_short_w"], hy_short_b=w["hy_short_b"][:, None],
        hy_w1=two(hy_w1), hy_b1=jnp.tile(w["hy_b1"][:, None], (1, 1, 2)),
        hy_w2=two(w["hy_w2"]), hy_b2=jnp.tile(w["hy_b2"][:, None], (1, 1, 2)), hy_w3=w["hy_w3"],
        hy_bias=w["hy_bias"],
        w_gate=w["w_gate"].astype(BF16), w_up=w["w_up"].astype(BF16), w_down=w["w_down"].astype(BF16),
    )


def kernel(x_prompt, x_sample, cache_ckv, cache_krope, state_ret, c, c_ctx, w_ada, b_ada, norm_g, w_in, w_out,
           ret_decay, mla_q_norm, mla_kv_norm, mla_w_uq, mla_w_ukv, hy_short_w, hy_short_b, hy_w1, hy_b1,
           hy_w2, hy_b2, hy_w3, hy_bias, w_gate, w_up, w_down):
    nb = c.shape[0]
    cvec = jnp.concatenate([c, c_ctx[None], jnp.zeros((MOD_ROWS - nb - 1, D_MODEL), F32)], axis=0)
    mod = _ada(cvec, w_ada, b_ada).reshape(DEPTH, MOD_ROWS, 1, 6 * D_MODEL)
    dec_rows = jnp.pad(jnp.repeat(ret_decay, RET_DV, axis=-1), ((0, 0), (0, 6), (0, 0)))
    krope_pad = jnp.pad(cache_krope, ((0, 0), (0, 0), (0, 0),
                                      (MLA_ROPE_OFF, MLA_HP - MLA_ROPE_OFF - MLA_ROPE)))
    w = dict(norm_g=norm_g, w_in=w_in, w_out=w_out, dec_rows=dec_rows, mla_q_norm=mla_q_norm,
             mla_kv_norm=mla_kv_norm, mla_w_uq=mla_w_uq, mla_w_ukv=mla_w_ukv, hy_short_w=hy_short_w,
             hy_short_b=hy_short_b, hy_w1=hy_w1, hy_b1=hy_b1, hy_w2=hy_w2, hy_b2=hy_b2, hy_w3=hy_w3,
             hy_bias=hy_bias, w_gate=w_gate, w_up=w_up, w_down=w_down)
    xp, xs = x_prompt, x_sample
    new_ckv, new_kr, new_s = [], [], []
    p = _prep(w)
    for l in range(DEPTH):
        xp, ckv, krp, s_ret = _layer(xp, mod, l, p, False, None, None)
        new_ckv.append(ckv)
        new_kr.append(krp[:, :, MLA_ROPE_OFF:MLA_ROPE_OFF + MLA_ROPE])
        new_s.append(s_ret)
        ctx_kv = _ctx_kv(cache_ckv, krope_pad, l, p["w_ukv"])
        xs, _, _, _ = _layer(xs, mod, l, p, True, ctx_kv, state_ret)
    return (xp, xs, jnp.stack(new_ckv, axis=1), jnp.stack(new_kr, axis=1), jnp.stack(new_s, axis=1))
```

```python
import functools
import math

import numpy as np
import jax
import jax.numpy as jnp
from jax import lax
from jax.experimental import pallas as pl
from jax.experimental.pallas import tpu as pltpu

F32 = jnp.float32
BF16 = jnp.bfloat16

D_MODEL = 1024
DEPTH = 2
GRID_W = 64
EPS = 1e-6
ROPE_BASE = 10000.0
RET_HEADS = 4
RET_DK = 64
RET_DV = 64
RET_W = RET_HEADS * RET_DV
FNET_GROUPS = 4
FNET_GC = 64
FNET_W = FNET_GROUPS * FNET_GC
MLA_HEADS = 4
MLA_Q_LORA = 256
MLA_KV_LORA = 128
MLA_NOPE = 64
MLA_ROPE = 32
MLA_V = 64
MLA_W = MLA_HEADS * MLA_V
MLA_HP = 128
MLA_ROPE_OFF = MLA_NOPE
ATTN_CK = 256
SUBL = 8
FFT_TN = 256
FFT_S = 64
HY_CH = 256
HY_ORDER = 2
HY_BANDS = 16
HY_EMB = 1 + 2 * HY_BANDS
HY_EMB_PAD = 128
HY_FFN = 64
HY_FAST_DECAY = 0.3
HY_SLOW_DECAY = 1.5
HY_TARGET = 1e-2
D_FF = ((8 * D_MODEL + 3 * 256 - 1) // (3 * 256)) * 256
FFN_CHUNK = 256
MOD_ROWS = 16
CTX_ROW = 8

OFF_RET = 0
OFF_FU = 4 * RET_W
OFF_CQ = OFF_FU + FNET_W
OFF_CKV = OFF_CQ + MLA_Q_LORA
OFF_KR = OFF_CKV + MLA_KV_LORA
OFF_HU = OFF_KR + MLA_HP
IN_WP = OFF_HU + 3 * HY_CH

VMEM_LIMIT = 52 * 1024 * 1024


def _cparams(sem):
    return pltpu.CompilerParams(dimension_semantics=sem, vmem_limit_bytes=VMEM_LIMIT)


def _tiles(L):
    return dict(
        tm=min(L, 512),
        tm_ffn=min(L, 512),
        chunk=min(L, 256),
        tq=min(L, 512),
        hb=min(L, 512),
        tcol=1024,
    )


def _bf16_const(a):
    return jnp.asarray(np.asarray(a, np.float32)).astype(BF16)


@functools.lru_cache(maxsize=None)
def _rope_np(L, width, seg_off, seg_w, rope_dim):
    row = np.repeat(np.arange(L // GRID_W), GRID_W).astype(np.float64)
    col = np.tile(np.arange(GRID_W), L // GRID_W).astype(np.float64)
    quarter = rope_dim // 4
    inv = ROPE_BASE ** (-np.arange(quarter, dtype=np.float64) / quarter)
    cos = np.ones((L, width), np.float64)
    sin = np.zeros((L, width), np.float64)
    for lane in range(width):
        r = lane % seg_w - seg_off
        if r < 0 or r >= rope_dim:
            continue
        pos = row if r < rope_dim // 2 else col
        ang = pos * inv[r % quarter]
        first = (r % (rope_dim // 2)) < quarter
        cos[:, lane] = np.cos(ang)
        sin[:, lane] = -np.sin(ang) if first else np.sin(ang)
    return cos.astype(np.float32), sin.astype(np.float32)


@functools.lru_cache(maxsize=None)
def _chan_dft_np():
    k = np.arange(FNET_GC)
    ang = 2.0 * np.pi * np.outer(k, k) / FNET_GC
    c = np.kron(np.eye(FNET_GROUPS), np.cos(ang))
    s = np.kron(np.eye(FNET_GROUPS), np.sin(ang))
    return np.concatenate([c, s], axis=1)


@functools.lru_cache(maxsize=None)
def _pos_dft_np(L):
    k = np.arange(L, dtype=np.int64)
    ang = 2.0 * np.pi * (np.outer(k, k) % L) / L
    return np.concatenate([np.cos(ang), -np.sin(ang)], axis=1)


@functools.lru_cache(maxsize=None)
def _hyena_feat_np(L):
    pos = np.arange(L, dtype=np.float64)
    t = pos / L
    bands = np.arange(1, HY_BANDS + 1, dtype=np.float64)
    ang = (2.0 * math.pi / L) * pos[:, None] * bands[None, :]
    z = np.zeros((L, HY_EMB_PAD), np.float64)
    z[:, 0] = t
    z[:, 1:1 + HY_BANDS] = np.sin(ang)
    z[:, 1 + HY_BANDS:HY_EMB] = np.cos(ang)
    deltas = np.abs(np.linspace(math.log(HY_TARGET) / HY_SLOW_DECAY,
                                math.log(HY_TARGET) / HY_FAST_DECAY, HY_CH))
    window = np.exp(-t[:, None] * deltas[None, :])
    return z.astype(np.float32), window.astype(np.float32)


@functools.lru_cache(maxsize=None)
def _hy_fwd_tabs_np(L, S):
    R, Q, D = 2 * L // S, L // S, S // 2
    c, a = np.arange(R)[:, None], np.arange(Q)[None, :]
    ang = 2 * np.pi * ((c * a) % R) / R
    f1 = np.concatenate([np.cos(ang), -np.sin(ang)], axis=0)
    cc, d, b = np.arange(R)[:, None, None], np.arange(D)[None, :, None], np.arange(S)[None, None, :]
    th = np.pi * (((cc + R * d) * b) % (2 * L)) / L
    ur = np.concatenate([np.cos(th), np.sin(th)], axis=2)
    us = np.concatenate([np.sin(th), -np.cos(th)], axis=2)
    us[0, 0, :S] = (-1.0) ** np.arange(S)
    us[0, 0, S:] = 0.0
    return f1, np.concatenate([ur, us], axis=1)


@functools.lru_cache(maxsize=None)
def _hy_inv_tabs_np(L, S):
    R, Q, D = 2 * L // S, L // S, S // 2
    cc, f, d = np.arange(R)[:, None, None], np.arange(S)[None, :, None], np.arange(D)[None, None, :]
    k = cc + R * d
    be = np.pi * ((k * f) % (2 * L)) / L
    w = np.where(k == 0, 1.0, 2.0) / (2.0 * L) + 0.0 * be
    bc = np.concatenate([w * np.cos(be), w * np.sin(be)], axis=2)
    bs = np.concatenate([-w * np.sin(be), w * np.cos(be)], axis=2)
    bc[0, :, D] = (-1.0) ** np.arange(S) / (2.0 * L)
    bs[0, :, D] = 0.0
    e, c = np.arange(Q)[:, None], np.arange(R)[None, :]
    al = 2 * np.pi * ((e * c) % R) / R
    return np.concatenate([bc, bs], axis=1), np.concatenate([np.cos(al), np.sin(al)], axis=1)


@functools.lru_cache(maxsize=None)
def _four_tabs_np(L, S):
    R = L // S
    c, a = np.arange(R)[:, None], np.arange(R)[None, :]
    ang = 2 * np.pi * ((c * a) % R) / R
    f1 = np.concatenate([np.concatenate([np.cos(ang), -np.sin(ang)], axis=1),
                         np.concatenate([-np.sin(ang), -np.cos(ang)], axis=1)], axis=0)
    cc, d, b = np.arange(R)[:, None, None], np.arange(S)[None, :, None], np.arange(S)[None, None, :]
    th = 2 * np.pi * (((cc + R * d) * b) % L) / L
    return f1, np.concatenate([np.cos(th), np.sin(th)], axis=2)


def _rms(x, g):
    return x * lax.rsqrt(jnp.mean(x * x, axis=-1, keepdims=True) + EPS) * g


def _silu(x):
    return x * jax.nn.sigmoid(x)


def _bdot(a, b):
    return jnp.dot(a.astype(BF16), b.astype(BF16), preferred_element_type=F32)


def _split(a):
    hi = a.astype(BF16)
    lo = (a - hi.astype(F32)).astype(BF16)
    return hi, lo


def _dot3(a, b):
    ah, al = _split(a)
    bh, bl = _split(b)
    d = lambda x, y: jnp.dot(x, y, preferred_element_type=F32)
    return d(ah, bh) + (d(al, bh) + d(ah, bl))


def _store_kv(kv, kr_pad, k_ref, v_ref):
    wk = MLA_HEADS * MLA_HP
    k_ref[...] = (kv[:, :wk] + jnp.concatenate([kr_pad] * MLA_HEADS, axis=1)).astype(k_ref.dtype)
    lane = lax.broadcasted_iota(jnp.int32, (1, wk), 1)
    ones = jnp.where(lane % MLA_HP >= MLA_V, 1.0, 0.0)
    v_ref[...] = (kv[:, wk:] + ones).astype(v_ref.dtype)


def _rope(x, cos, sin_signed, quarter):
    w = x.shape[-1]
    lane = lax.broadcasted_iota(jnp.int32, x.shape, 1)
    first = (lane % (2 * quarter)) < quarter
    up = pltpu.roll(x, w - quarter, axis=1)
    dn = pltpu.roll(x, quarter, axis=1)
    return x * cos + jnp.where(first, up, dn) * sin_signed


def _ada_kernel(c_ref, w_ref, b_ref, o_ref):
    s = _silu(c_ref[...])
    o_ref[...] = _bdot(s, w_ref[...]) + b_ref[...]


def _ada(cvec, w_ada, b_ada):
    tn = 1024
    n6 = w_ada.shape[-1]
    return pl.pallas_call(
        _ada_kernel,
        grid=(DEPTH, n6 // tn),
        in_specs=[
            pl.BlockSpec((MOD_ROWS, D_MODEL), lambda l, j: (0, 0)),
            pl.BlockSpec((None, D_MODEL, tn), lambda l, j: (l, 0, j)),
            pl.BlockSpec((None, 1, tn), lambda l, j: (l, 0, j)),
        ],
        out_specs=pl.BlockSpec((None, MOD_ROWS, tn), lambda l, j: (l, 0, j)),
        out_shape=jax.ShapeDtypeStruct((DEPTH, MOD_ROWS, n6), F32),
        compiler_params=_cparams(("parallel", "parallel")),
        name="ada",
    )(cvec, w_ada, b_ada.reshape(DEPTH, 1, n6))


def _in_proj_kernel(*refs, latent):
    (x_ref, mod_ref, g0_ref, w_ref, qn_ref, kvn_ref, wuq_ref, wukv_ref, bd_ref) = refs[:9]
    pos = 9
    if latent:
        cr_ref, sr_ref, cm_ref, sm_ref = refs[pos:pos + 4]
        pos += 4
        qkvg_ref, fx_ref, q_ref, k_ref, v_ref, hu_ref = refs[pos:pos + 6]
        lanes = lambda ref, reps: jnp.concatenate([ref[...]] * reps, axis=1)
    else:
        qkvg_ref, fx_ref, q_ref, k_ref, v_ref, hu_ref, ckv_ref, kr_ref = refs[pos:pos + 8]

    x = x_ref[...]
    mod = mod_ref[...]
    sh1 = mod[:, 0:D_MODEL]
    sc1 = mod[:, D_MODEL:2 * D_MODEL]
    h = _rms(x, g0_ref[0:1, :]) * (1.0 + sc1) + sh1
    proj = jnp.dot(h.astype(BF16), w_ref[...], preferred_element_type=F32)

    rq = proj[:, OFF_RET:OFF_RET + RET_W]
    rk = proj[:, OFF_RET + RET_W:OFF_RET + 2 * RET_W] * (RET_DK ** -0.5)
    if latent:
        cos_r, sin_r = lanes(cr_ref, RET_W // 128), lanes(sr_ref, RET_W // 128)
        rq = _rope(rq, cos_r, sin_r, RET_DK // 4)
        rk = _rope(rk, cos_r, sin_r, RET_DK // 4)
    qkvg_ref[:, 0:RET_W] = rq
    qkvg_ref[:, RET_W:2 * RET_W] = rk
    qkvg_ref[:, 2 * RET_W:4 * RET_W] = proj[:, OFF_RET + 2 * RET_W:OFF_RET + 4 * RET_W]

    fcs = _bdot(proj[:, OFF_FU:OFF_FU + FNET_W], bd_ref[...])
    fx_ref[0] = fcs[:, :FNET_W].astype(fx_ref.dtype)
    fx_ref[1] = fcs[:, FNET_W:].astype(fx_ref.dtype)

    cqn = _rms(proj[:, OFF_CQ:OFF_CQ + MLA_Q_LORA], qn_ref[...])
    q = _bdot(cqn, wuq_ref[...])
    ckvn = _rms(proj[:, OFF_CKV:OFF_CKV + MLA_KV_LORA], kvn_ref[...])
    kv = _bdot(ckvn, wukv_ref[...])
    krp = proj[:, OFF_KR:OFF_KR + MLA_HP]
    if latent:
        q = _rope(q, lanes(cm_ref, MLA_HEADS), lanes(sm_ref, MLA_HEADS), MLA_ROPE // 4)
        krp = _rope(krp, cm_ref[...], sm_ref[...], MLA_ROPE // 4)
    else:
        ckv_ref[...] = ckvn
        kr_ref[...] = krp
    q_ref[...] = (q * ((MLA_NOPE + MLA_ROPE) ** -0.5 * math.log2(math.e))).astype(q_ref.dtype)
    _store_kv(kv, krp, k_ref, v_ref)

    hu_ref[...] = proj[:, OFF_HU:OFF_HU + 3 * HY_CH]


def _in_proj(x, mod, l, p, latent):
    B, L, _ = x.shape
    t = _tiles(L)
    tm = t["tm"]
    nt = L // tm
    row = (lambda b: b) if latent else (lambda b: CTX_ROW)
    full = lambda shape: pl.BlockSpec((None,) + shape, lambda i, b: (l,) + (0,) * len(shape),
                                      pipeline_mode=pl.Buffered(1))
    in_specs = [
        pl.BlockSpec((None, tm, D_MODEL), lambda i, b: (b, i, 0)),
        pl.BlockSpec((None, None, 1, 6 * D_MODEL), lambda i, b: (l, row(b), 0, 0)),
        full((4, D_MODEL)),
        full((D_MODEL, IN_WP)),
        full((1, MLA_Q_LORA)),
        full((1, MLA_KV_LORA)),
        full((MLA_Q_LORA, MLA_HEADS * MLA_HP)),
        full((MLA_KV_LORA, 2 * MLA_HEADS * MLA_HP)),
        pl.BlockSpec((FNET_W, 2 * FNET_W), lambda i, b: (0, 0), pipeline_mode=pl.Buffered(1)),
    ]
    args = [x, mod, p["norm_g"], p["w_in"], p["q_norm"], p["kv_norm"], p["w_uq"], p["w_ukv"],
            _bf16_const(_chan_dft_np())]
    if latent:
        tabs = (_rope_np(L, 128, 0, RET_DK, RET_DK)
                + _rope_np(L, MLA_HP, MLA_ROPE_OFF, MLA_HP, MLA_ROPE))
        for tab in tabs:
            in_specs.append(pl.BlockSpec((tm, tab.shape[1]), lambda i, b: (i, 0)))
            args.append(jnp.asarray(tab))
    bm = lambda w: pl.BlockSpec((None, tm, w), lambda i, b: (b, i, 0))
    sds = jax.ShapeDtypeStruct
    out_specs = [
        bm(4 * RET_W),
        pl.BlockSpec((2, tm, FNET_W), lambda i, b: (0, i, b)),
        bm(MLA_HEADS * MLA_HP), bm(MLA_HEADS * MLA_HP), bm(MLA_HEADS * MLA_HP),
        pl.BlockSpec((tm, 3 * HY_CH), lambda i, b: (i, b)),
    ]
    out_shape = [
        sds((B, L, 4 * RET_W), F32),
        sds((2, L, B * FNET_W), F32 if _use_fft(L) else BF16),
        sds((B, L, MLA_HEADS * MLA_HP), BF16),
        sds((B, L, MLA_HEADS * MLA_HP), BF16),
        sds((B, L, MLA_HEADS * MLA_HP), BF16),
        sds((L, B * 3 * HY_CH), F32),
    ]
    if not latent:
        out_specs += [bm(MLA_KV_LORA), bm(MLA_HP)]
        out_shape += [sds((B, L, MLA_KV_LORA), F32), sds((B, L, MLA_HP), F32)]
    outs = pl.pallas_call(
        functools.partial(_in_proj_kernel, latent=latent),
        grid=(nt, B), in_specs=in_specs, out_specs=out_specs, out_shape=out_shape,
        compiler_params=_cparams(("parallel", "parallel")),
        name="in_proj_lat" if latent else "in_proj_ctx",
    )(*args)
    return tuple(outs) + ((None, None) if latent else ())


def _ctx_kv_kernel(ckv_ref, kr_ref, w_ref, k_ref, v_ref):
    _store_kv(_bdot(ckv_ref[...], w_ref[...]), kr_ref[...], k_ref, v_ref)


def _ctx_kv(cache_ckv, krope_pad, l, w_ukv):
    B, _, P, _ = cache_ckv.shape
    return pl.pallas_call(
        _ctx_kv_kernel,
        grid=(B,),
        in_specs=[
            pl.BlockSpec((None, None, P, MLA_KV_LORA), lambda b: (b, l, 0, 0)),
            pl.BlockSpec((None, None, P, MLA_HP), lambda b: (b, l, 0, 0)),
            pl.BlockSpec((None,) + w_ukv.shape[1:], lambda b: (l, 0, 0)),
        ],
        out_specs=[
            pl.BlockSpec((None, P, MLA_HEADS * MLA_HP), lambda b: (b, 0, 0)),
            pl.BlockSpec((None, P, MLA_HEADS * MLA_HP), lambda b: (b, 0, 0)),
        ],
        out_shape=[
            jax.ShapeDtypeStruct((B, P, MLA_HEADS * MLA_HP), BF16),
            jax.ShapeDtypeStruct((B, P, MLA_HEADS * MLA_HP), BF16),
        ],
        compiler_params=_cparams(("parallel",)),
        name="ctx_kv",
    )(cache_ckv, krope_pad, w_ukv)


def _ret_kernel(*refs, L, C, has_s0, emit_state):
    q_ref, k_ref, v_ref, g_ref, dec_ref = refs[:5]
    pos = 5
    if has_s0:
        s0_ref = refs[pos]
        pos += 1
    o_ref = refs[pos]
    pos += 1
    if emit_state:
        sn_ref = refs[pos]
        pos += 1
    sf_ref, sb_ref, dm_ref, tab_ref = refs[pos:pos + 4]
    n = L // C
    H, W = RET_HEADS, RET_W

    d = dec_ref[0:2, :]
    lg = jnp.minimum(d, 0.0) - jnp.log1p(jnp.exp(-jnp.abs(d)))
    lf, lb = lg[0:1, :], lg[1:2, :]
    @pl.when(pl.program_id(0) == 0)
    def _():
        ii = lax.broadcasted_iota(jnp.int32, (C, C), 0)
        jj = lax.broadcasted_iota(jnp.int32, (C, C), 1)
        diff = (ii - jj).astype(F32)
        for h in range(H):
            lfh = lf[:, h * RET_DV:h * RET_DV + 1]
            lbh = lb[:, h * RET_DV:h * RET_DV + 1]
            dm_ref[h] = jnp.where(diff > 0, jnp.exp(diff * lfh),
                                  jnp.where(diff < 0, jnp.exp(-diff * lbh), 2.0))
        ci = lax.broadcasted_iota(jnp.int32, (C, W), 0).astype(F32)
        tab_ref[0] = jnp.exp((ci + 1.0) * lf)
        tab_ref[1] = jnp.exp((C - ci) * lb)
        tab_ref[2] = jnp.exp((C - 1.0 - ci) * lf)
        tab_ref[3] = jnp.exp(ci * lb)

    g_f = jnp.exp(C * lf)
    g_b = jnp.exp(C * lb)
    hr = lax.broadcasted_iota(jnp.int32, (W, W), 0) // RET_DK
    hc = lax.broadcasted_iota(jnp.int32, (W, W), 1) // RET_DV
    same_head = hr == hc
    avg = jnp.where(same_head, 1.0 / RET_DV, 0.0).astype(BF16)

    sf_ref[0] = jnp.zeros((W, W), F32)
    sb_ref[n] = jnp.zeros((W, W), F32)
    if has_s0:
        for h in range(H):
            blk = slice(h * RET_DK, (h + 1) * RET_DK)
            sf_ref[0, blk, blk] = s0_ref[0, h]
            sb_ref[n, blk, blk] = s0_ref[1, h]

    tdot = lambda a, b: lax.dot_general(a.astype(BF16), b.astype(BF16), (((0,), (0,)), ((), ())),
                                        preferred_element_type=F32)

    def states(j, carry):
        jb = n - 1 - j
        rf = pl.multiple_of(j * C, C)
        rb = pl.multiple_of(jb * C, C)
        upd = tdot(k_ref[pl.ds(rf, C), :] * tab_ref[2], v_ref[pl.ds(rf, C), :])
        sf_ref[j + 1] = g_f * sf_ref[j] + jnp.where(same_head, upd, 0.0)
        upd = tdot(k_ref[pl.ds(rb, C), :] * tab_ref[3], v_ref[pl.ds(rb, C), :])
        sb_ref[jb] = g_b * sb_ref[jb + 1] + jnp.where(same_head, upd, 0.0)
        return carry

    lax.fori_loop(0, n, states, 0, unroll=4 if n % 4 == 0 else 1)

    lane_head = lax.broadcasted_iota(jnp.int32, (C, W), 1) // RET_DV

    def group_mean(x):
        hi, lo = _split(x)
        return (jnp.dot(hi, avg, preferred_element_type=F32)
                + jnp.dot(lo, avg, preferred_element_type=F32))

    def chunk(j, carry):
        r0 = pl.multiple_of(j * C, C)
        q = q_ref[pl.ds(r0, C), :]
        qb = q.astype(BF16)
        kb = k_ref[pl.ds(r0, C), :].astype(BF16)
        vb = v_ref[pl.ds(r0, C), :].astype(BF16)
        o = tab_ref[0] * jnp.dot(qb, sf_ref[j].astype(BF16), preferred_element_type=F32)
        o = o + tab_ref[1] * jnp.dot(qb, sb_ref[j + 1].astype(BF16), preferred_element_type=F32)
        inner = jnp.zeros((C, W), F32)
        for h in range(H):
            in_head = lane_head == h
            qh = jnp.where(in_head, q, 0.0).astype(BF16)
            s = lax.dot_general(qh, kb, (((1,), (1,)), ((), ())), preferred_element_type=F32)
            oh = jnp.dot((s * dm_ref[h]).astype(BF16), vb, preferred_element_type=F32)
            inner = jnp.where(in_head, oh, inner)
        o = o + inner
        dlt = o - group_mean(o)
        var = group_mean(dlt * dlt)
        gate = g_ref[pl.ds(r0, C), :]
        o_ref[pl.ds(r0, C), :] = (_silu(gate) * (dlt * lax.rsqrt(var + EPS))).astype(o_ref.dtype)
        return carry

    lax.fori_loop(0, n, chunk, 0, unroll=4 if n % 4 == 0 else 1)

    if emit_state:
        for h in range(H):
            blk = slice(h * RET_DK, (h + 1) * RET_DK)
            sn_ref[0, h] = sf_ref[n, blk, blk]
            sn_ref[1, h] = sb_ref[0, blk, blk]


def _retention(qkvg, dec_rows, s0, l, emit_state):
    B, L, _ = qkvg.shape
    C = _tiles(L)["chunk"]
    n = L // C
    has_s0 = s0 is not None
    def colspec(j):
        return pl.BlockSpec((None, L, RET_W), lambda b: (b, 0, j))

    in_specs = [colspec(j) for j in range(4)]
    in_specs.append(pl.BlockSpec((None, 8, RET_W), lambda b: (l, 0, 0)))
    args = [qkvg, qkvg, qkvg, qkvg, dec_rows]
    st_block = (None, None, 2, RET_HEADS, RET_DK, RET_DV)
    if has_s0:
        in_specs.append(pl.BlockSpec(st_block, lambda b: (b, l, 0, 0, 0, 0)))
        args.append(s0)
    out_specs = [pl.BlockSpec((None, L, RET_W), lambda b: (b, 0, 0))]
    out_shape = [jax.ShapeDtypeStruct((B, L, RET_W), BF16)]
    if emit_state:
        out_specs.append(pl.BlockSpec((None, 2, RET_HEADS, RET_DK, RET_DV), lambda b: (b, 0, 0, 0, 0)))
        out_shape.append(jax.ShapeDtypeStruct((B, 2, RET_HEADS, RET_DK, RET_DV), F32))
    outs = pl.pallas_call(
        functools.partial(_ret_kernel, L=L, C=C, has_s0=has_s0, emit_state=emit_state),
        grid=(B,), in_specs=in_specs, out_specs=out_specs, out_shape=out_shape,
        scratch_shapes=[
            pltpu.VMEM((n + 1, RET_W, RET_W), F32),
            pltpu.VMEM((n + 1, RET_W, RET_W), F32),
            pltpu.VMEM((RET_HEADS, C, C), F32),
            pltpu.VMEM((4, C, RET_W), F32),
        ],
        compiler_params=_cparams(("arbitrary",)),
        name="retention_lat" if has_s0 else "retention_ctx",
    )(*args)
    return outs if emit_state else (outs[0], None)


def _mm_kernel(a_ref, b_ref, o_ref, *, scale):
    o_ref[...] = (jnp.dot(a_ref[...], b_ref[...], preferred_element_type=F32) * scale).astype(o_ref.dtype)


def _mm(a, b, scale, out_dtype, tm, tn, name):
    M, K = a.shape
    _, N = b.shape
    tm, tn = min(tm, M), min(tn, N)
    return pl.pallas_call(
        functools.partial(_mm_kernel, scale=scale),
        grid=(N // tn, M // tm),
        in_specs=[pl.BlockSpec((tm, K), lambda j, i: (i, 0)),
                  pl.BlockSpec((K, tn), lambda j, i: (0, j))],
        out_specs=pl.BlockSpec((tm, tn), lambda j, i: (i, j)),
        out_shape=jax.ShapeDtypeStruct((M, N), out_dtype),
        compiler_params=_cparams(("parallel", "parallel")),
        name=name,
    )(a, b)


def _attn_kernel(*refs, has_ctx):
    if has_ctx:
        q_ref, k_ref, v_ref, kc_ref, vc_ref, o_ref, s_ref = refs
        srcs = ((k_ref, v_ref), (kc_ref, vc_ref))
    else:
        q_ref, k_ref, v_ref, o_ref, s_ref = refs
        srcs = ((k_ref, v_ref),)
    nt = lambda a, b: lax.dot_general(a, b, (((1,), (1,)), ((), ())), preferred_element_type=F32)
    chunks = [(kr, vr, c0) for kr, vr in srcs for c0 in range(0, kr.shape[0], ATTN_CK)]
    def scores(h):
        hq = slice(h * MLA_HP, (h + 1) * MLA_HP)
        qh = q_ref[:, hq]
        mx = None
        for n, (kr, _, c0) in enumerate(chunks):
            s = nt(qh, kr[c0:c0 + ATTN_CK, hq])
            s_ref[h % 2, :, n * ATTN_CK:(n + 1) * ATTN_CK] = s
            for g in range(ATTN_CK // 128):
                part = s[:, g * 128:(g + 1) * 128]
                mx = part if mx is None else jnp.maximum(mx, part)
        return jnp.max(mx, axis=-1, keepdims=True)

    def weighted_values(h, m):
        hq = slice(h * MLA_HP, (h + 1) * MLA_HP)
        acc = None
        for n, (_, vr, c0) in enumerate(chunks):
            p = jnp.exp2(s_ref[h % 2, :, n * ATTN_CK:(n + 1) * ATTN_CK] - m).astype(BF16)
            d = jnp.dot(p, vr[c0:c0 + ATTN_CK, hq], preferred_element_type=F32)
            acc = d if acc is None else acc + d
        o_ref[:, h * MLA_V:(h + 1) * MLA_V] = (acc[:, :MLA_V] / acc[:, MLA_V:2 * MLA_V]).astype(o_ref.dtype)

    m = scores(0)
    for h in range(MLA_HEADS):
        m_next = scores(h + 1) if h + 1 < MLA_HEADS else None
        weighted_values(h, m)
        m = m_next


def _attention(q, k, v, kc, vc):
    B, L, _ = q.shape
    tq = _tiles(L)["tq"]
    has_ctx = kc is not None
    whole = lambda a: pl.BlockSpec((None,) + a.shape[1:], lambda b, i: (b, 0, 0))
    in_specs = [pl.BlockSpec((None, tq, q.shape[2]), lambda b, i: (b, i, 0)), whole(k), whole(v)]
    args = [q, k, v]
    n_keys = L
    if has_ctx:
        in_specs += [whole(kc), whole(vc)]
        args += [kc, vc]
        n_keys += kc.shape[1]
    return pl.pallas_call(
        functools.partial(_attn_kernel, has_ctx=has_ctx),
        grid=(B, L // tq), in_specs=in_specs,
        out_specs=pl.BlockSpec((None, tq, MLA_W), lambda b, i: (b, i, 0)),
        out_shape=jax.ShapeDtypeStruct((B, L, MLA_W), BF16),
        scratch_shapes=[pltpu.VMEM((2, tq, n_keys), F32)],
        compiler_params=_cparams(("parallel", "parallel")),
        name="attention_lat" if has_ctx else "attention_ctx",
    )(*args)


def _hy_filter_kernel(z_ref, w1_ref, b1_ref, w2_ref, b2_ref, w3_ref, win_ref, x_ref):
    h1 = jnp.sin(_dot3(z_ref[...], w1_ref[...]) + b1_ref[...])
    h2 = jnp.sin(_dot3(h1, w2_ref[...]) + b2_ref[...])
    w3 = w3_ref[...]
    h = jnp.concatenate([_dot3(h2[:, :HY_FFN], w3), _dot3(h2[:, HY_FFN:], w3)], axis=0)
    win = win_ref[...]
    row = lax.broadcasted_iota(jnp.int32, win.shape, 0)
    for o in range(HY_ORDER):
        base = o * 2 * HY_CH
        hf = h[:, base:base + HY_CH] * win
        hb = h[:, base + HY_CH:base + 2 * HY_CH] * win
        hb = jnp.where(row == 0, 0.0, hb)
        nrm = jnp.sum(jnp.abs(hf), axis=0, keepdims=True) + jnp.sum(jnp.abs(hb), axis=0, keepdims=True) + EPS
        inv = 1.0 / nrm
        x_ref[:, base:base + HY_CH] = (hf + hb) * inv
        x_ref[:, base + HY_CH:base + 2 * HY_CH] = (hf - hb) * inv


def _hy_filter(L, l, p):
    z, win = _hyena_feat_np(L)
    z2 = np.concatenate([z[:L // 2], z[L // 2:]], axis=1)
    full = lambda shape: pl.BlockSpec(shape, lambda i: (0,) * len(shape))
    layer = lambda shape: pl.BlockSpec((None,) + shape, lambda i: (l,) + (0,) * len(shape))
    return pl.pallas_call(
        _hy_filter_kernel,
        grid=(1,),
        in_specs=[full((L // 2, 2 * HY_EMB_PAD)), layer((2 * HY_EMB_PAD, 2 * HY_FFN)), layer((1, 2 * HY_FFN)),
                  layer((2 * HY_FFN, 2 * HY_FFN)), layer((1, 2 * HY_FFN)),
                  layer((HY_FFN, HY_ORDER * 2 * HY_CH)), full((L, HY_CH))],
        out_specs=full((L, HY_ORDER * 2 * HY_CH)),
        out_shape=jax.ShapeDtypeStruct((L, HY_ORDER * 2 * HY_CH), F32),
        compiler_params=_cparams(("arbitrary",)),
        name="hy_filter",
    )(jnp.asarray(z2), p["hy_w1"], p["hy_b1"], p["hy_w2"], p["hy_b2"], p["hy_w3"], jnp.asarray(win))


def _tile_rows(ref, lead, rows):
    return jnp.concatenate([ref[lead, r * SUBL:(r + 1) * SUBL, :] for r in rows], axis=0)


def _fft_fwd_kernel(*refs, Q, R, S, filt):
    if filt:
        u_ref, k1_ref, e_ref, y_ref, a_scr = refs
    else:
        u_ref, k1_ref, e_ref, ga_ref, gb_ref, ny_ref, y_ref, a_scr = refs
    D = S // 2
    nb = S // SUBL
    tn = u_ref.shape[-1]
    dot = lambda p, q: jnp.dot(p, q, preferred_element_type=F32)
    k1 = k1_ref[...]
    for bh in range(nb):
        x = u_ref[:, bh].reshape(Q * SUBL, tn)
        if filt:
            xh, xl = _split(x)
            a_scr[bh] = dot(k1, xh) + dot(k1, xl)
        else:
            a_scr[bh] = dot(k1, x.astype(BF16))
    for c in range(R):
        rhs = jnp.concatenate([_tile_rows(a_scr, bh, (part * R + c,)) for part in (0, 1) for bh in range(nb)],
                              axis=0)
        if filt:
            rh, rl = _split(rhs)
            y_ref[c] = dot(e_ref[c], rh) + dot(e_ref[c], rl)
            continue
        y = dot(e_ref[c], rhs.astype(BF16))
        ur, us = y[:D], y[D:]
        ga, gb = ga_ref[c], gb_ref[c]
        gd = ga
        if c == 0:
            row0 = lax.broadcasted_iota(jnp.int32, ga.shape, 0) == 0
            gb = jnp.where(row0, 0.0, gb)
            gd = jnp.where(row0, jnp.broadcast_to(ny_ref[0, 0:1, :], ga.shape), ga)
        y_ref[c, :D, :] = (ur * ga - us * gb).astype(y_ref.dtype)
        y_ref[c, D:, :] = (ur * gb + us * gd).astype(y_ref.dtype)


def _fft_filter(taps, L):
    S = _hyena_s(L)
    R, Q = 2 * L // S, L // S
    N = taps.shape[1]
    tn = FFT_TN
    nb = S // SUBL
    f1_np, e_np = _hy_fwd_tabs_np(L, S)
    const = lambda shape: pl.BlockSpec(shape, lambda j: (0,) * len(shape), pipeline_mode=pl.Buffered(1))
    return pl.pallas_call(
        functools.partial(_fft_fwd_kernel, Q=Q, R=R, S=S, filt=True),
        grid=(N // tn,),
        in_specs=[pl.BlockSpec((Q, nb, SUBL, tn), lambda j: (0, 0, 0, j)),
                  const((2 * R * SUBL, Q * SUBL)), const((R, S, 2 * S))],
        out_specs=pl.BlockSpec((R, S, tn), lambda j: (0, 0, j)),
        out_shape=jax.ShapeDtypeStruct((R, S, N), F32),
        scratch_shapes=[pltpu.VMEM((nb, 2 * R * SUBL, tn), F32)],
        compiler_params=_cparams(("parallel",)),
        name="fft_filter",
    )(taps.reshape(Q, nb, SUBL, N), _bf16_const(np.kron(f1_np, np.eye(SUBL))), _bf16_const(e_np))


def _fft_inv_kernel(y_ref, e_ref, k2_ref, gate_ref, z_ref, d_ref, o_ref, scr, *, Q, R, S):
    dot = lambda p, q: jnp.dot(p, q, preferred_element_type=F32)
    tn = y_ref.shape[-1]
    per = scr.shape[1] // (2 * S)
    for c in range(R):
        scr[c // per, (c % per) * 2 * S:(c % per + 1) * 2 * S, :] = dot(e_ref[c], y_ref[c])
    k2 = k2_ref[...]
    d = d_ref[...]
    for fh in range(S // SUBL):
        rhs = jnp.concatenate([_tile_rows(scr, c // per, ((c % per) * 2 * S // SUBL + part * (S // SUBL) + fh,))
                               for part in (0, 1) for c in range(R)], axis=0).astype(BF16)
        conv = dot(k2, rhs).reshape(Q, SUBL, tn)
        o_ref[:, fh] = (gate_ref[:, fh] * (conv + d * z_ref[:, fh])).astype(o_ref.dtype)


def _hy_fused_kernel(hu_ref, w_ref, b_ref, k1_ref, e_ref, ei_ref, k2_ref, ga0, gb0, ny0, ga1, gb1, ny1,
                     bias_ref, o_ref, v4, x14, x24, z14, y_scr, scr, *, Q, R, S):
    L, tn = hu_ref.shape[0], HY_CH
    w, b = w_ref[...], b_ref[...]
    row = lax.broadcasted_iota(jnp.int32, (L, tn), 0)
    for g, dst in enumerate((v4, x14, x24)):
        cs = slice(g * tn, (g + 1) * tn)
        u = hu_ref[:, cs]
        prev = jnp.where(row == 0, 0.0, pltpu.roll(u, 1, axis=0))
        nxt = jnp.where(row == L - 1, 0.0, pltpu.roll(u, L - 1, axis=0))
        uc = prev * w[0:1, cs] + u * w[1:2, cs] + nxt * w[2:3, cs] + b[:, cs]
        dst[...] = uc.reshape(dst.shape)
    fwd = functools.partial(_fft_fwd_kernel, Q=Q, R=R, S=S, filt=False)
    inv = functools.partial(_fft_inv_kernel, Q=Q, R=R, S=S)
    fwd(v4, k1_ref, e_ref, ga0, gb0, ny0, y_scr, scr)
    inv(y_scr, ei_ref, k2_ref, x14, v4, bias_ref.at[0:1], z14, scr)
    fwd(z14, k1_ref, e_ref, ga1, gb1, ny1, y_scr, scr)
    inv(y_scr, ei_ref, k2_ref, x24, z14, bias_ref.at[1:2], o_ref, scr)


def _hy_fused(hu, B, spec, l, p):
    layer = lambda shape: pl.BlockSpec((None,) + shape, lambda j: (l,) + (0,) * len(shape),
                                       pipeline_mode=pl.Buffered(1))
    L = hu.shape[0]
    S = _hyena_s(L)
    R, Q, D = 2 * L // S, L // S, S // 2
    nb = S // SUBL
    tn = HY_CH
    f1_np, e_np = _hy_fwd_tabs_np(L, S)
    ei_np, f2_np = _hy_inv_tabs_np(L, S)
    const = lambda shape: pl.BlockSpec(shape, lambda j: (0,) * len(shape), pipeline_mode=pl.Buffered(1))
    one = lambda blk, idx: pl.BlockSpec(blk, lambda j: idx, pipeline_mode=pl.Buffered(1))
    g_specs = [s for o in range(HY_ORDER) for s in (one((R, D, tn), (0, 0, 2 * o)),
                                                    one((R, D, tn), (0, 1, 2 * o + 1)),
                                                    one((1, 8, tn), (0, D // 8, 2 * o)))]
    work = pltpu.VMEM((Q, nb, SUBL, tn), F32)
    out = pl.pallas_call(
        functools.partial(_hy_fused_kernel, Q=Q, R=R, S=S),
        grid=(B,),
        in_specs=[pl.BlockSpec((L, 3 * HY_CH), lambda j: (0, j)), layer((3, 3 * HY_CH)), layer((1, 3 * HY_CH)),
                  const((2 * R * SUBL, Q * SUBL)), const((R, S, 2 * S)), const((R, 2 * S, S)),
                  const((Q * SUBL, 2 * R * SUBL))] + g_specs + [layer((HY_ORDER, HY_CH))],
        out_specs=pl.BlockSpec((Q, nb, SUBL, tn), lambda j: (0, 0, 0, j)),
        out_shape=jax.ShapeDtypeStruct((Q, nb, SUBL, B * HY_CH), BF16),
        scratch_shapes=[work, work, work, work, pltpu.VMEM((R, S, tn), BF16),
                        pltpu.VMEM((nb, 2 * R * SUBL, tn), F32)],
        compiler_params=_cparams(("parallel",)),
        name="hy_fused",
    )(hu, p["hy_short_w"], p["hy_short_b"], _bf16_const(np.kron(f1_np, np.eye(SUBL))), _bf16_const(e_np),
      _bf16_const(ei_np), _bf16_const(np.kron(f2_np, np.eye(SUBL))), *([spec] * 6), p["hy_bias"])
    return out.reshape(L, B * HY_CH)


def _fft_four_kernel(x_ref, k1_ref, e2_ref, o_ref, a_scr, *, R, S, scale):
    dot = lambda p, q: jnp.dot(p, q, preferred_element_type=F32)
    tn = x_ref.shape[-1]
    nb = S // SUBL
    k1 = k1_ref[...]
    for bh in range(nb):
        x = jnp.concatenate([x_ref[part, :, bh].reshape(R * SUBL, tn) for part in (0, 1)], axis=0)
        a_scr[bh] = dot(k1, x.astype(BF16))
    for ch in range(R // SUBL):
        rhs = jnp.concatenate([_tile_rows(a_scr, bh, (part * R + ch * SUBL + cl,))
                               for cl in range(SUBL) for part in (0, 1) for bh in range(nb)],
                              axis=0).astype(BF16)
        y = dot(e2_ref[ch], rhs)
        o_ref[:, ch] = (y.reshape(S, SUBL, tn) * scale).astype(o_ref.dtype)


def _fft_fourier(fx, L):
    S = FFT_S
    R = L // S
    N = fx.shape[2]
    tn = FFT_TN
    nb = S // SUBL
    f1_np, e_np = _four_tabs_np(L, S)
    e2 = np.zeros((R // SUBL, S, SUBL, SUBL, 2 * S))
    for cl in range(SUBL):
        e2[:, :, cl, cl, :] = e_np[cl::SUBL]
    e2 = e2.reshape(R // SUBL, S * SUBL, SUBL * 2 * S)
    const = lambda shape: pl.BlockSpec(shape, lambda j: (0,) * len(shape), pipeline_mode=pl.Buffered(1))
    out = pl.pallas_call(
        functools.partial(_fft_four_kernel, R=R, S=S, scale=(FNET_GC * L) ** -0.5),
        grid=(N // tn,),
        in_specs=[pl.BlockSpec((2, R, nb, SUBL, tn), lambda j: (0, 0, 0, 0, j)),
                  const((2 * R * SUBL, 2 * R * SUBL)), const(e2.shape)],
        out_specs=pl.BlockSpec((S, R // SUBL, SUBL, tn), lambda j: (0, 0, 0, j)),
        out_shape=jax.ShapeDtypeStruct((S, R // SUBL, SUBL, N), BF16),
        scratch_shapes=[pltpu.VMEM((nb, 2 * R * SUBL, tn), F32)],
        compiler_params=_cparams(("parallel",)),
        name="fft_fourier",
    )(fx.reshape(2, R, nb, SUBL, N), _bf16_const(np.kron(f1_np, np.eye(SUBL))), _bf16_const(e2))
    return out.reshape(L, N)


def _use_fft(L):
    return L // FFT_S >= SUBL


def _hyena_s(L):
    return 2 * FFT_S if (2 * L) % (2 * FFT_S * SUBL) == 0 else FFT_S


def _hyena(hu, B, l, p):
    L = hu.shape[0]
    return _hy_fused(hu, B, _fft_filter(_hy_filter(L, l, p), L), l, p)


def _mix_ffn_kernel(ret_ref, four_ref, att_ref, hy_ref, wo_ref, x_ref, mod_ref, g_ref,
                    wg_ref, wu_ref, wd_ref, o_ref, hm_ref):
    mod = mod_ref[...]
    g1 = mod[:, 2 * D_MODEL:3 * D_MODEL]
    sh2 = mod[:, 3 * D_MODEL:4 * D_MODEL]
    sc2 = mod[:, 4 * D_MODEL:5 * D_MODEL]
    g2 = mod[:, 5 * D_MODEL:6 * D_MODEL]
    g = g_ref[...]

    def mix(rows):
        d = lambda a, i: jnp.dot(a[rows, :], wo_ref[i * 256:(i + 1) * 256, :], preferred_element_type=F32)
        m = d(ret_ref, 0) + d(four_ref, 1) + d(att_ref, 2) + d(hy_ref, 3)
        x1 = x_ref[rows, :] + g1 * _rms(m, g[1:2])
        o_ref[rows, :] = x1
        return (_rms(x1, g[2:3]) * (1.0 + sc2) + sh2).astype(BF16)

    def hidden(rows, h2):
        for c in range(D_FF // FFN_CHUNK):
            cs = slice(c * FFN_CHUNK, (c + 1) * FFN_CHUNK)
            a = jnp.dot(h2, wg_ref[:, cs], preferred_element_type=F32)
            u = jnp.dot(h2, wu_ref[:, cs], preferred_element_type=F32)
            hm_ref[rows, cs] = (_silu(a) * u).astype(BF16)

    def down(rows):
        f = jnp.dot(hm_ref[rows, :], wd_ref[...], preferred_element_type=F32)
        o_ref[rows, :] = o_ref[rows, :] + g2 * _rms(f, g[3:4])

    half = o_ref.shape[0] // 2
    ra, rb = slice(0, half), slice(half, 2 * half)
    hidden(ra, mix(ra))
    h2b = mix(rb)
    down(ra)
    hidden(rb, h2b)
    down(rb)


def _mix_ffn(ret, four, att, hy, x, mod, l, p, latent):
    B, L, _ = x.shape
    tm = _tiles(L)["tm_ffn"]
    row = (lambda b: b) if latent else (lambda b: CTX_ROW)
    bm = lambda w: pl.BlockSpec((None, tm, w), lambda b, i: (b, i, 0))
    tmj = pl.BlockSpec((tm, 256), lambda b, i: (i, b))
    resident = lambda shape: pl.BlockSpec((None,) + shape, lambda b, i: (l, 0, 0), pipeline_mode=pl.Buffered(1))
    return pl.pallas_call(
        _mix_ffn_kernel,
        grid=(B, L // tm),
        in_specs=[bm(RET_W), tmj, bm(MLA_W), tmj,
                  resident((4 * 256, D_MODEL)),
                  bm(D_MODEL),
                  pl.BlockSpec((None, None, 1, 6 * D_MODEL), lambda b, i: (l, row(b), 0, 0)),
                  pl.BlockSpec((None, 4, D_MODEL), lambda b, i: (l, 0, 0)),
                  resident((D_MODEL, D_FF)), resident((D_MODEL, D_FF)), resident((D_FF, D_MODEL))],
        out_specs=bm(D_MODEL),
        out_shape=jax.ShapeDtypeStruct((B, L, D_MODEL), F32),
        scratch_shapes=[pltpu.VMEM((tm, D_FF), BF16)],
        compiler_params=_cparams(("parallel", "parallel")),
        name="mix_ffn",
    )(ret, four, att, hy, p["w_out"], x, mod, p["norm_g"], p["w_gate"], p["w_up"], p["w_down"])


def _layer(x, mod, l, p, latent, ctx_kv, s0):
    B, L, _ = x.shape
    t = _tiles(L)
    qkvg, fx, q, k, v, hu, ckv, krp = _in_proj(x, mod, l, p, latent)
    ret, s_new = _retention(qkvg, p["dec_rows"], s0, l, emit_state=not latent)
    if _use_fft(L):
        four = _fft_fourier(fx, L)
    else:
        four = _mm(_bf16_const(_pos_dft_np(L)), fx.reshape(2 * L, B * FNET_W),
                   (FNET_GC * L) ** -0.5, BF16, t["hb"], t["tcol"], "fourier_pos")
    kc, vc = ctx_kv if latent else (None, None)
    att = _attention(q, k, v, kc, vc)
    hy = _hyena(hu, B, l, p)
    return _mix_ffn(ret, four, att, hy, x, mod, l, p, latent), ckv, krp, s_new


def _pack_w_in_kernel(w_ref, o_ref):
    w = w_ref[...]
    o_ref[:, :OFF_KR] = w[:, :OFF_KR].astype(o_ref.dtype)
    tile = pltpu.roll(w[:, OFF_KR:OFF_KR + MLA_HP], MLA_ROPE_OFF, axis=1)
    lane = lax.broadcasted_iota(jnp.int32, tile.shape, 1)
    tile = jnp.where(lane >= MLA_ROPE_OFF, jnp.where(lane < MLA_ROPE_OFF + MLA_ROPE, tile, 0.0), 0.0)
    o_ref[:, OFF_KR:OFF_HU] = tile.astype(o_ref.dtype)
    o_ref[:, OFF_HU:] = w[:, OFF_KR + MLA_ROPE:].astype(o_ref.dtype)


def _pack_w_in(w_in):
    tr = 256
    return pl.pallas_call(
        _pack_w_in_kernel,
        grid=(DEPTH, D_MODEL // tr),
        in_specs=[pl.BlockSpec((None, tr, w_in.shape[-1]), lambda l, i: (l, i, 0))],
        out_specs=pl.BlockSpec((None, tr, IN_WP), lambda l, i: (l, i, 0)),
        out_shape=jax.ShapeDtypeStruct((DEPTH, D_MODEL, IN_WP), BF16),
        compiler_params=_cparams(("parallel", "parallel")),
        name="pack_w_in",
    )(w_in)


def _prep(w):
    last = lambda a, n: jnp.pad(a, [(0, 0)] * (a.ndim - 1) + [(0, n)])
    uq = w["mla_w_uq"].reshape(DEPTH, MLA_Q_LORA, MLA_HEADS, MLA_NOPE + MLA_ROPE)
    uq = last(uq, MLA_HP - MLA_NOPE - MLA_ROPE).reshape(DEPTH, MLA_Q_LORA, -1)
    ukv = w["mla_w_ukv"].reshape(DEPTH, MLA_KV_LORA, MLA_HEADS, MLA_NOPE + MLA_V)
    uk = last(ukv[..., :MLA_NOPE], MLA_HP - MLA_NOPE).reshape(DEPTH, MLA_KV_LORA, -1)
    uv = last(ukv[..., MLA_NOPE:], MLA_HP - MLA_V).reshape(DEPTH, MLA_KV_LORA, -1)
    two = lambda a: jnp.stack([jnp.kron(jnp.eye(2, dtype=F32), a[l]) for l in range(DEPTH)])
    hy_w1 = jnp.pad(w["hy_w1"], ((0, 0), (0, HY_EMB_PAD - HY_EMB), (0, 0)))
    return dict(
        norm_g=w["norm_g"], w_in=_pack_w_in(w["w_in"]), w_out=w["w_out"].astype(BF16),
        q_norm=w["mla_q_norm"][:, None], kv_norm=w["mla_kv_norm"][:, None],
        w_uq=uq.astype(BF16), w_ukv=jnp.concatenate([uk, uv], axis=-1).astype(BF16),
        dec_rows=w["dec_rows"],
        hy_short_w=w["hy_short_w"], hy_short_b=w["hy_short_b"][:, None],
        hy_w1=two(hy_w1), hy_b1=jnp.tile(w["hy_b1"][:, None], (1, 1, 2)),
        hy_w2=two(w["hy_w2"]), hy_b2=jnp.tile(w["hy_b2"][:, None], (1, 1, 2)), hy_w3=w["hy_w3"],
        hy_bias=w["hy_bias"],
        w_gate=w["w_gate"].astype(BF16), w_up=w["w_up"].astype(BF16), w_down=w["w_down"].astype(BF16),
    )


def kernel(x_prompt, x_sample, cache_ckv, cache_krope, state_ret, c, c_ctx, w_ada, b_ada, norm_g, w_in, w_out,
           ret_decay, mla_q_norm, mla_kv_norm, mla_w_uq, mla_w_ukv, hy_short_w, hy_short_b, hy_w1, hy_b1,
           hy_w2, hy_b2, hy_w3, hy_bias, w_gate, w_up, w_down):
    nb = c.shape[0]
    cvec = jnp.concatenate([c, c_ctx[None], jnp.zeros((MOD_ROWS - nb - 1, D_MODEL), F32)], axis=0)
    mod = _ada(cvec, w_ada, b_ada).reshape(DEPTH, MOD_ROWS, 1, 6 * D_MODEL)
    dec_rows = jnp.pad(jnp.repeat(ret_decay, RET_DV, axis=-1), ((0, 0), (0, 6), (0, 0)))
    krope_pad = jnp.pad(cache_krope, ((0, 0), (0, 0), (0, 0),
                                      (MLA_ROPE_OFF, MLA_HP - MLA_ROPE_OFF - MLA_ROPE)))
    w = dict(norm_g=norm_g, w_in=w_in, w_out=w_out, dec_rows=dec_rows, mla_q_norm=mla_q_norm,
             mla_kv_norm=mla_kv_norm, mla_w_uq=mla_w_uq, mla_w_ukv=mla_w_ukv, hy_short_w=hy_short_w,
             hy_short_b=hy_short_b, hy_w1=hy_w1, hy_b1=hy_b1, hy_w2=hy_w2, hy_b2=hy_b2, hy_w3=hy_w3,
             hy_bias=hy_bias, w_gate=w_gate, w_up=w_up, w_down=w_down)
    xp, xs = x_prompt, x_sample
    new_ckv, new_kr, new_s = [], [], []
    p = _prep(w)
    for l in range(DEPTH):
        xp, ckv, krp, s_ret = _layer(xp, mod, l, p, False, None, None)
        new_ckv.append(ckv)
        new_kr.append(krp[:, :, MLA_ROPE_OFF:MLA_ROPE_OFF + MLA_ROPE])
        new_s.append(s_ret)
        ctx_kv = _ctx_kv(cache_ckv, krope_pad, l, p["w_ukv"])
        xs, _, _, _ = _layer(xs, mod, l, p, True, ctx_kv, state_ret)
    return (xp, xs, jnp.stack(new_ckv, axis=1), jnp.stack(new_kr, axis=1), jnp.stack(new_s, axis=1))
```

```python
import functools
import math

import numpy as np
import jax
import jax.numpy as jnp
from jax import lax
from jax.experimental import pallas as pl
from jax.experimental.pallas import tpu as pltpu

F32 = jnp.float32
BF16 = jnp.bfloat16

D_MODEL = 1024
DEPTH = 2
GRID_W = 64
EPS = 1e-6
ROPE_BASE = 10000.0
RET_HEADS = 4
RET_DK = 64
RET_DV = 64
RET_W = RET_HEADS * RET_DV
FNET_GROUPS = 4
FNET_GC = 64
FNET_W = FNET_GROUPS * FNET_GC
MLA_HEADS = 4
MLA_Q_LORA = 256
MLA_KV_LORA = 128
MLA_NOPE = 64
MLA_ROPE = 32
MLA_V = 64
MLA_W = MLA_HEADS * MLA_V
MLA_HP = 128
MLA_ROPE_OFF = MLA_NOPE
ATTN_CK = 256
SUBL = 8
FFT_TN = 256
FFT_S = 64
HY_CH = 256
HY_ORDER = 2
HY_BANDS = 16
HY_EMB = 1 + 2 * HY_BANDS
HY_EMB_PAD = 128
HY_FFN = 64
HY_FAST_DECAY = 0.3
HY_SLOW_DECAY = 1.5
HY_TARGET = 1e-2
D_FF = ((8 * D_MODEL + 3 * 256 - 1) // (3 * 256)) * 256
FFN_CHUNK = 256
IN_GROUP = 256
MOD_ROWS = 16
CTX_ROW = 8

OFF_RET = 0
OFF_FU = 4 * RET_W
OFF_CQ = OFF_FU + FNET_W
OFF_CKV = OFF_CQ + MLA_Q_LORA
OFF_KR = OFF_CKV + MLA_KV_LORA
OFF_HU = OFF_KR + MLA_HP
IN_WP = OFF_HU + 3 * HY_CH

VMEM_LIMIT = 52 * 1024 * 1024


def _cparams(sem):
    return pltpu.CompilerParams(dimension_semantics=sem, vmem_limit_bytes=VMEM_LIMIT)


def _tiles(L):
    return dict(
        tm=min(L, 512),
        tm_ffn=min(L, 512),
        chunk=min(L, 256),
        tq=min(L, 1024),
        hb=min(L, 512),
        tcol=1024,
    )


def _bf16_const(a):
    return jnp.asarray(np.asarray(a, np.float32)).astype(BF16)


@functools.lru_cache(maxsize=None)
def _rope_np(L, width, seg_off, seg_w, rope_dim):
    row = np.repeat(np.arange(L // GRID_W), GRID_W).astype(np.float64)
    col = np.tile(np.arange(GRID_W), L // GRID_W).astype(np.float64)
    quarter = rope_dim // 4
    inv = ROPE_BASE ** (-np.arange(quarter, dtype=np.float64) / quarter)
    cos = np.ones((L, width), np.float64)
    sin = np.zeros((L, width), np.float64)
    for lane in range(width):
        r = lane % seg_w - seg_off
        if r < 0 or r >= rope_dim:
            continue
        pos = row if r < rope_dim // 2 else col
        ang = pos * inv[r % quarter]
        first = (r % (rope_dim // 2)) < quarter
        cos[:, lane] = np.cos(ang)
        sin[:, lane] = -np.sin(ang) if first else np.sin(ang)
    return cos.astype(np.float32), sin.astype(np.float32)


@functools.lru_cache(maxsize=None)
def _chan_dft_np():
    k = np.arange(FNET_GC)
    ang = 2.0 * np.pi * np.outer(k, k) / FNET_GC
    c = np.kron(np.eye(FNET_GROUPS), np.cos(ang))
    s = np.kron(np.eye(FNET_GROUPS), np.sin(ang))
    return np.concatenate([c, s], axis=1)


@functools.lru_cache(maxsize=None)
def _pos_dft_np(L):
    k = np.arange(L, dtype=np.int64)
    ang = 2.0 * np.pi * (np.outer(k, k) % L) / L
    return np.concatenate([np.cos(ang), -np.sin(ang)], axis=1)


@functools.lru_cache(maxsize=None)
def _hyena_feat_np(L):
    pos = np.arange(L, dtype=np.float64)
    t = pos / L
    bands = np.arange(1, HY_BANDS + 1, dtype=np.float64)
    ang = (2.0 * math.pi / L) * pos[:, None] * bands[None, :]
    z = np.zeros((L, HY_EMB_PAD), np.float64)
    z[:, 0] = t
    z[:, 1:1 + HY_BANDS] = np.sin(ang)
    z[:, 1 + HY_BANDS:HY_EMB] = np.cos(ang)
    deltas = np.abs(np.linspace(math.log(HY_TARGET) / HY_SLOW_DECAY,
                                math.log(HY_TARGET) / HY_FAST_DECAY, HY_CH))
    window = np.exp(-t[:, None] * deltas[None, :])
    return z.astype(np.float32), window.astype(np.float32)


@functools.lru_cache(maxsize=None)
def _hy_fwd_tabs_np(L, S):
    R, Q, D = 2 * L // S, L // S, S // 2
    c, a = np.arange(R)[:, None], np.arange(Q)[None, :]
    ang = 2 * np.pi * ((c * a) % R) / R
    f1 = np.concatenate([np.cos(ang), -np.sin(ang)], axis=0)
    cc, d, b = np.arange(R)[:, None, None], np.arange(D)[None, :, None], np.arange(S)[None, None, :]
    th = np.pi * (((cc + R * d) * b) % (2 * L)) / L
    ur = np.concatenate([np.cos(th), np.sin(th)], axis=2)
    us = np.concatenate([np.sin(th), -np.cos(th)], axis=2)
    us[0, 0, :S] = (-1.0) ** np.arange(S)
    us[0, 0, S:] = 0.0
    return f1, np.concatenate([ur, us], axis=1)


@functools.lru_cache(maxsize=None)
def _hy_inv_tabs_np(L, S):
    R, Q, D = 2 * L // S, L // S, S // 2
    cc, f, d = np.arange(R)[:, None, None], np.arange(S)[None, :, None], np.arange(D)[None, None, :]
    k = cc + R * d
    be = np.pi * ((k * f) % (2 * L)) / L
    w = np.where(k == 0, 1.0, 2.0) / (2.0 * L) + 0.0 * be
    bc = np.concatenate([w * np.cos(be), w * np.sin(be)], axis=2)
    bs = np.concatenate([-w * np.sin(be), w * np.cos(be)], axis=2)
    bc[0, :, D] = (-1.0) ** np.arange(S) / (2.0 * L)
    bs[0, :, D] = 0.0
    e, c = np.arange(Q)[:, None], np.arange(R)[None, :]
    al = 2 * np.pi * ((e * c) % R) / R
    return np.concatenate([bc, bs], axis=1), np.concatenate([np.cos(al), np.sin(al)], axis=1)


@functools.lru_cache(maxsize=None)
def _four_tabs_np(L, S):
    R = L // S
    c, a = np.arange(R)[:, None], np.arange(R)[None, :]
    ang = 2 * np.pi * ((c * a) % R) / R
    f1 = np.concatenate([np.concatenate([np.cos(ang), -np.sin(ang)], axis=1),
                         np.concatenate([-np.sin(ang), -np.cos(ang)], axis=1)], axis=0)
    cc, d, b = np.arange(R)[:, None, None], np.arange(S)[None, :, None], np.arange(S)[None, None, :]
    th = 2 * np.pi * (((cc + R * d) * b) % L) / L
    return f1, np.concatenate([np.cos(th), np.sin(th)], axis=2)


def _rms(x, g):
    return x * lax.rsqrt(jnp.mean(x * x, axis=-1, keepdims=True) + EPS) * g


def _silu(x):
    return x * jax.nn.sigmoid(x)


def _bdot(a, b):
    return jnp.dot(a.astype(BF16), b.astype(BF16), preferred_element_type=F32)


def _split(a):
    hi = a.astype(BF16)
    lo = (a - hi.astype(F32)).astype(BF16)
    return hi, lo


def _dot3(a, b):
    ah, al = _split(a)
    bh, bl = _split(b)
    d = lambda x, y: jnp.dot(x, y, preferred_element_type=F32)
    return d(ah, bh) + (d(al, bh) + d(ah, bl))


def _store_kv(kv, kr_pad, k_ref, v_ref):
    wk = MLA_HEADS * MLA_HP
    k_ref[...] = (kv[:, :wk] + jnp.concatenate([kr_pad] * MLA_HEADS, axis=1)).astype(k_ref.dtype)
    lane = lax.broadcasted_iota(jnp.int32, (1, wk), 1)
    ones = jnp.where(lane % MLA_HP >= MLA_V, 1.0, 0.0)
    v_ref[...] = (kv[:, wk:] + ones).astype(v_ref.dtype)


def _rope(x, cos, sin_signed, quarter):
    w = x.shape[-1]
    lane = lax.broadcasted_iota(jnp.int32, x.shape, 1)
    first = (lane % (2 * quarter)) < quarter
    up = pltpu.roll(x, w - quarter, axis=1)
    dn = pltpu.roll(x, quarter, axis=1)
    return x * cos + jnp.where(first, up, dn) * sin_signed


def _ada_kernel(c_ref, w_ref, b_ref, o_ref):
    s = _silu(c_ref[...])
    o_ref[...] = _bdot(s, w_ref[...]) + b_ref[...]


def _ada(cvec, w_ada, b_ada):
    tn = 1024
    n6 = w_ada.shape[-1]
    return pl.pallas_call(
        _ada_kernel,
        grid=(DEPTH, n6 // tn),
        in_specs=[
            pl.BlockSpec((MOD_ROWS, D_MODEL), lambda l, j: (0, 0)),
            pl.BlockSpec((None, D_MODEL, tn), lambda l, j: (l, 0, j)),
            pl.BlockSpec((None, 1, tn), lambda l, j: (l, 0, j)),
        ],
        out_specs=pl.BlockSpec((None, MOD_ROWS, tn), lambda l, j: (l, 0, j)),
        out_shape=jax.ShapeDtypeStruct((DEPTH, MOD_ROWS, n6), F32),
        compiler_params=_cparams(("parallel", "parallel")),
        name="ada",
    )(cvec, w_ada, b_ada.reshape(DEPTH, 1, n6))


def _in_proj_kernel(*refs, latent):
    (x_ref, mod_ref, g0_ref, w_ref, qn_ref, kvn_ref, wuq_ref, wukv_ref, bd_ref) = refs[:9]
    pos = 9
    if latent:
        cr_ref, sr_ref, cm_ref, sm_ref = refs[pos:pos + 4]
        pos += 4
        qkvg_ref, fx_ref, q_ref, k_ref, v_ref, hu_ref = refs[pos:pos + 6]
        lanes = lambda ref, rs, reps: jnp.concatenate([ref[rs, :]] * reps, axis=1)
    else:
        qkvg_ref, fx_ref, q_ref, k_ref, v_ref, hu_ref, ckv_ref, kr_ref = refs[pos:pos + 8]

    mod = mod_ref[...]
    sh1 = mod[:, 0:D_MODEL]
    sc1 = mod[:, D_MODEL:2 * D_MODEL]

    def project(rs):
        h = _rms(x_ref[rs, :], g0_ref[0:1, :]) * (1.0 + sc1) + sh1
        return jnp.dot(h.astype(BF16), w_ref[...], preferred_element_type=F32)

    def branches(rs, proj):
        rq = proj[:, OFF_RET:OFF_RET + RET_W]
        rk = proj[:, OFF_RET + RET_W:OFF_RET + 2 * RET_W] * (RET_DK ** -0.5)
        if latent:
            cos_r, sin_r = lanes(cr_ref, rs, RET_W // 128), lanes(sr_ref, rs, RET_W // 128)
            rq = _rope(rq, cos_r, sin_r, RET_DK // 4)
            rk = _rope(rk, cos_r, sin_r, RET_DK // 4)
        qkvg_ref[rs, 0:RET_W] = rq
        qkvg_ref[rs, RET_W:2 * RET_W] = rk
        qkvg_ref[rs, 2 * RET_W:4 * RET_W] = proj[:, OFF_RET + 2 * RET_W:OFF_RET + 4 * RET_W]

        fcs = _bdot(proj[:, OFF_FU:OFF_FU + FNET_W], bd_ref[...])
        fx_ref[0, rs, :] = fcs[:, :FNET_W].astype(fx_ref.dtype)
        fx_ref[1, rs, :] = fcs[:, FNET_W:].astype(fx_ref.dtype)

        cqn = _rms(proj[:, OFF_CQ:OFF_CQ + MLA_Q_LORA], qn_ref[...])
        q = _bdot(cqn, wuq_ref[...])
        ckvn = _rms(proj[:, OFF_CKV:OFF_CKV + MLA_KV_LORA], kvn_ref[...])
        kv = _bdot(ckvn, wukv_ref[...])
        krp = proj[:, OFF_KR:OFF_KR + MLA_HP]
        if latent:
            q = _rope(q, lanes(cm_ref, rs, MLA_HEADS), lanes(sm_ref, rs, MLA_HEADS), MLA_ROPE // 4)
            krp = _rope(krp, cm_ref[rs, :], sm_ref[rs, :], MLA_ROPE // 4)
        else:
            ckv_ref[rs, :] = ckvn
            kr_ref[rs, :] = krp
        q_ref[rs, :] = (q * ((MLA_NOPE + MLA_ROPE) ** -0.5 * math.log2(math.e))).astype(q_ref.dtype)
        _store_kv(kv, krp, k_ref.at[rs], v_ref.at[rs])

        hu_ref[rs, :] = proj[:, OFF_HU:OFF_HU + 3 * HY_CH]

    tm = x_ref.shape[0]
    groups = [slice(g * IN_GROUP, (g + 1) * IN_GROUP) for g in range(max(tm // IN_GROUP, 1))]
    projs = [project(rs) for rs in groups]
    for rs, proj in zip(groups, projs):
        branches(rs, proj)


def _in_proj(x, mod, l, p, latent):
    B, L, _ = x.shape
    t = _tiles(L)
    tm = t["tm"]
    nt = L // tm
    row = (lambda b: b) if latent else (lambda b: CTX_ROW)
    full = lambda shape: pl.BlockSpec((None,) + shape, lambda i, b: (l,) + (0,) * len(shape),
                                      pipeline_mode=pl.Buffered(1))
    in_specs = [
        pl.BlockSpec((None, tm, D_MODEL), lambda i, b: (b, i, 0)),
        pl.BlockSpec((None, None, 1, 6 * D_MODEL), lambda i, b: (l, row(b), 0, 0)),
        full((4, D_MODEL)),
        full((D_MODEL, IN_WP)),
        full((1, MLA_Q_LORA)),
        full((1, MLA_KV_LORA)),
        full((MLA_Q_LORA, MLA_HEADS * MLA_HP)),
        full((MLA_KV_LORA, 2 * MLA_HEADS * MLA_HP)),
        pl.BlockSpec((FNET_W, 2 * FNET_W), lambda i, b: (0, 0), pipeline_mode=pl.Buffered(1)),
    ]
    args = [x, mod, p["norm_g"], p["w_in"], p["q_norm"], p["kv_norm"], p["w_uq"], p["w_ukv"],
            _bf16_const(_chan_dft_np())]
    if latent:
        tabs = (_rope_np(L, 128, 0, RET_DK, RET_DK)
                + _rope_np(L, MLA_HP, MLA_ROPE_OFF, MLA_HP, MLA_ROPE))
        for tab in tabs:
            in_specs.append(pl.BlockSpec((tm, tab.shape[1]), lambda i, b: (i, 0)))
            args.append(jnp.asarray(tab))
    bm = lambda w: pl.BlockSpec((None, tm, w), lambda i, b: (b, i, 0))
    sds = jax.ShapeDtypeStruct
    out_specs = [
        bm(4 * RET_W),
        pl.BlockSpec((2, tm, FNET_W), lambda i, b: (0, i, b)),
        bm(MLA_HEADS * MLA_HP), bm(MLA_HEADS * MLA_HP), bm(MLA_HEADS * MLA_HP),
        pl.BlockSpec((tm, 3 * HY_CH), lambda i, b: (i, b)),
    ]
    out_shape = [
        sds((B, L, 4 * RET_W), F32),
        sds((2, L, B * FNET_W), F32 if _use_fft(L) else BF16),
        sds((B, L, MLA_HEADS * MLA_HP), BF16),
        sds((B, L, MLA_HEADS * MLA_HP), BF16),
        sds((B, L, MLA_HEADS * MLA_HP), BF16),
        sds((L, B * 3 * HY_CH), F32),
    ]
    if not latent:
        out_specs += [bm(MLA_KV_LORA), bm(MLA_HP)]
        out_shape += [sds((B, L, MLA_KV_LORA), F32), sds((B, L, MLA_HP), F32)]
    outs = pl.pallas_call(
        functools.partial(_in_proj_kernel, latent=latent),
        grid=(nt, B), in_specs=in_specs, out_specs=out_specs, out_shape=out_shape,
        compiler_params=_cparams(("parallel", "parallel")),
        name="in_proj_lat" if latent else "in_proj_ctx",
    )(*args)
    return tuple(outs) + ((None, None) if latent else ())


def _ctx_kv_kernel(ckv_ref, kr_ref, w_ref, k_ref, v_ref):
    _store_kv(_bdot(ckv_ref[...], w_ref[...]), kr_ref[...], k_ref, v_ref)


def _ctx_kv(cache_ckv, krope_pad, l, w_ukv):
    B, _, P, _ = cache_ckv.shape
    return pl.pallas_call(
        _ctx_kv_kernel,
        grid=(B,),
        in_specs=[
            pl.BlockSpec((None, None, P, MLA_KV_LORA), lambda b: (b, l, 0, 0)),
            pl.BlockSpec((None, None, P, MLA_HP), lambda b: (b, l, 0, 0)),
            pl.BlockSpec((None,) + w_ukv.shape[1:], lambda b: (l, 0, 0)),
        ],
        out_specs=[
            pl.BlockSpec((None, P, MLA_HEADS * MLA_HP), lambda b: (b, 0, 0)),
            pl.BlockSpec((None, P, MLA_HEADS * MLA_HP), lambda b: (b, 0, 0)),
        ],
        out_shape=[
            jax.ShapeDtypeStruct((B, P, MLA_HEADS * MLA_HP), BF16),
            jax.ShapeDtypeStruct((B, P, MLA_HEADS * MLA_HP), BF16),
        ],
        compiler_params=_cparams(("parallel",)),
        name="ctx_kv",
    )(cache_ckv, krope_pad, w_ukv)


def _ret_kernel(*refs, L, C, has_s0, emit_state):
    q_ref, k_ref, v_ref, g_ref, dec_ref = refs[:5]
    pos = 5
    if has_s0:
        s0_ref = refs[pos]
        pos += 1
    o_ref = refs[pos]
    pos += 1
    if emit_state:
        sn_ref = refs[pos]
        pos += 1
    sf_ref, sb_ref, dm_ref, tab_ref = refs[pos:pos + 4]
    n = L // C
    H, W = RET_HEADS, RET_W

    d = dec_ref[0:2, :]
    lg = jnp.minimum(d, 0.0) - jnp.log1p(jnp.exp(-jnp.abs(d)))
    lf, lb = lg[0:1, :], lg[1:2, :]
    @pl.when(pl.program_id(0) == 0)
    def _():
        ii = lax.broadcasted_iota(jnp.int32, (C, C), 0)
        jj = lax.broadcasted_iota(jnp.int32, (C, C), 1)
        diff = (ii - jj).astype(F32)
        for h in range(H):
            lfh = lf[:, h * RET_DV:h * RET_DV + 1]
            lbh = lb[:, h * RET_DV:h * RET_DV + 1]
            dm_ref[h] = jnp.where(diff > 0, jnp.exp(diff * lfh),
                                  jnp.where(diff < 0, jnp.exp(-diff * lbh), 2.0))
        ci = lax.broadcasted_iota(jnp.int32, (C, W), 0).astype(F32)
        tab_ref[0] = jnp.exp((ci + 1.0) * lf)
        tab_ref[1] = jnp.exp((C - ci) * lb)
        tab_ref[2] = jnp.exp((C - 1.0 - ci) * lf)
        tab_ref[3] = jnp.exp(ci * lb)

    g_f = jnp.exp(C * lf)
    g_b = jnp.exp(C * lb)
    hr = lax.broadcasted_iota(jnp.int32, (W, W), 0) // RET_DK
    hc = lax.broadcasted_iota(jnp.int32, (W, W), 1) // RET_DV
    same_head = hr == hc
    avg = jnp.where(same_head, 1.0 / RET_DV, 0.0).astype(BF16)

    sf_ref[0] = jnp.zeros((W, W), F32)
    sb_ref[n] = jnp.zeros((W, W), F32)
    if has_s0:
        for h in range(H):
            blk = slice(h * RET_DK, (h + 1) * RET_DK)
            sf_ref[0, blk, blk] = s0_ref[0, h]
            sb_ref[n, blk, blk] = s0_ref[1, h]

    tdot = lambda a, b: lax.dot_general(a.astype(BF16), b.astype(BF16), (((0,), (0,)), ((), ())),
                                        preferred_element_type=F32)

    def states(j, carry):
        jb = n - 1 - j
        rf = pl.multiple_of(j * C, C)
        rb = pl.multiple_of(jb * C, C)
        upd = tdot(k_ref[pl.ds(rf, C), :] * tab_ref[2], v_ref[pl.ds(rf, C), :])
        sf_ref[j + 1] = g_f * sf_ref[j] + jnp.where(same_head, upd, 0.0)
        upd = tdot(k_ref[pl.ds(rb, C), :] * tab_ref[3], v_ref[pl.ds(rb, C), :])
        sb_ref[jb] = g_b * sb_ref[jb + 1] + jnp.where(same_head, upd, 0.0)
        return carry

    lax.fori_loop(0, n, states, 0, unroll=4 if n % 4 == 0 else 1)

    lane_head = lax.broadcasted_iota(jnp.int32, (C, W), 1) // RET_DV

    def group_mean(x):
        hi, lo = _split(x)
        return (jnp.dot(hi, avg, preferred_element_type=F32)
                + jnp.dot(lo, avg, preferred_element_type=F32))

    def chunk(j, carry):
        r0 = pl.multiple_of(j * C, C)
        q = q_ref[pl.ds(r0, C), :]
        qb = q.astype(BF16)
        kb = k_ref[pl.ds(r0, C), :].astype(BF16)
        vb = v_ref[pl.ds(r0, C), :].astype(BF16)
        o = tab_ref[0] * jnp.dot(qb, sf_ref[j].astype(BF16), preferred_element_type=F32)
        o = o + tab_ref[1] * jnp.dot(qb, sb_ref[j + 1].astype(BF16), preferred_element_type=F32)
        inner = jnp.zeros((C, W), F32)
        for h in range(H):
            in_head = lane_head == h
            qh = jnp.where(in_head, q, 0.0).astype(BF16)
            s = lax.dot_general(qh, kb, (((1,), (1,)), ((), ())), preferred_element_type=F32)
            oh = jnp.dot((s * dm_ref[h]).astype(BF16), vb, preferred_element_type=F32)
            inner = jnp.where(in_head, oh, inner)
        o = o + inner
        dlt = o - group_mean(o)
        var = group_mean(dlt * dlt)
        gate = g_ref[pl.ds(r0, C), :]
        o_ref[pl.ds(r0, C), :] = (_silu(gate) * (dlt * lax.rsqrt(var + EPS))).astype(o_ref.dtype)
        return carry

    lax.fori_loop(0, n, chunk, 0, unroll=4 if n % 4 == 0 else 1)

    if emit_state:
        for h in range(H):
            blk = slice(h * RET_DK, (h + 1) * RET_DK)
            sn_ref[0, h] = sf_ref[n, blk, blk]
            sn_ref[1, h] = sb_ref[0, blk, blk]


def _retention(qkvg, dec_rows, s0, l, emit_state):
    B, L, _ = qkvg.shape
    C = _tiles(L)["chunk"]
    n = L // C
    has_s0 = s0 is not None
    def colspec(j):
        return pl.BlockSpec((None, L, RET_W), lambda b: (b, 0, j))

    in_specs = [colspec(j) for j in range(4)]
    in_specs.append(pl.BlockSpec((None, 8, RET_W), lambda b: (l, 0, 0)))
    args = [qkvg, qkvg, qkvg, qkvg, dec_rows]
    st_block = (None, None, 2, RET_HEADS, RET_DK, RET_DV)
    if has_s0:
        in_specs.append(pl.BlockSpec(st_block, lambda b: (b, l, 0, 0, 0, 0)))
        args.append(s0)
    out_specs = [pl.BlockSpec((None, L, RET_W), lambda b: (b, 0, 0))]
    out_shape = [jax.ShapeDtypeStruct((B, L, RET_W), BF16)]
    if emit_state:
        out_specs.append(pl.BlockSpec((None, 2, RET_HEADS, RET_DK, RET_DV), lambda b: (b, 0, 0, 0, 0)))
        out_shape.append(jax.ShapeDtypeStruct((B, 2, RET_HEADS, RET_DK, RET_DV), F32))
    outs = pl.pallas_call(
        functools.partial(_ret_kernel, L=L, C=C, has_s0=has_s0, emit_state=emit_state),
        grid=(B,), in_specs=in_specs, out_specs=out_specs, out_shape=out_shape,
        scratch_shapes=[
            pltpu.VMEM((n + 1, RET_W, RET_W), F32),
            pltpu.VMEM((n + 1, RET_W, RET_W), F32),
            pltpu.VMEM((RET_HEADS, C, C), F32),
            pltpu.VMEM((4, C, RET_W), F32),
        ],
        compiler_params=_cparams(("arbitrary",)),
        name="retention_lat" if has_s0 else "retention_ctx",
    )(*args)
    return outs if emit_state else (outs[0], None)


def _mm_kernel(a_ref, b_ref, o_ref, *, scale):
    o_ref[...] = (jnp.dot(a_ref[...], b_ref[...], preferred_element_type=F32) * scale).astype(o_ref.dtype)


def _mm(a, b, scale, out_dtype, tm, tn, name):
    M, K = a.shape
    _, N = b.shape
    tm, tn = min(tm, M), min(tn, N)
    return pl.pallas_call(
        functools.partial(_mm_kernel, scale=scale),
        grid=(N // tn, M // tm),
        in_specs=[pl.BlockSpec((tm, K), lambda j, i: (i, 0)),
                  pl.BlockSpec((K, tn), lambda j, i: (0, j))],
        out_specs=pl.BlockSpec((tm, tn), lambda j, i: (i, j)),
        out_shape=jax.ShapeDtypeStruct((M, N), out_dtype),
        compiler_params=_cparams(("parallel", "parallel")),
        name=name,
    )(a, b)


def _attn_kernel(*refs, has_ctx):
    if has_ctx:
        q_ref, k_ref, v_ref, kc_ref, vc_ref, o_ref, s_ref = refs
        srcs = ((k_ref, v_ref), (kc_ref, vc_ref))
    else:
        q_ref, k_ref, v_ref, o_ref, s_ref = refs
        srcs = ((k_ref, v_ref),)
    nt = lambda a, b: lax.dot_general(a, b, (((1,), (1,)), ((), ())), preferred_element_type=F32)
    chunks = [(kr, vr, c0) for kr, vr in srcs for c0 in range(0, kr.shape[0], ATTN_CK)]
    def scores(h):
        hq = slice(h * MLA_HP, (h + 1) * MLA_HP)
        qh = q_ref[:, hq]
        mx = None
        for n, (kr, _, c0) in enumerate(chunks):
            s = nt(qh, kr[c0:c0 + ATTN_CK, hq])
            s_ref[h % 2, :, n * ATTN_CK:(n + 1) * ATTN_CK] = s
            for g in range(ATTN_CK // 128):
                part = s[:, g * 128:(g + 1) * 128]
                mx = part if mx is None else jnp.maximum(mx, part)
        return jnp.max(mx, axis=-1, keepdims=True)

    def weighted_values(h, m):
        hq = slice(h * MLA_HP, (h + 1) * MLA_HP)
        acc = None
        for n, (_, vr, c0) in enumerate(chunks):
            p = jnp.exp2(s_ref[h % 2, :, n * ATTN_CK:(n + 1) * ATTN_CK] - m).astype(BF16)
            d = jnp.dot(p, vr[c0:c0 + ATTN_CK, hq], preferred_element_type=F32)
            acc = d if acc is None else acc + d
        o_ref[:, h * MLA_V:(h + 1) * MLA_V] = (acc[:, :MLA_V] / acc[:, MLA_V:2 * MLA_V]).astype(o_ref.dtype)

    m = scores(0)
    for h in range(MLA_HEADS):
        m_next = scores(h + 1) if h + 1 < MLA_HEADS else None
        weighted_values(h, m)
        m = m_next


def _attention(q, k, v, kc, vc):
    B, L, _ = q.shape
    tq = _tiles(L)["tq"]
    has_ctx = kc is not None
    whole = lambda a: pl.BlockSpec((None,) + a.shape[1:], lambda b, i: (b, 0, 0))
    in_specs = [pl.BlockSpec((None, tq, q.shape[2]), lambda b, i: (b, i, 0)), whole(k), whole(v)]
    args = [q, k, v]
    n_keys = L
    if has_ctx:
        in_specs += [whole(kc), whole(vc)]
        args += [kc, vc]
        n_keys += kc.shape[1]
    return pl.pallas_call(
        functools.partial(_attn_kernel, has_ctx=has_ctx),
        grid=(B, L // tq), in_specs=in_specs,
        out_specs=pl.BlockSpec((None, tq, MLA_W), lambda b, i: (b, i, 0)),
        out_shape=jax.ShapeDtypeStruct((B, L, MLA_W), BF16),
        scratch_shapes=[pltpu.VMEM((2, tq, n_keys), F32)],
        compiler_params=_cparams(("parallel", "parallel")),
        name="attention_lat" if has_ctx else "attention_ctx",
    )(*args)


def _hy_filter_kernel(z_ref, w1_ref, b1_ref, w2_ref, b2_ref, w3_ref, win_ref, x_ref):
    h1 = jnp.sin(_dot3(z_ref[...], w1_ref[...]) + b1_ref[...])
    h2 = jnp.sin(_dot3(h1, w2_ref[...]) + b2_ref[...])
    w3 = w3_ref[...]
    h = jnp.concatenate([_dot3(h2[:, :HY_FFN], w3), _dot3(h2[:, HY_FFN:], w3)], axis=0)
    win = win_ref[...]
    row = lax.broadcasted_iota(jnp.int32, win.shape, 0)
    for o in range(HY_ORDER):
        base = o * 2 * HY_CH
        hf = h[:, base:base + HY_CH] * win
        hb = h[:, base + HY_CH:base + 2 * HY_CH] * win
        hb = jnp.where(row == 0, 0.0, hb)
        nrm = jnp.sum(jnp.abs(hf), axis=0, keepdims=True) + jnp.sum(jnp.abs(hb), axis=0, keepdims=True) + EPS
        inv = 1.0 / nrm
        x_ref[:, base:base + HY_CH] = (hf + hb) * inv
        x_ref[:, base + HY_CH:base + 2 * HY_CH] = (hf - hb) * inv


def _hy_filter(L, l, p):
    z, win = _hyena_feat_np(L)
    z2 = np.concatenate([z[:L // 2], z[L // 2:]], axis=1)
    full = lambda shape: pl.BlockSpec(shape, lambda i: (0,) * len(shape))
    layer = lambda shape: pl.BlockSpec((None,) + shape, lambda i: (l,) + (0,) * len(shape))
    return pl.pallas_call(
        _hy_filter_kernel,
        grid=(1,),
        in_specs=[full((L // 2, 2 * HY_EMB_PAD)), layer((2 * HY_EMB_PAD, 2 * HY_FFN)), layer((1, 2 * HY_FFN)),
                  layer((2 * HY_FFN, 2 * HY_FFN)), layer((1, 2 * HY_FFN)),
                  layer((HY_FFN, HY_ORDER * 2 * HY_CH)), full((L, HY_CH))],
        out_specs=full((L, HY_ORDER * 2 * HY_CH)),
        out_shape=jax.ShapeDtypeStruct((L, HY_ORDER * 2 * HY_CH), F32),
        compiler_params=_cparams(("arbitrary",)),
        name="hy_filter",
    )(jnp.asarray(z2), p["hy_w1"], p["hy_b1"], p["hy_w2"], p["hy_b2"], p["hy_w3"], jnp.asarray(win))


def _tile_rows(ref, lead, rows):
    return jnp.concatenate([ref[lead, r * SUBL:(r + 1) * SUBL, :] for r in rows], axis=0)


def _fft_fwd_kernel(*refs, Q, R, S, filt):
    if filt:
        u_ref, k1_ref, e_ref, y_ref, a_scr = refs
    else:
        u_ref, k1_ref, e_ref, ga_ref, gb_ref, ny_ref, y_ref, a_scr = refs
    D = S // 2
    nb = S // SUBL
    tn = u_ref.shape[-1]
    dot = lambda p, q: jnp.dot(p, q, preferred_element_type=F32)
    k1 = k1_ref[...]
    for bh in range(nb):
        x = u_ref[:, bh].reshape(Q * SUBL, tn)
        if filt:
            xh, xl = _split(x)
            a_scr[bh] = dot(k1, xh) + dot(k1, xl)
        else:
            a_scr[bh] = dot(k1, x.astype(BF16))
    for c in range(R):
        rhs = jnp.concatenate([_tile_rows(a_scr, bh, (part * R + c,)) for part in (0, 1) for bh in range(nb)],
                              axis=0)
        if filt:
            rh, rl = _split(rhs)
            y_ref[c] = dot(e_ref[c], rh) + dot(e_ref[c], rl)
            continue
        y = dot(e_ref[c], rhs.astype(BF16))
        ur, us = y[:D], y[D:]
        ga, gb = ga_ref[c], gb_ref[c]
        gd = ga
        if c == 0:
            row0 = lax.broadcasted_iota(jnp.int32, ga.shape, 0) == 0
            gb = jnp.where(row0, 0.0, gb)
            gd = jnp.where(row0, jnp.broadcast_to(ny_ref[0, 0:1, :], ga.shape), ga)
        y_ref[c, :D, :] = (ur * ga - us * gb).astype(y_ref.dtype)
        y_ref[c, D:, :] = (ur * gb + us * gd).astype(y_ref.dtype)


def _fft_filter(taps, L):
    S = _hyena_s(L)
    R, Q = 2 * L // S, L // S
    N = taps.shape[1]
    tn = FFT_TN
    nb = S // SUBL
    f1_np, e_np = _hy_fwd_tabs_np(L, S)
    const = lambda shape: pl.BlockSpec(shape, lambda j: (0,) * len(shape), pipeline_mode=pl.Buffered(1))
    return pl.pallas_call(
        functools.partial(_fft_fwd_kernel, Q=Q, R=R, S=S, filt=True),
        grid=(N // tn,),
        in_specs=[pl.BlockSpec((Q, nb, SUBL, tn), lambda j: (0, 0, 0, j)),
                  const((2 * R * SUBL, Q * SUBL)), const((R, S, 2 * S))],
        out_specs=pl.BlockSpec((R, S, tn), lambda j: (0, 0, j)),
        out_shape=jax.ShapeDtypeStruct((R, S, N), F32),
        scratch_shapes=[pltpu.VMEM((nb, 2 * R * SUBL, tn), F32)],
        compiler_params=_cparams(("parallel",)),
        name="fft_filter",
    )(taps.reshape(Q, nb, SUBL, N), _bf16_const(np.kron(f1_np, np.eye(SUBL))), _bf16_const(e_np))


def _fft_inv_kernel(y_ref, e_ref, k2_ref, gate_ref, z_ref, d_ref, o_ref, scr, *, Q, R, S):
    dot = lambda p, q: jnp.dot(p, q, preferred_element_type=F32)
    tn = y_ref.shape[-1]
    per = scr.shape[1] // (2 * S)
    for c in range(R):
        scr[c // per, (c % per) * 2 * S:(c % per + 1) * 2 * S, :] = dot(e_ref[c], y_ref[c])
    k2 = k2_ref[...]
    d = d_ref[...]
    for fh in range(S // SUBL):
        rhs = jnp.concatenate([_tile_rows(scr, c // per, ((c % per) * 2 * S // SUBL + part * (S // SUBL) + fh,))
                               for part in (0, 1) for c in range(R)], axis=0).astype(BF16)
        conv = dot(k2, rhs).reshape(Q, SUBL, tn)
        o_ref[:, fh] = (gate_ref[:, fh] * (conv + d * z_ref[:, fh])).astype(o_ref.dtype)


def _hy_fused_kernel(hu_ref, w_ref, b_ref, k1_ref, e_ref, ei_ref, k2_ref, ga0, gb0, ny0, ga1, gb1, ny1,
                     bias_ref, o_ref, v4, x14, x24, z14, y_scr, scr, *, Q, R, S):
    L, tn = hu_ref.shape[0], HY_CH
    w, b = w_ref[...], b_ref[...]
    row = lax.broadcasted_iota(jnp.int32, (L, tn), 0)
    for g, dst in enumerate((v4, x14, x24)):
        cs = slice(g * tn, (g + 1) * tn)
        u = hu_ref[:, cs]
        prev = jnp.where(row == 0, 0.0, pltpu.roll(u, 1, axis=0))
        nxt = jnp.where(row == L - 1, 0.0, pltpu.roll(u, L - 1, axis=0))
        uc = prev * w[0:1, cs] + u * w[1:2, cs] + nxt * w[2:3, cs] + b[:, cs]
        dst[...] = uc.reshape(dst.shape)
    fwd = functools.partial(_fft_fwd_kernel, Q=Q, R=R, S=S, filt=False)
    inv = functools.partial(_fft_inv_kernel, Q=Q, R=R, S=S)
    fwd(v4, k1_ref, e_ref, ga0, gb0, ny0, y_scr, scr)
    inv(y_scr, ei_ref, k2_ref, x14, v4, bias_ref.at[0:1], z14, scr)
    fwd(z14, k1_ref, e_ref, ga1, gb1, ny1, y_scr, scr)
    inv(y_scr, ei_ref, k2_ref, x24, z14, bias_ref.at[1:2], o_ref, scr)


def _hy_fused(hu, B, spec, l, p):
    layer = lambda shape: pl.BlockSpec((None,) + shape, lambda j: (l,) + (0,) * len(shape),
                                       pipeline_mode=pl.Buffered(1))
    L = hu.shape[0]
    S = _hyena_s(L)
    R, Q, D = 2 * L // S, L // S, S // 2
    nb = S // SUBL
    tn = HY_CH
    f1_np, e_np = _hy_fwd_tabs_np(L, S)
    ei_np, f2_np = _hy_inv_tabs_np(L, S)
    const = lambda shape: pl.BlockSpec(shape, lambda j: (0,) * len(shape), pipeline_mode=pl.Buffered(1))
    one = lambda blk, idx: pl.BlockSpec(blk, lambda j: idx, pipeline_mode=pl.Buffered(1))
    g_specs = [s for o in range(HY_ORDER) for s in (one((R, D, tn), (0, 0, 2 * o)),
                                                    one((R, D, tn), (0, 1, 2 * o + 1)),
                                                    one((1, 8, tn), (0, D // 8, 2 * o)))]
    work = pltpu.VMEM((Q, nb, SUBL, tn), F32)
    out = pl.pallas_call(
        functools.partial(_hy_fused_kernel, Q=Q, R=R, S=S),
        grid=(B,),
        in_specs=[pl.BlockSpec((L, 3 * HY_CH), lambda j: (0, j)), layer((3, 3 * HY_CH)), layer((1, 3 * HY_CH)),
                  const((2 * R * SUBL, Q * SUBL)), const((R, S, 2 * S)), const((R, 2 * S, S)),
                  const((Q * SUBL, 2 * R * SUBL))] + g_specs + [layer((HY_ORDER, HY_CH))],
        out_specs=pl.BlockSpec((Q, nb, SUBL, tn), lambda j: (0, 0, 0, j)),
        out_shape=jax.ShapeDtypeStruct((Q, nb, SUBL, B * HY_CH), BF16),
        scratch_shapes=[work, work, work, work, pltpu.VMEM((R, S, tn), BF16),
                        pltpu.VMEM((nb, 2 * R * SUBL, tn), F32)],
        compiler_params=_cparams(("parallel",)),
        name="hy_fused",
    )(hu, p["hy_short_w"], p["hy_short_b"], _bf16_const(np.kron(f1_np, np.eye(SUBL))), _bf16_const(e_np),
      _bf16_const(ei_np), _bf16_const(np.kron(f2_np, np.eye(SUBL))), *([spec] * 6), p["hy_bias"])
    return out.reshape(L, B * HY_CH)


def _fft_four_kernel(x_ref, k1_ref, e2_ref, o_ref, a_scr, *, R, S, scale):
    dot = lambda p, q: jnp.dot(p, q, preferred_element_type=F32)
    tn = x_ref.shape[-1]
    nb = S // SUBL
    k1 = k1_ref[...]
    for bh in range(nb):
        x = jnp.concatenate([x_ref[part, :, bh].reshape(R * SUBL, tn) for part in (0, 1)], axis=0)
        a_scr[bh] = dot(k1, x.astype(BF16))
    for ch in range(R // SUBL):
        rhs = jnp.concatenate([_tile_rows(a_scr, bh, (part * R + ch * SUBL + cl,))
                               for cl in range(SUBL) for part in (0, 1) for bh in range(nb)],
                              axis=0).astype(BF16)
        y = dot(e2_ref[ch], rhs)
        o_ref[:, ch] = (y.reshape(S, SUBL, tn) * scale).astype(o_ref.dtype)


def _fft_fourier(fx, L):
    S = FFT_S
    R = L // S
    N = fx.shape[2]
    tn = FFT_TN
    nb = S // SUBL
    f1_np, e_np = _four_tabs_np(L, S)
    e2 = np.zeros((R // SUBL, S, SUBL, SUBL, 2 * S))
    for cl in range(SUBL):
        e2[:, :, cl, cl, :] = e_np[cl::SUBL]
    e2 = e2.reshape(R // SUBL, S * SUBL, SUBL * 2 * S)
    const = lambda shape: pl.BlockSpec(shape, lambda j: (0,) * len(shape), pipeline_mode=pl.Buffered(1))
    out = pl.pallas_call(
        functools.partial(_fft_four_kernel, R=R, S=S, scale=(FNET_GC * L) ** -0.5),
        grid=(N // tn,),
        in_specs=[pl.BlockSpec((2, R, nb, SUBL, tn), lambda j: (0, 0, 0, 0, j)),
                  const((2 * R * SUBL, 2 * R * SUBL)), const(e2.shape)],
        out_specs=pl.BlockSpec((S, R // SUBL, SUBL, tn), lambda j: (0, 0, 0, j)),
        out_shape=jax.ShapeDtypeStruct((S, R // SUBL, SUBL, N), BF16),
        scratch_shapes=[pltpu.VMEM((nb, 2 * R * SUBL, tn), F32)],
        compiler_params=_cparams(("parallel",)),
        name="fft_fourier",
    )(fx.reshape(2, R, nb, SUBL, N), _bf16_const(np.kron(f1_np, np.eye(SUBL))), _bf16_const(e2))
    return out.reshape(L, N)


def _use_fft(L):
    return L // FFT_S >= SUBL


def _hyena_s(L):
    return 2 * FFT_S if (2 * L) % (2 * FFT_S * SUBL) == 0 else FFT_S


def _hyena(hu, B, l, p):
    L = hu.shape[0]
    return _hy_fused(hu, B, _fft_filter(_hy_filter(L, l, p), L), l, p)


def _mix_ffn_kernel(ret_ref, four_ref, att_ref, hy_ref, wo_ref, x_ref, mod_ref, g_ref,
                    wg_ref, wu_ref, wd_ref, o_ref, hm_ref):
    mod = mod_ref[...]
    g1 = mod[:, 2 * D_MODEL:3 * D_MODEL]
    sh2 = mod[:, 3 * D_MODEL:4 * D_MODEL]
    sc2 = mod[:, 4 * D_MODEL:5 * D_MODEL]
    g2 = mod[:, 5 * D_MODEL:6 * D_MODEL]
    g = g_ref[...]

    def mix(rows):
        d = lambda a, i: jnp.dot(a[rows, :], wo_ref[i * 256:(i + 1) * 256, :], preferred_element_type=F32)
        m = d(ret_ref, 0) + d(four_ref, 1) + d(att_ref, 2) + d(hy_ref, 3)
        x1 = x_ref[rows, :] + g1 * _rms(m, g[1:2])
        o_ref[rows, :] = x1
        return (_rms(x1, g[2:3]) * (1.0 + sc2) + sh2).astype(BF16)

    def hidden(rows, h2):
        for c in range(D_FF // FFN_CHUNK):
            cs = slice(c * FFN_CHUNK, (c + 1) * FFN_CHUNK)
            a = jnp.dot(h2, wg_ref[:, cs], preferred_element_type=F32)
            u = jnp.dot(h2, wu_ref[:, cs], preferred_element_type=F32)
            hm_ref[rows, cs] = (_silu(a) * u).astype(BF16)

    def down(rows):
        f = jnp.dot(hm_ref[rows, :], wd_ref[...], preferred_element_type=F32)
        o_ref[rows, :] = o_ref[rows, :] + g2 * _rms(f, g[3:4])

    half = o_ref.shape[0] // 2
    ra, rb = slice(0, half), slice(half, 2 * half)
    hidden(ra, mix(ra))
    h2b = mix(rb)
    down(ra)
    hidden(rb, h2b)
    down(rb)


def _mix_ffn(ret, four, att, hy, x, mod, l, p, latent):
    B, L, _ = x.shape
    tm = _tiles(L)["tm_ffn"]
    row = (lambda b: b) if latent else (lambda b: CTX_ROW)
    bm = lambda w: pl.BlockSpec((None, tm, w), lambda b, i: (b, i, 0))
    tmj = pl.BlockSpec((tm, 256), lambda b, i: (i, b))
    resident = lambda shape: pl.BlockSpec((None,) + shape, lambda b, i: (l, 0, 0), pipeline_mode=pl.Buffered(1))
    return pl.pallas_call(
        _mix_ffn_kernel,
        grid=(B, L // tm),
        in_specs=[bm(RET_W), tmj, bm(MLA_W), tmj,
                  resident((4 * 256, D_MODEL)),
                  bm(D_MODEL),
                  pl.BlockSpec((None, None, 1, 6 * D_MODEL), lambda b, i: (l, row(b), 0, 0)),
                  pl.BlockSpec((None, 4, D_MODEL), lambda b, i: (l, 0, 0)),
                  resident((D_MODEL, D_FF)), resident((D_MODEL, D_FF)), resident((D_FF, D_MODEL))],
        out_specs=bm(D_MODEL),
        out_shape=jax.ShapeDtypeStruct((B, L, D_MODEL), F32),
        scratch_shapes=[pltpu.VMEM((tm, D_FF), BF16)],
        compiler_params=_cparams(("parallel", "parallel")),
        name="mix_ffn",
    )(ret, four, att, hy, p["w_out"], x, mod, p["norm_g"], p["w_gate"], p["w_up"], p["w_down"])


def _layer(x, mod, l, p, latent, ctx_kv, s0):
    B, L, _ = x.shape
    t = _tiles(L)
    qkvg, fx, q, k, v, hu, ckv, krp = _in_proj(x, mod, l, p, latent)
    ret, s_new = _retention(qkvg, p["dec_rows"], s0, l, emit_state=not latent)
    if _use_fft(L):
        four = _fft_fourier(fx, L)
    else:
        four = _mm(_bf16_const(_pos_dft_np(L)), fx.reshape(2 * L, B * FNET_W),
                   (FNET_GC * L) ** -0.5, BF16, t["hb"], t["tcol"], "fourier_pos")
    kc, vc = ctx_kv if latent else (None, None)
    att = _attention(q, k, v, kc, vc)
    hy = _hyena(hu, B, l, p)
    return _mix_ffn(ret, four, att, hy, x, mod, l, p, latent), ckv, krp, s_new


def _pack_w_in_kernel(w_ref, o_ref):
    w = w_ref[...]
    o_ref[:, :OFF_KR] = w[:, :OFF_KR].astype(o_ref.dtype)
    tile = pltpu.roll(w[:, OFF_KR:OFF_KR + MLA_HP], MLA_ROPE_OFF, axis=1)
    lane = lax.broadcasted_iota(jnp.int32, tile.shape, 1)
    tile = jnp.where(lane >= MLA_ROPE_OFF, jnp.where(lane < MLA_ROPE_OFF + MLA_ROPE, tile, 0.0), 0.0)
    o_ref[:, OFF_KR:OFF_HU] = tile.astype(o_ref.dtype)
    o_ref[:, OFF_HU:] = w[:, OFF_KR + MLA_ROPE:].astype(o_ref.dtype)


def _pack_w_in(w_in):
    tr = 256
    return pl.pallas_call(
        _pack_w_in_kernel,
        grid=(DEPTH, D_MODEL // tr),
        in_specs=[pl.BlockSpec((None, tr, w_in.shape[-1]), lambda l, i: (l, i, 0))],
        out_specs=pl.BlockSpec((None, tr, IN_WP), lambda l, i: (l, i, 0)),
        out_shape=jax.ShapeDtypeStruct((DEPTH, D_MODEL, IN_WP), BF16),
        compiler_params=_cparams(("parallel", "parallel")),
        name="pack_w_in",
    )(w_in)


def _prep(w):
    last = lambda a, n: jnp.pad(a, [(0, 0)] * (a.ndim - 1) + [(0, n)])
    uq = w["mla_w_uq"].reshape(DEPTH, MLA_Q_LORA, MLA_HEADS, MLA_NOPE + MLA_ROPE)
    uq = last(uq, MLA_HP - MLA_NOPE - MLA_ROPE).reshape(DEPTH, MLA_Q_LORA, -1)
    ukv = w["mla_w_ukv"].reshape(DEPTH, MLA_KV_LORA, MLA_HEADS, MLA_NOPE + MLA_V)
    uk = last(ukv[..., :MLA_NOPE], MLA_HP - MLA_NOPE).reshape(DEPTH, MLA_KV_LORA, -1)
    uv = last(ukv[..., MLA_NOPE:], MLA_HP - MLA_V).reshape(DEPTH, MLA_KV_LORA, -1)
    two = lambda a: jnp.stack([jnp.kron(jnp.eye(2, dtype=F32), a[l]) for l in range(DEPTH)])
    hy_w1 = jnp.pad(w["hy_w1"], ((0, 0), (0, HY_EMB_PAD - HY_EMB), (0, 0)))
    return dict(
        norm_g=w["norm_g"], w_in=_pack_w_in(w["w_in"]), w_out=w["w_out"].astype(BF16),
        q_norm=w["mla_q_norm"][:, None], kv_norm=w["mla_kv_norm"][:, None],
        w_uq=uq.astype(BF16), w_ukv=jnp.concatenate([uk, uv], axis=-1).astype(BF16),
        dec_rows=w["dec_rows"],
        hy_short_w=w["hy_short_w"], hy_short_b=w["hy_short_b"][:, None],
        hy_w1=two(hy_w1), hy_b1=jnp.tile(w["hy_b1"][:, None], (1, 1, 2)),
        hy_w2=two(w["hy_w2"]), hy_b2=jnp.tile(w["hy_b2"][:, None], (1, 1, 2)), hy_w3=w["hy_w3"],
        hy_bias=w["hy_bias"],
        w_gate=w["w_gate"].astype(BF16), w_up=w["w_up"].astype(BF16), w_down=w["w_down"].astype(BF16),
    )


def kernel(x_prompt, x_sample, cache_ckv, cache_krope, state_ret, c, c_ctx, w_ada, b_ada, norm_g, w_in, w_out,
           ret_decay, mla_q_norm, mla_kv_norm, mla_w_uq, mla_w_ukv, hy_short_w, hy_short_b, hy_w1, hy_b1,
           hy_w2, hy_b2, hy_w3, hy_bias, w_gate, w_up, w_down):
    nb = c.shape[0]
    cvec = jnp.concatenate([c, c_ctx[None], jnp.zeros((MOD_ROWS - nb - 1, D_MODEL), F32)], axis=0)
    mod = _ada(cvec, w_ada, b_ada).reshape(DEPTH, MOD_ROWS, 1, 6 * D_MODEL)
    dec_rows = jnp.pad(jnp.repeat(ret_decay, RET_DV, axis=-1), ((0, 0), (0, 6), (0, 0)))
    krope_pad = jnp.pad(cache_krope, ((0, 0), (0, 0), (0, 0),
                                      (MLA_ROPE_OFF, MLA_HP - MLA_ROPE_OFF - MLA_ROPE)))
    w = dict(norm_g=norm_g, w_in=w_in, w_out=w_out, dec_rows=dec_rows, mla_q_norm=mla_q_norm,
             mla_kv_norm=mla_kv_norm, mla_w_uq=mla_w_uq, mla_w_ukv=mla_w_ukv, hy_short_w=hy_short_w,
             hy_short_b=hy_short_b, hy_w1=hy_w1, hy_b1=hy_b1, hy_w2=hy_w2, hy_b2=hy_b2, hy_w3=hy_w3,
             hy_bias=hy_bias, w_gate=w_gate, w_up=w_up, w_down=w_down)
    xp, xs = x_prompt, x_sample
    new_ckv, new_kr, new_s = [], [], []
    p = _prep(w)
    for l in range(DEPTH):
        xp, ckv, krp, s_ret = _layer(xp, mod, l, p, False, None, None)
        new_ckv.append(ckv)
        new_kr.append(krp[:, :, MLA_ROPE_OFF:MLA_ROPE_OFF + MLA_ROPE])
        new_s.append(s_ret)
        ctx_kv = _ctx_kv(cache_ckv, krope_pad, l, p["w_ukv"])
        xs, _, _, _ = _layer(xs, mod, l, p, True, ctx_kv, state_ret)
    return (xp, xs, jnp.stack(new_ckv, axis=1), jnp.stack(new_kr, axis=1), jnp.stack(new_s, axis=1))
```
